```python
import jax, jax.numpy as jnp
from jax import lax
import numpy as np

D_MODEL = 1024
BATCH = 2
SEQ = 8192
DEPTH = 1

HEAD_DIM = 64
ATTN_HEADS = 8
ATTN_W = ATTN_HEADS * HEAD_DIM
CONV_W = 256
MEM_HEADS = 4
MEM_W = MEM_HEADS * HEAD_DIM
MIX_W = ATTN_W + CONV_W + MEM_W
N_MEM = 256
DILATED_PATTERNS = ((128, 1), (512, 4), (2048, 16))
ROPE_THETA = 500000.0
ROT_DIM = HEAD_DIM // 4
CONV_K = 31
FFN_CONV_K = 3
D_FF = 2816
IN_COLS = 3 * ATTN_W + 2 * CONV_W + MEM_W
SPLITS = (ATTN_W, 2 * ATTN_W, 3 * ATTN_W, 3 * ATTN_W + 2 * CONV_W)
NORM_EPS = 1e-6
NEG_INF = -1e30

kernel_name = "hybrid_dilated_conformer_memory_block"


def rms_norm(x, g):
    xf = x.astype(jnp.float32)
    y = xf * lax.rsqrt(jnp.mean(xf * xf, axis=-1, keepdims=True) + NORM_EPS)
    return (y * g.astype(jnp.float32)).astype(x.dtype)


def layer_norm(x, g, b):
    xf = x.astype(jnp.float32)
    mu = jnp.mean(xf, axis=-1, keepdims=True)
    var = jnp.mean(jnp.square(xf - mu), axis=-1, keepdims=True)
    y = (xf - mu) * lax.rsqrt(var + NORM_EPS)
    return (y * g.astype(jnp.float32) + b.astype(jnp.float32)).astype(x.dtype)


def partial_rotary(x, positions):
    inv_freq = ROPE_THETA ** (-jnp.arange(0, ROT_DIM, 2, dtype=jnp.float32) / ROT_DIM)
    ang = positions.astype(jnp.float32)[..., None] * inv_freq
    cos = jnp.cos(ang)[:, :, None, :]
    sin = jnp.sin(ang)[:, :, None, :]
    xr = x[..., :ROT_DIM].astype(jnp.float32)
    x1, x2 = xr[..., : ROT_DIM // 2], xr[..., ROT_DIM // 2:]
    rot = jnp.concatenate([x1 * cos - x2 * sin, x2 * cos + x1 * sin], axis=-1)
    return jnp.concatenate([rot.astype(x.dtype), x[..., ROT_DIM:]], axis=-1)


def depthwise_conv(x, w, b):
    k = w.shape[0]
    y = lax.conv_general_dilated(
        x, w[:, None, :].astype(x.dtype), window_strides=(1,),
        padding=[(k // 2, k // 2)], dimension_numbers=("NWC", "WIO", "NWC"),
        feature_group_count=x.shape[-1])
    return y + b.astype(x.dtype)


def banded_attention(q, k, v, half):
    n, L, H, Dh = q.shape
    blk = half
    nb = -(-L // blk)
    Lp = nb * blk
    qb = jnp.pad(q, ((0, 0), (0, Lp - L), (0, 0), (0, 0))).reshape(n, nb, blk, H, Dh)

    def halo(t):
        tp = jnp.pad(t, ((0, 0), (blk, Lp - L + blk), (0, 0), (0, 0))).reshape(n, nb + 2, blk, H, Dh)
        return jnp.concatenate([tp[:, :-2], tp[:, 1:-1], tp[:, 2:]], axis=2)

    kb, vb = halo(k), halo(v)
    s = jnp.einsum("nbqhd,nbkhd->nbhqk", qb, kb).astype(jnp.float32) * (Dh ** -0.5)
    bidx = jnp.arange(nb)[:, None, None]
    qi = jnp.arange(blk)[None, :, None]
    kj = jnp.arange(3 * blk)[None, None, :]
    key_pos = bidx * blk - blk + kj
    valid = (jnp.abs(kj - blk - qi) <= half) & (key_pos >= 0) & (key_pos < L)
    s = jnp.where(valid[None, :, None], s, NEG_INF)
    m = jnp.max(s, axis=-1, keepdims=True)
    p = jnp.exp(s - m)
    den = jnp.sum(p, axis=-1, keepdims=True)
    o = jnp.einsum("nbhqk,nbkhd->nbqhd", p, vb.astype(jnp.float32))
    o = o / jnp.transpose(den, (0, 1, 3, 2, 4))
    lse = jnp.transpose((m + jnp.log(den))[..., 0], (0, 1, 3, 2))
    return o.reshape(n, Lp, H, Dh)[:, :L], lse.reshape(n, Lp, H)[:, :L]


def dilated_attention(q, k, v, window, dil):
    B, S, H, Dh = q.shape
    L = S // dil

    def to_sub(t):
        return t.reshape(B, L, dil, H, Dh).transpose(0, 2, 1, 3, 4).reshape(B * dil, L, H, Dh)

    o, lse = banded_attention(to_sub(q), to_sub(k), to_sub(v), window // (2 * dil))
    o = o.reshape(B, dil, L, H, Dh).transpose(0, 2, 1, 3, 4).reshape(B, S, H, Dh)
    lse = lse.reshape(B, dil, L, H).transpose(0, 2, 1, 3).reshape(B, S, H)
    return o, lse


def setup_inputs(seed: int = 0) -> dict:
    key = jax.random.key(seed)
    ks = jax.random.split(key, 24)
    f32 = jnp.float32

    def nrm(k, shape, scale):
        return jax.random.normal(k, shape, f32) * scale

    def gain(k, shape):
        return 1.0 + 0.02 * jax.random.normal(k, shape, f32)

    offsets = jax.random.randint(ks[2], (BATCH, 1), 0, 1024, dtype=jnp.int32)
    positions = (jnp.arange(SEQ, dtype=jnp.int32)[None, :] + offsets).astype(jnp.int32)
    return {
        "x": nrm(ks[0], (BATCH, SEQ, D_MODEL), 1.0),
        "mem": nrm(ks[1], (BATCH, N_MEM, D_MODEL), 1.0),
        "positions": positions,
        "mix_norm_g": gain(ks[3], (DEPTH, D_MODEL)),
        "mem_norm_g": gain(ks[4], (DEPTH, D_MODEL)),
        "w_in": nrm(ks[5], (DEPTH, D_MODEL, IN_COLS), D_MODEL ** -0.5),
        "w_mem_kv": nrm(ks[6], (DEPTH, D_MODEL, 2 * MEM_W), D_MODEL ** -0.5),
        "q_norm_g": gain(ks[7], (DEPTH, HEAD_DIM)),
        "k_norm_g": gain(ks[8], (DEPTH, HEAD_DIM)),
        "mq_norm_g": gain(ks[9], (DEPTH, HEAD_DIM)),
        "mk_norm_g": gain(ks[10], (DEPTH, HEAD_DIM)),
        "conv_dw_w": nrm(ks[11], (DEPTH, CONV_K, CONV_W), CONV_K ** -0.5),
        "conv_dw_b": nrm(ks[12], (DEPTH, CONV_W), 0.02),
        "conv_ln_g": gain(ks[13], (DEPTH, CONV_W)),
        "conv_ln_b": nrm(ks[14], (DEPTH, CONV_W), 0.02),
        "w_out": nrm(ks[15], (DEPTH, MIX_W, D_MODEL), MIX_W ** -0.5),
        "ffn_norm_g": gain(ks[16], (DEPTH, D_MODEL)),
        "w_up": nrm(ks[17], (DEPTH, D_MODEL, 2 * D_FF), D_MODEL ** -0.5),
        "ffn_dw_w": nrm(ks[18], (DEPTH, FFN_CONV_K, 2 * D_FF), FFN_CONV_K ** -0.5),
        "ffn_dw_b": nrm(ks[19], (DEPTH, 2 * D_FF), 0.02),
        "w_down": nrm(ks[20], (DEPTH, D_FF, D_MODEL), D_FF ** -0.5),
    }


def reference(x, mem, positions, mix_norm_g, mem_norm_g, w_in, w_mem_kv, q_norm_g, k_norm_g,
              mq_norm_g, mk_norm_g, conv_dw_w, conv_dw_b, conv_ln_g, conv_ln_b, w_out,
              ffn_norm_g, w_up, ffn_dw_w, ffn_dw_b, w_down):
    B, S, _ = x.shape
    h = x
    for l in range(DEPTH):
        hn = rms_norm(h, mix_norm_g[l])
        proj = hn @ w_in[l]
        q, k, v, glu, qm = jnp.split(proj, SPLITS, axis=-1)

        q = partial_rotary(rms_norm(q.reshape(B, S, ATTN_HEADS, HEAD_DIM), q_norm_g[l]), positions)
        k = partial_rotary(rms_norm(k.reshape(B, S, ATTN_HEADS, HEAD_DIM), k_norm_g[l]), positions)
        v = v.reshape(B, S, ATTN_HEADS, HEAD_DIM)
        outs, lses = [], []
        for window, dil in DILATED_PATTERNS:
            o, lse = dilated_attention(q, k, v, window, dil)
            outs.append(o)
            lses.append(lse)
        wts = jax.nn.softmax(jnp.stack(lses, axis=0), axis=0)
        attn = jnp.sum(wts[..., None] * jnp.stack(outs, axis=0), axis=0)
        attn = attn.astype(h.dtype).reshape(B, S, ATTN_W)

        ga, gg = jnp.split(glu, 2, axis=-1)
        c = ga * jax.nn.sigmoid(gg)
        c = depthwise_conv(c, conv_dw_w[l], conv_dw_b[l])
        c = jax.nn.silu(layer_norm(c, conv_ln_g[l], conv_ln_b[l]))

        qm = rms_norm(qm.reshape(B, S, MEM_HEADS, HEAD_DIM), mq_norm_g[l])
        kvm = rms_norm(mem, mem_norm_g[l]) @ w_mem_kv[l]
        km, vm = jnp.split(kvm, 2, axis=-1)
        km = rms_norm(km.reshape(B, N_MEM, MEM_HEADS, HEAD_DIM), mk_norm_g[l])
        vm = vm.reshape(B, N_MEM, MEM_HEADS, HEAD_DIM)
        sm = jnp.einsum("bshd,bmhd->bhsm", qm, km).astype(jnp.float32) * (HEAD_DIM ** -0.5)
        pm = jax.nn.softmax(sm, axis=-1)
        mo = jnp.einsum("bhsm,bmhd->bshd", pm, vm.astype(jnp.float32))
        mo = mo.astype(h.dtype).reshape(B, S, MEM_W)

        mixed = jnp.concatenate([attn, c, mo], axis=-1)
        h = h + mixed @ w_out[l]

        f = rms_norm(h, ffn_norm_g[l]) @ w_up[l]
        f = depthwise_conv(f, ffn_dw_w[l], ffn_dw_b[l])
        fg, fu = jnp.split(f, 2, axis=-1)
        h = h + (jax.nn.silu(fg) * fu) @ w_down[l]
    return h
```

```python
import functools

import jax
import jax.numpy as jnp
from jax import lax
from jax.experimental import pallas as pl
from jax.experimental.pallas import tpu as pltpu

F32 = jnp.float32
BF16 = jnp.bfloat16

D_MODEL = 1024
HEAD_DIM = 64
ATTN_HEADS = 8
ATTN_W = ATTN_HEADS * HEAD_DIM
CONV_W = 256
MEM_HEADS = 4
MEM_W = MEM_HEADS * HEAD_DIM
N_MEM = 256
DILATIONS = (16, 4, 1)
BAND_HALF = 64
ROPE_THETA = 500000.0
ROT_DIM = HEAD_DIM // 4
CONV_K = 31
FFN_CONV_K = 3
D_FF = 2816
NORM_EPS = 1e-6
NEG_INF = -1e30
SM_SCALE = HEAD_DIM ** -0.5

LANES = 128
MXU_DIM = 256
BF16_ROWS = 16
VMEM_LIMIT = 56 * 1024 * 1024

TM = 512
TL = 128
TK = TL + 2 * BAND_HALF
FF_CH = MXU_DIM
HALO = BF16_ROWS


def _cparams(n_axes):
    return pltpu.CompilerParams(dimension_semantics=("parallel",) * n_axes,
                                vmem_limit_bytes=VMEM_LIMIT)


def _const_spec(shape):
    return pl.BlockSpec(shape, lambda *_: (0,) * len(shape), pipeline_mode=pl.Buffered(1))


def _rms_rows(x, g):
    r = lax.rsqrt(jnp.mean(x * x, axis=-1, keepdims=True) + NORM_EPS)
    return x * r * g


def _head_norm(t, gain, gsum):
    outs = []
    for c in range(t.shape[1] // MXU_DIM):
        tc = t[:, c * MXU_DIM:(c + 1) * MXU_DIM]
        ssum = jnp.dot((tc * tc).astype(BF16), gsum, preferred_element_type=F32)
        r = lax.rsqrt(ssum * (1.0 / HEAD_DIM) + NORM_EPS)
        outs.append(tc * r * gain[:, c * MXU_DIM:(c + 1) * MXU_DIM])
    return outs


def _sigmoid(x):
    return 1.0 / (1.0 + jnp.exp(-x))


def _in_proj_kernel(x_ref, pos_ref, g_ref, w_ref, gq_ref, gk_ref, gm_ref, gsum_ref, rope_ref,
                    q_ref, k_ref, v_ref, c_ref, qm_ref):
    hn = _rms_rows(x_ref[0], g_ref[...]).astype(BF16)

    ang = pos_ref[0].astype(F32) * rope_ref[0:1, :]
    cosv = jnp.cos(ang)
    sinv = jnp.sin(ang) * rope_ref[1:2, :]
    first_half = rope_ref[2:3, :] > 0.5

    def rotary_store(chunks, scale, out_ref):
        for c, t in enumerate(chunks):
            for s in range(MXU_DIM // LANES):
                xc = t[:, s * LANES:(s + 1) * LANES]
                partner = jnp.where(first_half, pltpu.roll(xc, LANES - ROT_DIM // 2, 1),
                                    pltpu.roll(xc, ROT_DIM // 2, 1))
                rot = xc * cosv + partner * sinv
                lo = c * MXU_DIM + s * LANES
                out_ref[0, :, lo:lo + LANES] = (rot * scale).astype(BF16)

    gsum = gsum_ref[...]
    q = jnp.dot(hn, w_ref[:, 0:ATTN_W], preferred_element_type=F32)
    rotary_store(_head_norm(q, gq_ref[...], gsum), SM_SCALE, q_ref)
    k = jnp.dot(hn, w_ref[:, ATTN_W:2 * ATTN_W], preferred_element_type=F32)
    rotary_store(_head_norm(k, gk_ref[...], gsum), 1.0, k_ref)
    v_ref[0] = jnp.dot(hn, w_ref[:, 2 * ATTN_W:3 * ATTN_W],
                       preferred_element_type=F32).astype(BF16)
    glu = jnp.dot(hn, w_ref[:, 3 * ATTN_W:3 * ATTN_W + 2 * CONV_W], preferred_element_type=F32)
    c_ref[0] = glu[:, :CONV_W] * _sigmoid(glu[:, CONV_W:])
    qm = jnp.dot(hn, w_ref[:, 3 * ATTN_W + 2 * CONV_W:], preferred_element_type=F32)
    (qmn,) = _head_norm(qm, gm_ref[...], gsum)
    qm_ref[0] = (qmn * SM_SCALE).astype(BF16)


def _in_proj(x, pos3, g, w_in, gq, gk, gm, gsum, rope):
    B, S, _ = x.shape
    in_cols = w_in.shape[1]
    tok = lambda w: pl.BlockSpec((1, TM, w), lambda b, t: (b, t, 0))
    return pl.pallas_call(
        _in_proj_kernel,
        grid=(B, S // TM),
        in_specs=[tok(D_MODEL), tok(1), _const_spec((1, D_MODEL)), _const_spec((D_MODEL, in_cols)),
                  _const_spec((1, ATTN_W)), _const_spec((1, ATTN_W)), _const_spec((1, MEM_W)),
                  _const_spec((MXU_DIM, MXU_DIM)), _const_spec((8, LANES))],
        out_specs=[tok(ATTN_W), tok(ATTN_W), tok(ATTN_W), tok(CONV_W), tok(MEM_W)],
        out_shape=[jax.ShapeDtypeStruct((B, S, ATTN_W), BF16)] * 3
        + [jax.ShapeDtypeStruct((B, S, CONV_W), F32), jax.ShapeDtypeStruct((B, S, MEM_W), BF16)],
        compiler_params=_cparams(2),
        name="in_proj",
    )(x, pos3, g, w_in, gq, gk, gm, gsum, rope)


def _mem_kv_kernel(mem_ref, g_ref, w_ref, gk_ref, gsum_ref, kmt_ref, vm_ref):
    mn = _rms_rows(mem_ref[0], g_ref[...]).astype(BF16)
    kv = jnp.dot(mn, w_ref[...], preferred_element_type=F32)
    (km,) = _head_norm(kv[:, :MEM_W], gk_ref[...], gsum_ref[...])
    kmt_ref[0] = km.T.astype(BF16)
    vm_ref[0] = kv[:, MEM_W:].astype(BF16)


def _mem_kv(mem, g, w, gk, gsum):
    B = mem.shape[0]
    return pl.pallas_call(
        _mem_kv_kernel,
        grid=(B,),
        in_specs=[pl.BlockSpec((1, N_MEM, D_MODEL), lambda b: (b, 0, 0)),
                  _const_spec((1, D_MODEL)), _const_spec((D_MODEL, 2 * MEM_W)),
                  _const_spec((1, MEM_W)), _const_spec((MXU_DIM, MXU_DIM))],
        out_specs=[pl.BlockSpec((1, MEM_W, N_MEM), lambda b: (b, 0, 0)),
                   pl.BlockSpec((1, N_MEM, MEM_W), lambda b: (b, 0, 0))],
        out_shape=[jax.ShapeDtypeStruct((B, MEM_W, N_MEM), BF16),
                   jax.ShapeDtypeStruct((B, N_MEM, MEM_W), BF16)],
        compiler_params=_cparams(1),
        name="mem_kv",
    )(mem, g, w, gk, gsum)


def _head_masks():
    lane = lax.broadcasted_iota(jnp.int32, (1, LANES), 1)
    low = lane < HEAD_DIM
    return low, (jnp.where(low, 1.0, 0.0).astype(BF16), jnp.where(low, 0.0, 1.0).astype(BF16))


def _mem_attn_kernel(qm_ref, kmt_ref, vm_ref, o_ref):
    low, hmask = _head_masks()
    for c in range(MEM_W // LANES):
        sl = slice(c * LANES, (c + 1) * LANES)
        qc = qm_ref[0, :, sl]
        halves = []
        for e in range(2):
            s = jnp.dot(qc * hmask[e], kmt_ref[0, sl, :], preferred_element_type=F32)
            m = jnp.max(s, axis=-1, keepdims=True)
            p = jnp.exp(s - m)
            l = jnp.sum(p, axis=-1, keepdims=True)
            o = jnp.dot(p.astype(BF16), vm_ref[0, :, sl], preferred_element_type=F32)
            halves.append(o / l)
        o_ref[0, :, sl] = jnp.where(low, halves[0], halves[1]).astype(BF16)


def _mem_attn(qm, kmt, vm):
    B, S, _ = qm.shape
    return pl.pallas_call(
        _mem_attn_kernel,
        grid=(B, S // TM),
        in_specs=[pl.BlockSpec((1, TM, MEM_W), lambda b, t: (b, t, 0)),
                  pl.BlockSpec((1, MEM_W, N_MEM), lambda b, t: (b, 0, 0)),
                  pl.BlockSpec((1, N_MEM, MEM_W), lambda b, t: (b, 0, 0))],
        out_specs=pl.BlockSpec((1, TM, MEM_W), lambda b, t: (b, t, 0)),
        out_shape=jax.ShapeDtypeStruct((B, S, MEM_W), BF16),
        compiler_params=_cparams(2),
        name="mem_attn",
    )(qm, kmt, vm)


LSE_LANES = LANES // ATTN_HEADS


def _dil_attn_kernel(*refs, sub_len, merge):
    if merge:
        q_ref, k_ref, v_ref, oa_ref, la_ref, ob_ref, lb_ref, o_ref = refs
    else:
        q_ref, k_ref, v_ref, o_ref, lse_ref = refs
    l0 = pl.program_id(2) * TL
    start = pl.multiple_of(jnp.clip(l0 - BAND_HALF, 0, sub_len - TK), BAND_HALF)
    off = l0 - start
    qi = lax.broadcasted_iota(jnp.int32, (TL, TK), 0)
    kj = lax.broadcasted_iota(jnp.int32, (TL, TK), 1)
    valid = jnp.abs(kj - qi - off) <= BAND_HALF
    low, hmask = _head_masks()
    lane_head = lax.broadcasted_iota(jnp.int32, (1, LANES), 1) // LSE_LANES

    lse_tile = jnp.zeros((TL, LANES), F32)
    for c in range(ATTN_W // LANES):
        sl = slice(c * LANES, (c + 1) * LANES)
        qc = q_ref[0, :, sl]
        kc = k_ref[0, pl.ds(start, TK), sl]
        vc = v_ref[0, pl.ds(start, TK), sl]
        halves, wa, wb = [], [], []
        for e in range(2):
            h = 2 * c + e
            s = lax.dot_general(qc * hmask[e], kc, (((1,), (1,)), ((), ())),
                                preferred_element_type=F32)
            s = jnp.where(valid, s, NEG_INF)
            m = jnp.max(s, axis=-1, keepdims=True)
            p = jnp.exp(s - m)
            l = jnp.sum(p, axis=-1, keepdims=True)
            o = jnp.dot(p.astype(BF16), vc, preferred_element_type=F32) / l
            lse = m + jnp.log(l)
            if merge:
                la = la_ref[0, :, h * LSE_LANES:h * LSE_LANES + 1]
                lb = lb_ref[0, :, h * LSE_LANES:h * LSE_LANES + 1]
                top = jnp.maximum(lse, jnp.maximum(la, lb))
                e1, ea, eb = jnp.exp(lse - top), jnp.exp(la - top), jnp.exp(lb - top)
                den = e1 + ea + eb
                halves.append(o * (e1 / den))
                wa.append(ea / den)
                wb.append(eb / den)
            else:
                halves.append(o)
                lse_tile = jnp.where(lane_head == h, lse, lse_tile)
        oc = jnp.where(low, halves[0], halves[1])
        if merge:
            oc = oc + oa_ref[0, :, sl].astype(F32) * jnp.where(low, wa[0], wa[1])
            oc = oc + ob_ref[0, :, sl].astype(F32) * jnp.where(low, wb[0], wb[1])
        o_ref[0, :, sl] = oc.astype(BF16)
    if not merge:
        lse_ref[0] = lse_tile


def _dil_attn(q, k, v, dil, prev=None):
    B, S, _ = q.shape
    L = S // dil
    view = lambda a: a.reshape(B, L, dil * a.shape[-1])
    tile = lambda w: pl.BlockSpec((1, TL, w), lambda b, r, t: (b, t, r))
    seq = pl.BlockSpec((1, L, ATTN_W), lambda b, r, t: (b, 0, r))
    merge = prev is not None
    args = [view(q), view(k), view(v)]
    in_specs = [tile(ATTN_W), seq, seq]
    if merge:
        for o_p, lse_p in prev:
            args += [view(o_p), view(lse_p)]
            in_specs += [tile(ATTN_W), tile(LANES)]
        out_specs = tile(ATTN_W)
        out_shape = jax.ShapeDtypeStruct((B, L, dil * ATTN_W), BF16)
    else:
        out_specs = [tile(ATTN_W), tile(LANES)]
        out_shape = [jax.ShapeDtypeStruct((B, L, dil * ATTN_W), BF16),
                     jax.ShapeDtypeStruct((B, L, dil * LANES), F32)]
    res = pl.pallas_call(
        functools.partial(_dil_attn_kernel, sub_len=L, merge=merge),
        grid=(B, dil, L // TL),
        in_specs=in_specs, out_specs=out_specs, out_shape=out_shape,
        compiler_params=_cparams(3),
        name=f"dil_attn_d{dil}",
    )(*args)
    if merge:
        return res.reshape(B, S, ATTN_W)
    return res[0].reshape(B, S, ATTN_W), res[1].reshape(B, S, LANES)


def _halo_specs(width, n_tiles):
    per = TM // HALO
    last = n_tiles * per - 1
    prev = pl.BlockSpec((1, HALO, width), lambda b, t: (b, jnp.maximum(t * per - 1, 0), 0))
    nxt = pl.BlockSpec((1, HALO, width), lambda b, t: (b, jnp.minimum((t + 1) * per, last), 0))
    return prev, nxt


def _conv_mod_kernel(c_ref, cp_ref, cn_ref, w_ref, b_ref, g_ref, beta_ref, o_ref, buf_ref):
    t = pl.program_id(1)
    nt = pl.num_programs(1)
    buf_ref[0:HALO] = jnp.where(t > 0, cp_ref[0], 0.0)
    buf_ref[HALO:HALO + TM] = c_ref[0]
    buf_ref[HALO + TM:] = jnp.where(t < nt - 1, cn_ref[0], 0.0)
    acc = jnp.zeros((TM, CONV_W), F32) + b_ref[...]
    for tap in range(CONV_K):
        acc = acc + w_ref[tap:tap + 1, :] * buf_ref[pl.ds(HALO - CONV_K // 2 + tap, TM), :]
    mu = jnp.mean(acc, axis=-1, keepdims=True)
    d = acc - mu
    var = jnp.mean(d * d, axis=-1, keepdims=True)
    z = d * lax.rsqrt(var + NORM_EPS) * g_ref[...] + beta_ref[...]
    o_ref[0] = (z * _sigmoid(z)).astype(BF16)


def _conv_mod(c, w, b, g, beta):
    B, S, _ = c.shape
    prev, nxt = _halo_specs(CONV_W, S // TM)
    return pl.pallas_call(
        _conv_mod_kernel,
        grid=(B, S // TM),
        in_specs=[pl.BlockSpec((1, TM, CONV_W), lambda b, t: (b, t, 0)), prev, nxt,
                  _const_spec((CONV_K, CONV_W)), _const_spec((1, CONV_W)),
                  _const_spec((1, CONV_W)), _const_spec((1, CONV_W))],
        out_specs=pl.BlockSpec((1, TM, CONV_W), lambda b, t: (b, t, 0)),
        out_shape=jax.ShapeDtypeStruct((B, S, CONV_W), BF16),
        scratch_shapes=[pltpu.VMEM((TM + 2 * HALO, CONV_W), F32)],
        compiler_params=_cparams(2),
        name="conv_mod",
    )(c, c, c, w, b, g, beta)


def _out_proj_kernel(a_ref, cb_ref, mo_ref, x_ref, w_ref, g_ref, h_ref, hn_ref):
    mixed = jnp.concatenate([a_ref[0], cb_ref[0], mo_ref[0]], axis=-1)
    h = x_ref[0] + jnp.dot(mixed, w_ref[...], preferred_element_type=F32)
    h_ref[0] = h
    hn_ref[0] = _rms_rows(h, g_ref[...]).astype(BF16)


def _out_proj(attn, cb, mo, x, w_out, g):
    B, S, _ = x.shape
    tok = lambda w: pl.BlockSpec((1, TM, w), lambda b, t: (b, t, 0))
    return pl.pallas_call(
        _out_proj_kernel,
        grid=(B, S // TM),
        in_specs=[tok(ATTN_W), tok(CONV_W), tok(MEM_W), tok(D_MODEL),
                  _const_spec((D_MODEL, D_MODEL)), _const_spec((1, D_MODEL))],
        out_specs=[tok(D_MODEL), tok(D_MODEL)],
        out_shape=[jax.ShapeDtypeStruct((B, S, D_MODEL), F32),
                   jax.ShapeDtypeStruct((B, S, D_MODEL), BF16)],
        compiler_params=_cparams(2),
        name="out_proj",
    )(attn, cb, mo, x, w_out, g)


def _ffn_kernel(hn_ref, hp_ref, hx_ref, h_ref, wu_ref, dw_ref, db_ref, wd_ref, o_ref, ext_ref):
    t = pl.program_id(1)
    nt = pl.num_programs(1)
    ext_ref[0:TM] = hn_ref[0]
    ext_ref[TM:TM + HALO] = jnp.where(t < nt - 1, hx_ref[0], jnp.zeros_like(hx_ref[0]))
    ext_ref[TM + HALO:] = jnp.where(t > 0, hp_ref[0], jnp.zeros_like(hp_ref[0]))
    ext = ext_ref[...]
    rows = TM + 2 * HALO

    def conv3(f, lo):
        w = dw_ref[:, lo:lo + FF_CH]
        y = (w[0:1] * pltpu.roll(f, 1, 0) + w[1:2] * f + w[2:3] * pltpu.roll(f, rows - 1, 0))
        return y[:TM] + db_ref[:, lo:lo + FF_CH]

    acc = h_ref[0]
    for j in range(D_FF // FF_CH):
        lo_g, lo_u = j * FF_CH, D_FF + j * FF_CH
        fg = conv3(jnp.dot(ext, wu_ref[:, lo_g:lo_g + FF_CH], preferred_element_type=F32), lo_g)
        fu = conv3(jnp.dot(ext, wu_ref[:, lo_u:lo_u + FF_CH], preferred_element_type=F32), lo_u)
        gate = (fg * _sigmoid(fg) * fu).astype(BF16)
        acc = acc + jnp.dot(gate, wd_ref[lo_g:lo_g + FF_CH, :], preferred_element_type=F32)
    o_ref[0] = acc


def _ffn(hn, h, w_up, dw_w, dw_b, w_down):
    B, S, _ = h.shape
    prev, nxt = _halo_specs(D_MODEL, S // TM)
    tok = pl.BlockSpec((1, TM, D_MODEL), lambda b, t: (b, t, 0))
    return pl.pallas_call(
        _ffn_kernel,
        grid=(B, S // TM),
        in_specs=[tok, prev, nxt, tok, _const_spec((D_MODEL, 2 * D_FF)),
                  _const_spec((FFN_CONV_K, 2 * D_FF)), _const_spec((1, 2 * D_FF)),
                  _const_spec((D_FF, D_MODEL))],
        out_specs=tok,
        out_shape=jax.ShapeDtypeStruct((B, S, D_MODEL), F32),
        scratch_shapes=[pltpu.VMEM((TM + 2 * HALO, D_MODEL), BF16)],
        compiler_params=_cparams(2),
        name="ffn",
    )(hn, hn, hn, h, w_up, dw_w, dw_b, w_down)


def _rope_table():
    inv_freq = ROPE_THETA ** (-jnp.arange(0, ROT_DIM, 2, dtype=F32) / ROT_DIM)
    half = ROT_DIM // 2
    head = jnp.zeros((HEAD_DIM,), F32)
    freq = head.at[:ROT_DIM].set(jnp.tile(inv_freq, 2))
    sign = head.at[:half].set(-1.0).at[half:ROT_DIM].set(1.0)
    first = head.at[:half].set(1.0)
    rows = jnp.stack([jnp.tile(r, LANES // HEAD_DIM) for r in (freq, sign, first)])
    return jnp.concatenate([rows, jnp.zeros((8 - rows.shape[0], LANES), F32)])


def _group_sum_matrix():
    idx = jnp.arange(MXU_DIM) // HEAD_DIM
    return (idx[:, None] == idx[None, :]).astype(BF16)


def kernel(x, mem, positions, mix_norm_g, mem_norm_g, w_in, w_mem_kv, q_norm_g, k_norm_g, mq_norm_g, mk_norm_g, conv_dw_w, conv_dw_b, conv_ln_g, conv_ln_b, w_out, ffn_norm_g, w_up, ffn_dw_w, ffn_dw_b, w_down):
    B, S, _ = x.shape
    depth = w_in.shape[0]
    pos3 = positions.reshape(B, S, 1)
    rope = _rope_table()
    gsum = _group_sum_matrix()
    row = lambda a: a.reshape(1, -1)
    h = x
    for l in range(depth):
        q, k, v, c, qm = _in_proj(
            h, pos3, row(mix_norm_g[l]), w_in[l].astype(BF16),
            row(jnp.tile(q_norm_g[l], ATTN_HEADS)), row(jnp.tile(k_norm_g[l], ATTN_HEADS)),
            row(jnp.tile(mq_norm_g[l], MEM_HEADS)), gsum, rope)
        kmt, vm = _mem_kv(mem, row(mem_norm_g[l]), w_mem_kv[l].astype(BF16),
                          row(jnp.tile(mk_norm_g[l], MEM_HEADS)), gsum)
        mo = _mem_attn(qm, kmt, vm)
        partial = [_dil_attn(q, k, v, d) for d in DILATIONS[:-1]]
        attn = _dil_attn(q, k, v, DILATIONS[-1], prev=partial)
        cb = _conv_mod(c, conv_dw_w[l], row(conv_dw_b[l]), row(conv_ln_g[l]), row(conv_ln_b[l]))
        h, hn = _out_proj(attn, cb, mo, h, w_out[l].astype(BF16), row(ffn_norm_g[l]))
        h = _ffn(hn, h, w_up[l].astype(BF16), ffn_dw_w[l], row(ffn_dw_b[l]),
                 w_down[l].astype(BF16))
    return h
```

```python
import functools

import numpy as np
import jax
import jax.numpy as jnp
from jax import lax
from jax.experimental import pallas as pl
from jax.experimental.pallas import tpu as pltpu

F32 = jnp.float32
BF16 = jnp.bfloat16

D_MODEL = 1024
HEAD_DIM = 64
ATTN_HEADS = 8
ATTN_W = ATTN_HEADS * HEAD_DIM
CONV_W = 256
MEM_HEADS = 4
MEM_W = MEM_HEADS * HEAD_DIM
N_MEM = 256
PLANES = 16
MID_DIL = 4
BAND_HALF = 64
ROPE_THETA = 500000.0
ROT_DIM = HEAD_DIM // 4
CONV_K = 31
FFN_CONV_K = 3
D_FF = 2816
NORM_EPS = 1e-6
NEG_INF = -1e30
SM_SCALE = HEAD_DIM ** -0.5

LANES = 128
SUBLANES = 8
MXU_DIM = 256
BF16_ROWS = 16
VMEM_LIMIT = 56 * 1024 * 1024

TM = 512
TL = 128
TK = TL + 2 * BAND_HALF
SUPER = PLANES * TL
FF_CH = MXU_DIM
HALO = BF16_ROWS


def _cparams(n_axes):
    return pltpu.CompilerParams(dimension_semantics=("parallel",) * n_axes,
                                vmem_limit_bytes=VMEM_LIMIT)


def _const_spec(shape):
    return pl.BlockSpec(shape, lambda *_: (0,) * len(shape), pipeline_mode=pl.Buffered(1))


def _rms_rows(x, g):
    r = lax.rsqrt(jnp.mean(x * x, axis=-1, keepdims=True) + NORM_EPS)
    return x * r * g


def _head_norm(t, gain, gsum):
    outs = []
    for c in range(t.shape[1] // MXU_DIM):
        tc = t[:, c * MXU_DIM:(c + 1) * MXU_DIM]
        ssum = jnp.dot((tc * tc).astype(BF16), gsum, preferred_element_type=F32)
        r = lax.rsqrt(ssum * (1.0 / HEAD_DIM) + NORM_EPS)
        outs.append(tc * r * gain[:, c * MXU_DIM:(c + 1) * MXU_DIM])
    return outs


def _sigmoid(x):
    return 1.0 / (1.0 + jnp.exp(-x))


def _in_proj_kernel(x_ref, pos_ref, g_ref, w_ref, gq_ref, gk_ref, gm_ref, gsum_ref, rope_ref,
                    perm_ref, q16_ref, k16_ref, v16_ref, kn_ref, vn_ref, c_ref, qm_ref):
    hn = _rms_rows(x_ref[0], g_ref[...]).astype(BF16)

    ang = pos_ref[0].astype(F32) * rope_ref[0:1, :]
    cosv = jnp.cos(ang)
    sinv = jnp.sin(ang) * rope_ref[1:2, :]
    first_half = rope_ref[2:3, :] > 0.5

    def rotary(chunks, scale):
        out = []
        for t in chunks:
            for s in range(MXU_DIM // LANES):
                xc = t[:, s * LANES:(s + 1) * LANES]
                partner = jnp.where(first_half, pltpu.roll(xc, LANES - ROT_DIM // 2, 1),
                                    pltpu.roll(xc, ROT_DIM // 2, 1))
                out.append(((xc * cosv + partner * sinv) * scale).astype(BF16))
        return jnp.concatenate(out, axis=1)

    gsum = gsum_ref[...]
    q = jnp.dot(hn, w_ref[:, 0:ATTN_W], preferred_element_type=F32)
    qb = rotary(_head_norm(q, gq_ref[...], gsum), SM_SCALE)
    k = jnp.dot(hn, w_ref[:, ATTN_W:2 * ATTN_W], preferred_element_type=F32)
    kb = rotary(_head_norm(k, gk_ref[...], gsum), 1.0)
    vb = jnp.dot(hn, w_ref[:, 2 * ATTN_W:3 * ATTN_W], preferred_element_type=F32).astype(BF16)
    kn_ref[0] = kb
    vn_ref[0] = vb
    qkv = jnp.dot(perm_ref[...], jnp.concatenate([qb, kb, vb], axis=1),
                  preferred_element_type=F32)
    rows = TM // PLANES
    for r in range(PLANES):
        blk = qkv[r * rows:(r + 1) * rows]
        q16_ref[0, r] = blk[:, 0:ATTN_W]
        k16_ref[0, r] = blk[:, ATTN_W:2 * ATTN_W].astype(BF16)
        v16_ref[0, r] = blk[:, 2 * ATTN_W:].astype(BF16)

    glu = jnp.dot(hn, w_ref[:, 3 * ATTN_W:3 * ATTN_W + 2 * CONV_W], preferred_element_type=F32)
    c_ref[0] = glu[:, :CONV_W] * _sigmoid(glu[:, CONV_W:])
    qm = jnp.dot(hn, w_ref[:, 3 * ATTN_W + 2 * CONV_W:], preferred_element_type=F32)
    (qmn,) = _head_norm(qm, gm_ref[...], gsum)
    qm_ref[0] = (qmn * SM_SCALE).astype(BF16)


def _in_proj(x, pos3, g, w_in, gq, gk, gm, gsum, rope, perm):
    B, S, _ = x.shape
    in_cols = w_in.shape[1]
    tok = lambda w: pl.BlockSpec((1, TM, w), lambda b, t: (b, t, 0))
    plane = pl.BlockSpec((1, PLANES, TM // PLANES, ATTN_W), lambda b, t: (b, 0, t, 0))
    plane_shape = (B, PLANES, S // PLANES, ATTN_W)
    return pl.pallas_call(
        _in_proj_kernel,
        grid=(B, S // TM),
        in_specs=[tok(D_MODEL), tok(1), _const_spec((1, D_MODEL)), _const_spec((D_MODEL, in_cols)),
                  _const_spec((1, ATTN_W)), _const_spec((1, ATTN_W)), _const_spec((1, MEM_W)),
                  _const_spec((MXU_DIM, MXU_DIM)), _const_spec((8, LANES)),
                  _const_spec((TM, TM))],
        out_specs=[plane, plane, plane, tok(ATTN_W), tok(ATTN_W), tok(CONV_W), tok(MEM_W)],
        out_shape=[jax.ShapeDtypeStruct(plane_shape, F32),
                   jax.ShapeDtypeStruct(plane_shape, BF16),
                   jax.ShapeDtypeStruct(plane_shape, BF16),
                   jax.ShapeDtypeStruct((B, S, ATTN_W), BF16),
                   jax.ShapeDtypeStruct((B, S, ATTN_W), BF16),
                   jax.ShapeDtypeStruct((B, S, CONV_W), F32),
                   jax.ShapeDtypeStruct((B, S, MEM_W), BF16)],
        compiler_params=_cparams(2),
        name="in_proj",
    )(x, pos3, g, w_in, gq, gk, gm, gsum, rope, perm)


def _mem_kv_kernel(mem_ref, g_ref, w_ref, gk_ref, gsum_ref, kmt_ref, vm_ref):
    mn = _rms_rows(mem_ref[0], g_ref[...]).astype(BF16)
    kv = jnp.dot(mn, w_ref[...], preferred_element_type=F32)
    (km,) = _head_norm(kv[:, :MEM_W], gk_ref[...], gsum_ref[...])
    kmt_ref[0] = km.T.astype(BF16)
    vm_ref[0] = kv[:, MEM_W:].astype(BF16)


def _mem_kv(mem, g, w, gk, gsum):
    B = mem.shape[0]
    return pl.pallas_call(
        _mem_kv_kernel,
        grid=(B,),
        in_specs=[pl.BlockSpec((1, N_MEM, D_MODEL), lambda b: (b, 0, 0)),
                  _const_spec((1, D_MODEL)), _const_spec((D_MODEL, 2 * MEM_W)),
                  _const_spec((1, MEM_W)), _const_spec((MXU_DIM, MXU_DIM))],
        out_specs=[pl.BlockSpec((1, MEM_W, N_MEM), lambda b: (b, 0, 0)),
                   pl.BlockSpec((1, N_MEM, MEM_W), lambda b: (b, 0, 0))],
        out_shape=[jax.ShapeDtypeStruct((B, MEM_W, N_MEM), BF16),
                   jax.ShapeDtypeStruct((B, N_MEM, MEM_W), BF16)],
        compiler_params=_cparams(1),
        name="mem_kv",
    )(mem, g, w, gk, gsum)


def _head_masks():
    lane = lax.broadcasted_iota(jnp.int32, (1, LANES), 1)
    low = lane < HEAD_DIM
    return low, (jnp.where(low, 1.0, 0.0).astype(BF16), jnp.where(low, 0.0, 1.0).astype(BF16))


def _mem_attn_kernel(qm_ref, kmt_ref, vm_ref, o_ref):
    low, hmask = _head_masks()
    for c in range(MEM_W // LANES):
        sl = slice(c * LANES, (c + 1) * LANES)
        qc = qm_ref[0, :, sl]
        halves = []
        for e in range(2):
            s = jnp.dot(qc * hmask[e], kmt_ref[0, sl, :], preferred_element_type=F32)
            m = jnp.max(s, axis=-1, keepdims=True)
            p = jnp.exp(s - m)
            l = jnp.sum(p, axis=-1, keepdims=True)
            o = jnp.dot(p.astype(BF16), vm_ref[0, :, sl], preferred_element_type=F32)
            halves.append(o / l)
        o_ref[0, :, sl] = jnp.where(low, halves[0], halves[1]).astype(BF16)


def _mem_attn(qm, kmt, vm):
    B, S, _ = qm.shape
    return pl.pallas_call(
        _mem_attn_kernel,
        grid=(B, S // TM),
        in_specs=[pl.BlockSpec((1, TM, MEM_W), lambda b, t: (b, t, 0)),
                  pl.BlockSpec((1, MEM_W, N_MEM), lambda b, t: (b, 0, 0)),
                  pl.BlockSpec((1, N_MEM, MEM_W), lambda b, t: (b, 0, 0))],
        out_specs=pl.BlockSpec((1, TM, MEM_W), lambda b, t: (b, t, 0)),
        out_shape=jax.ShapeDtypeStruct((B, S, MEM_W), BF16),
        compiler_params=_cparams(2),
        name="mem_attn",
    )(qm, kmt, vm)


HW = ATTN_W // 2
MID_ROWS = TL // MID_DIL
MID_KEYS = TK // MID_DIL
MID_LEAD = (MID_KEYS - MID_ROWS) // 2
ONE_ROWS = TL // PLANES
TILE_UNROLL = 4


def _band_tables():
    rho = np.arange(TL)[:, None]
    kap = np.arange(TK)[None, :]
    d16 = kap - BAND_HALF - rho
    j, lq = rho // MID_ROWS, rho % MID_ROWS
    jk, lk = kap // MID_KEYS, kap % MID_KEYS
    d4 = MID_DIL * (lk - MID_LEAD - lq) + (jk - j)
    r, l1 = rho // ONE_ROWS, rho % ONE_ROWS
    d1 = kap - BAND_HALF - PLANES * l1 - r
    tabs = [np.where(np.abs(d) <= BAND_HALF, 0.0, NEG_INF) for d in (d16, d4, d1)]
    return jnp.asarray(np.stack(tabs), F32)


def _attend(q, kw, vw, bias2, hmask, m_old=None):
    ones = jnp.ones((TK, LANES), BF16)
    res = []
    for c in range(HW // LANES):
        sl = slice(c * LANES, (c + 1) * LANES)
        qc = q[:, sl]
        qs = jnp.concatenate([qc * hmask[0], qc * hmask[1]], axis=0)
        s = lax.dot_general(qs, kw[:, sl], (((1,), (1,)), ((), ())),
                            preferred_element_type=F32) + bias2
        m = jnp.max(s, axis=-1, keepdims=True)
        if m_old is None:
            shift = m
        else:
            m = jnp.maximum(m, m_old[c])
            shift = jnp.concatenate([m] * (TK // LANES), axis=1)
        p = jnp.exp(s - shift).astype(BF16)
        r = jnp.dot(p, jnp.concatenate([vw[:, sl], ones], axis=1), preferred_element_type=F32)
        res.append((m, r[:, LANES:], r[:, :LANES]))
    return res


def _side_by_side(stacked, low):
    return jnp.where(low, stacked[:TL], stacked[TL:])


def _merge(new, m_old, l_old, acc_old, low):
    m, l_n, acc_n = new
    b = jnp.exp(m_old - m)
    return m, l_n + b * l_old, _side_by_side(acc_n, low) + _side_by_side(b, low) * acc_old


def _dil_attn_kernel(q_ref, kc_ref, kp_ref, kx_ref, vc_ref, vp_ref, vx_ref,
                     knc_ref, knp_ref, knx_ref, vnc_ref, vnp_ref, vnx_ref, band_ref, unperm_ref,
                     o_ref, acc_ref, m_ref, l_ref, kne_ref, vne_ref):
    st = pl.program_id(1)
    first, last = st == 0, st == pl.num_programs(1) - 1
    low, hmask = _head_masks()
    col = lax.broadcasted_iota(jnp.int32, (1, TK), 1)
    chunk = lambda c: slice(c * LANES, (c + 1) * LANES)

    def stacked_bias(band, col_idx=None, lo=0, hi=TK):
        if col_idx is not None:
            band = band + jnp.where((col_idx < lo) | (col_idx >= hi), NEG_INF, 0.0)
        return jnp.concatenate([band, band], axis=0)

    bias16 = stacked_bias(band_ref[0], col, jnp.where(first, BAND_HALF, 0),
                          jnp.where(last, TK - BAND_HALF, TK))

    def body16(r, carry):
        kw = jnp.concatenate([kp_ref[0, r], kc_ref[0, r], kx_ref[0, r]], axis=0)
        vw = jnp.concatenate([vp_ref[0, r], vc_ref[0, r], vx_ref[0, r]], axis=0)
        res = _attend(q_ref[0, r].astype(BF16), kw, vw, bias16, hmask)
        for c, (m, l, acc) in enumerate(res):
            acc_ref[r, :, chunk(c)] = _side_by_side(acc, low)
            for e in range(2):
                rows = slice(e * TL, (e + 1) * TL)
                m_ref[c, e, r] = jnp.broadcast_to(m[rows], (TL, LANES))
                l_ref[c, e, r] = l[rows]
        return carry

    lax.fori_loop(0, PLANES, body16, 0, unroll=TILE_UNROLL)

    def mid_window(cur, prev, nxt, plane, lb):
        lo = lb * MID_ROWS - MID_LEAD
        if lo < 0:
            return jnp.concatenate([prev[0, plane, TL // 2 + lo:TL // 2],
                                    cur[0, plane, 0:lo + MID_KEYS]], axis=0)
        if lo + MID_KEYS > TL:
            return jnp.concatenate([cur[0, plane, lo:TL],
                                    nxt[0, plane, 0:lo + MID_KEYS - TL]], axis=0)
        return cur[0, plane, lo:lo + MID_KEYS]

    def body4(r4, carry):
        planes = [r4 + MID_DIL * j for j in range(MID_DIL)]
        for lb in range(TL // MID_ROWS):
            rows = slice(lb * MID_ROWS, (lb + 1) * MID_ROWS)
            band = band_ref[1]
            if lb == 0:
                bias = stacked_bias(band, col % MID_KEYS, jnp.where(first, MID_LEAD, 0), MID_KEYS)
            elif lb == TL // MID_ROWS - 1:
                bias = stacked_bias(band, col % MID_KEYS, 0,
                                    jnp.where(last, MID_KEYS - MID_LEAD, MID_KEYS))
            else:
                bias = stacked_bias(band)
            stacked = lambda ref, c: jnp.concatenate(
                [ref[c, e, p, rows] for e in range(2) for p in planes], axis=0)
            q = jnp.concatenate([q_ref[0, p, rows] for p in planes], axis=0).astype(BF16)
            kw = jnp.concatenate([mid_window(kc_ref, kp_ref, kx_ref, p, lb) for p in planes],
                                 axis=0)
            vw = jnp.concatenate([mid_window(vc_ref, vp_ref, vx_ref, p, lb) for p in planes],
                                 axis=0)
            m_old = [stacked(m_ref, c) for c in range(HW // LANES)]
            res = _attend(q, kw, vw, bias, hmask, m_old)
            for c, new in enumerate(res):
                acc_old = jnp.concatenate([acc_ref[p, rows, chunk(c)] for p in planes], axis=0)
                m_m, l_m, acc_m = _merge(new, m_old[c], stacked(l_ref, c), acc_old, low)
                for j, p in enumerate(planes):
                    piece = slice(j * MID_ROWS, (j + 1) * MID_ROWS)
                    acc_ref[p, rows, chunk(c)] = acc_m[piece]
                    for e in range(2):
                        head_piece = slice(e * TL + j * MID_ROWS, e * TL + (j + 1) * MID_ROWS)
                        m_ref[c, e, p, rows] = m_m[head_piece]
                        l_ref[c, e, p, rows] = l_m[head_piece]
        return carry

    lax.fori_loop(0, MID_DIL, body4, 0)

    n_tiles = SUPER // TL
    for ext_ref, prev, cur, nxt in ((kne_ref, knp_ref, knc_ref, knx_ref),
                                    (vne_ref, vnp_ref, vnc_ref, vnx_ref)):
        ext_ref[0:BAND_HALF] = prev[0]
        ext_ref[BAND_HALF:BAND_HALF + SUPER] = cur[0]
        ext_ref[BAND_HALF + SUPER:] = nxt[0]
    band1 = stacked_bias(band_ref[2])

    def body1(t, carry):
        lo = jnp.where(first & (t == 0), BAND_HALF, 0)
        hi = jnp.where(last & (t == n_tiles - 1), TK - BAND_HALF, TK)
        bias = band1 + jnp.where((col < lo) | (col >= hi), NEG_INF, 0.0)
        rows = pl.ds(pl.multiple_of(t * ONE_ROWS, ONE_ROWS), ONE_ROWS)
        win = pl.ds(pl.multiple_of(t * TL, TL), TK)
        stacked = lambda ref, c: jnp.concatenate(
            [ref[c, e, :, rows, :].reshape(TL, LANES) for e in range(2)], axis=0)
        q = q_ref[0, :, rows, :].reshape(TL, HW).astype(BF16)
        m_old = [stacked(m_ref, c) for c in range(HW // LANES)]
        res = _attend(q, kne_ref[win, :], vne_ref[win, :], bias, hmask, m_old)
        outs = []
        for c, new in enumerate(res):
            acc_old = acc_ref[:, rows, chunk(c)].reshape(TL, LANES)
            _, l_m, acc_m = _merge(new, m_old[c], stacked(l_ref, c), acc_old, low)
            outs.append((acc_m / _side_by_side(l_m, low)).astype(BF16))
        merged = jnp.concatenate(outs, axis=1)
        tok = jnp.dot(unperm_ref[...], merged, preferred_element_type=F32)
        o_ref[0, pl.ds(pl.multiple_of(t * TL, TL), TL), :] = tok.astype(BF16)
        return carry

    lax.fori_loop(0, n_tiles, body1, 0, unroll=TILE_UNROLL)


def _dil_attn(q16, k16, v16, kn, vn, band, unperm):
    B, S, _ = kn.shape
    n_half = S // PLANES // BAND_HALF
    n_tok_half = S // BAND_HALF
    per = TL // BAND_HALF
    per_tok = SUPER // BAND_HALF
    pcur = pl.BlockSpec((1, PLANES, TL, HW), lambda b, s, hh: (b, 0, s, hh))
    pprev = pl.BlockSpec((1, PLANES, BAND_HALF, HW),
                         lambda b, s, hh: (b, 0, jnp.maximum(s * per - 1, 0), hh))
    pnext = pl.BlockSpec((1, PLANES, BAND_HALF, HW),
                         lambda b, s, hh: (b, 0, jnp.minimum((s + 1) * per, n_half - 1), hh))
    tcur = pl.BlockSpec((1, SUPER, HW), lambda b, s, hh: (b, s, hh))
    tprev = pl.BlockSpec((1, BAND_HALF, HW),
                         lambda b, s, hh: (b, jnp.maximum(s * per_tok - 1, 0), hh))
    tnext = pl.BlockSpec((1, BAND_HALF, HW),
                         lambda b, s, hh: (b, jnp.minimum((s + 1) * per_tok, n_tok_half - 1), hh))
    return pl.pallas_call(
        _dil_attn_kernel,
        grid=(B, S // SUPER, ATTN_W // HW),
        in_specs=[pcur, pcur, pprev, pnext, pcur, pprev, pnext,
                  tcur, tprev, tnext, tcur, tprev, tnext,
                  _const_spec((3, TL, TK)), _const_spec((TL, TL))],
        out_specs=tcur,
        out_shape=jax.ShapeDtypeStruct((B, S, ATTN_W), BF16),
        scratch_shapes=[pltpu.VMEM((PLANES, TL, HW), F32),
                        pltpu.VMEM((HW // LANES, 2, PLANES, TL, LANES), F32),
                        pltpu.VMEM((HW // LANES, 2, PLANES, TL, LANES), F32),
                        pltpu.VMEM((SUPER + 2 * BAND_HALF, HW), BF16),
                        pltpu.VMEM((SUPER + 2 * BAND_HALF, HW), BF16)],
        compiler_params=_cparams(3),
        name="dil_attn",
    )(q16, k16, k16, k16, v16, v16, v16, kn, kn, kn, vn, vn, vn, band, unperm)


def _halo_specs(width, n_tiles, rows):
    per = TM // rows
    last = n_tiles * per - 1
    prev = pl.BlockSpec((1, rows, width), lambda b, t: (b, jnp.maximum(t * per - 1, 0), 0))
    nxt = pl.BlockSpec((1, rows, width), lambda b, t: (b, jnp.minimum((t + 1) * per, last), 0))
    return prev, nxt


def _conv_mod_kernel(c_ref, cp_ref, cn_ref, w_ref, b_ref, g_ref, beta_ref, o_ref):
    t = pl.program_id(1)
    nt = pl.num_programs(1)
    buf = jnp.concatenate([jnp.where(t > 0, cp_ref[0], 0.0), c_ref[0],
                           jnp.where(t < nt - 1, cn_ref[0], 0.0)], axis=0)
    rows = TM + 2 * HALO
    acc = jnp.zeros((TM, CONV_W), F32) + b_ref[...]
    base = HALO - CONV_K // 2
    for shift in range(SUBLANES):
        rolled = buf if shift == 0 else pltpu.roll(buf, rows - shift, 0)
        for tap in range(CONV_K):
            off = base + tap
            if off % SUBLANES == shift:
                lo = off - shift
                acc = acc + w_ref[tap:tap + 1, :] * rolled[lo:lo + TM]
    mu = jnp.mean(acc, axis=-1, keepdims=True)
    d = acc - mu
    var = jnp.mean(d * d, axis=-1, keepdims=True)
    z = d * lax.rsqrt(var + NORM_EPS) * g_ref[...] + beta_ref[...]
    o_ref[0] = (z * _sigmoid(z)).astype(BF16)


def _conv_mod(c, w, b, g, beta):
    B, S, _ = c.shape
    prev, nxt = _halo_specs(CONV_W, S // TM, HALO)
    return pl.pallas_call(
        _conv_mod_kernel,
        grid=(B, S // TM),
        in_specs=[pl.BlockSpec((1, TM, CONV_W), lambda b, t: (b, t, 0)), prev, nxt,
                  _const_spec((CONV_K, CONV_W)), _const_spec((1, CONV_W)),
                  _const_spec((1, CONV_W)), _const_spec((1, CONV_W))],
        out_specs=pl.BlockSpec((1, TM, CONV_W), lambda b, t: (b, t, 0)),
        out_shape=jax.ShapeDtypeStruct((B, S, CONV_W), BF16),
        compiler_params=_cparams(2),
        name="conv_mod",
    )(c, c, c, w, b, g, beta)


def _out_proj_kernel(a_ref, cb_ref, mo_ref, x_ref, w_ref, h_ref):
    mixed = jnp.concatenate([a_ref[0], cb_ref[0], mo_ref[0]], axis=-1)
    h_ref[0] = x_ref[0] + jnp.dot(mixed, w_ref[...], preferred_element_type=F32)


def _out_proj(attn, cb, mo, x, w_out):
    B, S, _ = x.shape
    tok = lambda w: pl.BlockSpec((1, TM, w), lambda b, t: (b, t, 0))
    return pl.pallas_call(
        _out_proj_kernel,
        grid=(B, S // TM),
        in_specs=[tok(ATTN_W), tok(CONV_W), tok(MEM_W), tok(D_MODEL),
                  _const_spec((D_MODEL, D_MODEL))],
        out_specs=tok(D_MODEL),
        out_shape=jax.ShapeDtypeStruct((B, S, D_MODEL), F32),
        compiler_params=_cparams(2),
        name="out_proj",
    )(attn, cb, mo, x, w_out)


def _ffn_kernel(h_ref, hp_ref, hx_ref, g_ref, wu_ref, dw_ref, db_ref, wd_ref, o_ref, gate_ref):
    t = pl.program_id(1)
    nt = pl.num_programs(1)
    ext = jnp.concatenate([h_ref[0], jnp.where(t < nt - 1, hx_ref[0], 0.0),
                           jnp.where(t > 0, hp_ref[0], 0.0)], axis=0)
    ext = _rms_rows(ext, g_ref[...]).astype(BF16)
    rows = TM + 2 * SUBLANES

    def conv3(f, lo):
        w = dw_ref[:, lo:lo + FF_CH]
        y = (w[0:1] * pltpu.roll(f, 1, 0) + w[1:2] * f + w[2:3] * pltpu.roll(f, rows - 1, 0))
        return y[:TM] + db_ref[:, lo:lo + FF_CH]

    for j in range(D_FF // FF_CH):
        lo_g, lo_u = j * FF_CH, D_FF + j * FF_CH
        fg = conv3(jnp.dot(ext, wu_ref[:, lo_g:lo_g + FF_CH], preferred_element_type=F32), lo_g)
        fu = conv3(jnp.dot(ext, wu_ref[:, lo_u:lo_u + FF_CH], preferred_element_type=F32), lo_u)
        gate_ref[:, lo_g:lo_g + FF_CH] = (fg * _sigmoid(fg) * fu).astype(BF16)
    o_ref[0] = h_ref[0] + jnp.dot(gate_ref[...], wd_ref[...], preferred_element_type=F32)


def _ffn(h, g, w_up, dw_w, dw_b, w_down):
    B, S, _ = h.shape
    prev, nxt = _halo_specs(D_MODEL, S // TM, SUBLANES)
    tok = pl.BlockSpec((1, TM, D_MODEL), lambda b, t: (b, t, 0))
    return pl.pallas_call(
        _ffn_kernel,
        grid=(B, S // TM),
        in_specs=[tok, prev, nxt, _const_spec((1, D_MODEL)), _const_spec((D_MODEL, 2 * D_FF)),
                  _const_spec((FFN_CONV_K, 2 * D_FF)), _const_spec((1, 2 * D_FF)),
                  _const_spec((D_FF, D_MODEL))],
        out_specs=tok,
        out_shape=jax.ShapeDtypeStruct((B, S, D_MODEL), F32),
        scratch_shapes=[pltpu.VMEM((TM, D_FF), BF16)],
        compiler_params=_cparams(2),
        name="ffn",
    )(h, h, h, g, w_up, dw_w, dw_b, w_down)


def _rope_table():
    inv_freq = ROPE_THETA ** (-jnp.arange(0, ROT_DIM, 2, dtype=F32) / ROT_DIM)
    half = ROT_DIM // 2
    head = jnp.zeros((HEAD_DIM,), F32)
    freq = head.at[:ROT_DIM].set(jnp.tile(inv_freq, 2))
    sign = head.at[:half].set(-1.0).at[half:ROT_DIM].set(1.0)
    first = head.at[:half].set(1.0)
    rows = jnp.stack([jnp.tile(r, LANES // HEAD_DIM) for r in (freq, sign, first)])
    return jnp.concatenate([rows, jnp.zeros((8 - rows.shape[0], LANES), F32)])


def _group_sum_matrix():
    idx = np.arange(MXU_DIM) // HEAD_DIM
    return jnp.asarray(idx[:, None] == idx[None, :], BF16)


def _plane_perm(n):
    out = np.arange(n)
    src = (out % (n // PLANES)) * PLANES + out // (n // PLANES)
    return np.asarray(src[:, None] == np.arange(n)[None, :], np.float32)


def kernel(x, mem, positions, mix_norm_g, mem_norm_g, w_in, w_mem_kv, q_norm_g, k_norm_g, mq_norm_g, mk_norm_g, conv_dw_w, conv_dw_b, conv_ln_g, conv_ln_b, w_out, ffn_norm_g, w_up, ffn_dw_w, ffn_dw_b, w_down):
    B, S, _ = x.shape
    depth = w_in.shape[0]
    pos3 = positions.reshape(B, S, 1)
    rope = _rope_table()
    gsum = _group_sum_matrix()
    perm = jnp.asarray(_plane_perm(TM), BF16)
    unperm = jnp.asarray(_plane_perm(TL).T, BF16)
    band = _band_tables()
    row = lambda a: a.reshape(1, -1)
    h = x
    for l in range(depth):
        q16, k16, v16, kn, vn, c, qm = _in_proj(
            h, pos3, row(mix_norm_g[l]), w_in[l].astype(BF16),
            row(jnp.tile(q_norm_g[l], ATTN_HEADS)), row(jnp.tile(k_norm_g[l], ATTN_HEADS)),
            row(jnp.tile(mq_norm_g[l], MEM_HEADS)), gsum, rope, perm)
        kmt, vm = _mem_kv(mem, row(mem_norm_g[l]), w_mem_kv[l].astype(BF16),
                          row(jnp.tile(mk_norm_g[l], MEM_HEADS)), gsum)
        mo = _mem_attn(qm, kmt, vm)
        attn = _dil_attn(q16, k16, v16, kn, vn, band, unperm)
        cb = _conv_mod(c, conv_dw_w[l], row(conv_dw_b[l]), row(conv_ln_g[l]), row(conv_ln_b[l]))
        h = _out_proj(attn, cb, mo, h, w_out[l].astype(BF16))
        h = _ffn(h, row(ffn_norm_g[l]), w_up[l].astype(BF16), ffn_dw_w[l], row(ffn_dw_b[l]),
                 w_down[l].astype(BF16))
    return h
```

```python
import functools

import numpy as np
import jax
import jax.numpy as jnp
from jax import lax
from jax.experimental import pallas as pl
from jax.experimental.pallas import tpu as pltpu

F32 = jnp.float32
BF16 = jnp.bfloat16

D_MODEL = 1024
HEAD_DIM = 64
ATTN_HEADS = 8
ATTN_W = ATTN_HEADS * HEAD_DIM
CONV_W = 256
MEM_HEADS = 4
MEM_W = MEM_HEADS * HEAD_DIM
N_MEM = 256
PLANES = 16
MID_DIL = 4
BAND_HALF = 64
ROPE_THETA = 500000.0
ROT_DIM = HEAD_DIM // 4
CONV_K = 31
FFN_CONV_K = 3
D_FF = 2816
NORM_EPS = 1e-6
NEG_INF = -1e30
SM_SCALE = HEAD_DIM ** -0.5

LANES = 128
SUBLANES = 8
MXU_DIM = 256
BF16_ROWS = 16
VMEM_LIMIT = 56 * 1024 * 1024

TM = 512
TF = 512
TL = 128
TK = TL + 2 * BAND_HALF
SUPER = PLANES * TL
FF_CH = MXU_DIM
HALO = BF16_ROWS


def _cparams(n_axes):
    return pltpu.CompilerParams(dimension_semantics=("parallel",) * n_axes,
                                vmem_limit_bytes=VMEM_LIMIT)


def _const_spec(shape):
    return pl.BlockSpec(shape, lambda *_: (0,) * len(shape), pipeline_mode=pl.Buffered(1))


def _rms_rows(x, g):
    r = lax.rsqrt(jnp.mean(x * x, axis=-1, keepdims=True) + NORM_EPS)
    return x * r * g


def _head_norm(t, gain, gsum):
    outs = []
    for c in range(t.shape[1] // MXU_DIM):
        tc = t[:, c * MXU_DIM:(c + 1) * MXU_DIM]
        ssum = jnp.dot((tc * tc).astype(BF16), gsum, preferred_element_type=F32)
        r = lax.rsqrt(ssum * (1.0 / HEAD_DIM) + NORM_EPS)
        outs.append(tc * r * gain[:, c * MXU_DIM:(c + 1) * MXU_DIM])
    return outs


def _sigmoid(x):
    return 1.0 / (1.0 + jnp.exp(-x))


def _in_proj_kernel(x_ref, xp_ref, xx_ref, pos_ref, g_ref, w_ref, gq_ref, gk_ref, gm_ref,
                    gsum_ref, invf_ref, lane_ref, expand_ref, perm_ref,
                    cw_ref, cbias_ref, cg_ref, cbeta_ref, kmt_ref, vm_ref,
                    q16_ref, k16_ref, v16_ref, kn_ref, vn_ref, cb_ref, mo_ref):
    t = pl.program_id(1)
    has_prev, has_next = t > 0, t < pl.num_programs(1) - 1
    hn = _rms_rows(x_ref[0], g_ref[...]).astype(BF16)

    ang = invf_ref[:, 0:1] * pos_ref[0].astype(F32)

    def hi_lo(v):
        hi = v.astype(BF16).astype(F32)
        return [hi, (v - hi).astype(BF16).astype(F32)]

    tab = jnp.concatenate(hi_lo(jnp.cos(ang)) + hi_lo(jnp.sin(ang))
                          + [jnp.zeros((LANES - 4 * SUBLANES, TM), F32)], axis=0)
    cs = jnp.dot(tab.T.astype(BF16), expand_ref[...], preferred_element_type=F32)
    cosv = cs[:, :LANES] + lane_ref[0:1, :]
    sinv = cs[:, LANES:]
    first_half = lane_ref[1:2, :] > 0.5

    def rotary(chunks, scale):
        out = []
        for t in chunks:
            for s in range(MXU_DIM // LANES):
                xc = t[:, s * LANES:(s + 1) * LANES]
                partner = jnp.where(first_half, pltpu.roll(xc, LANES - ROT_DIM // 2, 1),
                                    pltpu.roll(xc, ROT_DIM // 2, 1))
                out.append(((xc * cosv + partner * sinv) * scale).astype(BF16))
        return jnp.concatenate(out, axis=1)

    gsum = gsum_ref[...]
    q = jnp.dot(hn, w_ref[:, 0:ATTN_W], preferred_element_type=F32)
    qb = rotary(_head_norm(q, gq_ref[...], gsum), SM_SCALE)
    k = jnp.dot(hn, w_ref[:, ATTN_W:2 * ATTN_W], preferred_element_type=F32)
    kb = rotary(_head_norm(k, gk_ref[...], gsum), 1.0)
    vb = jnp.dot(hn, w_ref[:, 2 * ATTN_W:3 * ATTN_W], preferred_element_type=F32).astype(BF16)
    kn_ref[0] = kb
    vn_ref[0] = vb
    qkv = jnp.dot(perm_ref[...], jnp.concatenate([qb, kb, vb], axis=1),
                  preferred_element_type=F32)
    rows = TM // PLANES
    for r in range(PLANES):
        blk = qkv[r * rows:(r + 1) * rows]
        q16_ref[0, r] = blk[:, 0:ATTN_W]
        k16_ref[0, r] = blk[:, ATTN_W:2 * ATTN_W].astype(BF16)
        v16_ref[0, r] = blk[:, 2 * ATTN_W:].astype(BF16)

    hn_halo = _rms_rows(jnp.concatenate([xp_ref[0], xx_ref[0]], axis=0), g_ref[...]).astype(BF16)
    hn_ext = jnp.concatenate([hn_halo[:HALO], hn, hn_halo[HALO:]], axis=0)
    glu = jnp.dot(hn_ext, w_ref[:, 3 * ATTN_W:3 * ATTN_W + 2 * CONV_W],
                  preferred_element_type=F32)
    rid = lax.broadcasted_iota(jnp.int32, (TM + 2 * HALO, 1), 0)
    in_seq = ((rid >= HALO) | has_prev) & ((rid < TM + HALO) | has_next)
    cbuf = jnp.where(in_seq, glu[:, :CONV_W] * _sigmoid(glu[:, CONV_W:]), 0.0)
    cb_ref[0] = _conv_module(cbuf, TM, cw_ref, cbias_ref, cg_ref, cbeta_ref).astype(BF16)

    qm = jnp.dot(hn, w_ref[:, 3 * ATTN_W + 2 * CONV_W:], preferred_element_type=F32)
    (qmn,) = _head_norm(qm, gm_ref[...], gsum)
    mo_ref[0] = _mem_attn((qmn * SM_SCALE).astype(BF16), kmt_ref, vm_ref)


def _in_proj(x, pos_row, g, w_in, gq, gk, gm, gsum, invf, lane_tab, expand, perm,
             conv_w, conv_b, conv_g, conv_beta, kmt, vm):
    B, S, _ = x.shape
    in_cols = w_in.shape[1]
    tok = lambda w: pl.BlockSpec((1, TM, w), lambda b, t: (b, t, 0))
    xprev, xnext = _halo_specs(D_MODEL, S // TM, HALO)
    plane = pl.BlockSpec((1, PLANES, TM // PLANES, ATTN_W), lambda b, t: (b, 0, t, 0))
    plane_shape = (B, PLANES, S // PLANES, ATTN_W)
    per_batch = lambda r, w: pl.BlockSpec((1, r, w), lambda b, t: (b, 0, 0))
    return pl.pallas_call(
        _in_proj_kernel,
        grid=(B, S // TM),
        in_specs=[tok(D_MODEL), xprev, xnext, pl.BlockSpec((1, 1, TM), lambda b, t: (b, 0, t)),
                  _const_spec((1, D_MODEL)), _const_spec((D_MODEL, in_cols)),
                  _const_spec((1, ATTN_W)), _const_spec((1, ATTN_W)), _const_spec((1, MEM_W)),
                  _const_spec((MXU_DIM, MXU_DIM)), _const_spec((SUBLANES, LANES)),
                  _const_spec((SUBLANES, LANES)), _const_spec((LANES, 2 * LANES)),
                  _const_spec((TM, TM)),
                  _const_spec((CONV_K, CONV_W)), _const_spec((1, CONV_W)),
                  _const_spec((1, CONV_W)), _const_spec((1, CONV_W)),
                  per_batch(MEM_W, N_MEM), per_batch(N_MEM, MEM_W)],
        out_specs=[plane, plane, plane, tok(ATTN_W), tok(ATTN_W), tok(CONV_W), tok(MEM_W)],
        out_shape=[jax.ShapeDtypeStruct(plane_shape, F32),
                   jax.ShapeDtypeStruct(plane_shape, BF16),
                   jax.ShapeDtypeStruct(plane_shape, BF16),
                   jax.ShapeDtypeStruct((B, S, ATTN_W), BF16),
                   jax.ShapeDtypeStruct((B, S, ATTN_W), BF16),
                   jax.ShapeDtypeStruct((B, S, CONV_W), BF16),
                   jax.ShapeDtypeStruct((B, S, MEM_W), BF16)],
        compiler_params=_cparams(2),
        name="in_proj",
    )(x, x, x, pos_row, g, w_in, gq, gk, gm, gsum, invf, lane_tab, expand, perm,
      conv_w, conv_b, conv_g, conv_beta, kmt, vm)


def _mem_kv_kernel(mem_ref, g_ref, w_ref, gk_ref, gsum_ref, kmt_ref, vm_ref):
    mn = _rms_rows(mem_ref[0], g_ref[...]).astype(BF16)
    kv = jnp.dot(mn, w_ref[...], preferred_element_type=F32)
    (km,) = _head_norm(kv[:, :MEM_W], gk_ref[...], gsum_ref[...])
    kmt_ref[0] = km.T.astype(BF16)
    vm_ref[0] = kv[:, MEM_W:].astype(BF16)


def _mem_kv(mem, g, w, gk, gsum):
    B = mem.shape[0]
    return pl.pallas_call(
        _mem_kv_kernel,
        grid=(B,),
        in_specs=[pl.BlockSpec((1, N_MEM, D_MODEL), lambda b: (b, 0, 0)),
                  _const_spec((1, D_MODEL)), _const_spec((D_MODEL, 2 * MEM_W)),
                  _const_spec((1, MEM_W)), _const_spec((MXU_DIM, MXU_DIM))],
        out_specs=[pl.BlockSpec((1, MEM_W, N_MEM), lambda b: (b, 0, 0)),
                   pl.BlockSpec((1, N_MEM, MEM_W), lambda b: (b, 0, 0))],
        out_shape=[jax.ShapeDtypeStruct((B, MEM_W, N_MEM), BF16),
                   jax.ShapeDtypeStruct((B, N_MEM, MEM_W), BF16)],
        compiler_params=_cparams(1),
        name="mem_kv",
    )(mem, g, w, gk, gsum)


def _head_masks():
    lane = lax.broadcasted_iota(jnp.int32, (1, LANES), 1)
    low = lane < HEAD_DIM
    return low, (jnp.where(low, 1.0, 0.0).astype(BF16), jnp.where(low, 0.0, 1.0).astype(BF16))


def _mem_attn(qm, kmt_ref, vm_ref):
    low, hmask = _head_masks()
    out = []
    for c in range(MEM_W // LANES):
        sl = slice(c * LANES, (c + 1) * LANES)
        qc = qm[:, sl]
        halves = []
        for e in range(2):
            s = jnp.dot(qc * hmask[e], kmt_ref[0, sl, :], preferred_element_type=F32)
            m = jnp.max(s, axis=-1, keepdims=True)
            p = jnp.exp(s - m)
            l = jnp.sum(p, axis=-1, keepdims=True)
            o = jnp.dot(p.astype(BF16), vm_ref[0, :, sl], preferred_element_type=F32)
            halves.append(o / l)
        out.append(jnp.where(low, halves[0], halves[1]).astype(BF16))
    return jnp.concatenate(out, axis=1)


HW = ATTN_W // 2
MID_ROWS = TL // MID_DIL
MID_KEYS = TK // MID_DIL
MID_LEAD = (MID_KEYS - MID_ROWS) // 2
ONE_ROWS = TL // PLANES
TILE_GROUP = 8


def _band_tables():
    rho = np.arange(TL)[:, None]
    kap = np.arange(TK)[None, :]
    d16 = kap - BAND_HALF - rho
    j, lq = rho // MID_ROWS, rho % MID_ROWS
    jk, lk = kap // MID_KEYS, kap % MID_KEYS
    d4 = MID_DIL * (lk - MID_LEAD - lq) + (jk - j)
    r, l1 = rho // ONE_ROWS, rho % ONE_ROWS
    d1 = kap - BAND_HALF - PLANES * l1 - r
    tabs = [np.where(np.abs(d) <= BAND_HALF, 0.0, NEG_INF) for d in (d16, d4, d1)]
    return jnp.asarray(np.stack(tabs), F32)


ALL_CHUNKS = tuple(range(HW // LANES))


def _scores(q, kw, bias2, hmask, chunks=ALL_CHUNKS):
    out = []
    for c in chunks:
        sl = slice(c * LANES, (c + 1) * LANES)
        qc = q[:, sl]
        qs = jnp.concatenate([qc * hmask[0], qc * hmask[1]], axis=0)
        out.append(lax.dot_general(qs, kw[:, sl], (((1,), (1,)), ((), ())),
                                   preferred_element_type=F32) + bias2)
    return out


def _softmax_pv(scores, vw, m_old=None, chunks=ALL_CHUNKS):
    ones = jnp.ones((TK, LANES), BF16)
    res = []
    for c, s in zip(chunks, scores):
        m = jnp.max(s, axis=-1, keepdims=True)
        if m_old is None:
            shift = m
        else:
            m = jnp.maximum(m, m_old[c])
            shift = jnp.concatenate([m] * (TK // LANES), axis=1)
        p = jnp.exp(s - shift).astype(BF16)
        v_aug = jnp.concatenate([vw[:, c * LANES:(c + 1) * LANES], ones], axis=1)
        r = jnp.dot(p, v_aug, preferred_element_type=F32)
        res.append((m, r[:, LANES:], r[:, :LANES]))
    return res


def _attend(q, kw, vw, bias2, hmask, m_old=None):
    res = []
    for c in ALL_CHUNKS:
        res += _softmax_pv(_scores(q, kw, bias2, hmask, (c,)), vw, m_old, (c,))
    return res


def _side_by_side(stacked, low):
    return jnp.where(low, stacked[:TL], stacked[TL:])


def _merge(new, m_old, l_old, acc_old, low):
    m, l_n, acc_n = new
    b = jnp.exp(m_old - m)
    return m, l_n + b * l_old, _side_by_side(acc_n, low) + _side_by_side(b, low) * acc_old


def _dil_attn_kernel(q_ref, kc_ref, kp_ref, kx_ref, vc_ref, vp_ref, vx_ref,
                     knc_ref, knp_ref, knx_ref, vnc_ref, vnp_ref, vnx_ref, band_ref, unperm_ref,
                     o_ref, acc_ref, m_ref, l_ref, kne_ref, vne_ref):
    st = pl.program_id(1)
    first, last = st == 0, st == pl.num_programs(1) - 1
    low, hmask = _head_masks()
    col = lax.broadcasted_iota(jnp.int32, (1, TK), 1)
    chunk = lambda c: slice(c * LANES, (c + 1) * LANES)

    def stacked_bias(band, col_idx=None, lo=0, hi=TK):
        if col_idx is not None:
            band = band + jnp.where((col_idx < lo) | (col_idx >= hi), NEG_INF, 0.0)
        return jnp.concatenate([band, band], axis=0)

    bias16 = stacked_bias(band_ref[0], col, jnp.where(first, BAND_HALF, 0),
                          jnp.where(last, TK - BAND_HALF, TK))

    def body16(r, carry):
        kw = jnp.concatenate([kp_ref[0, r], kc_ref[0, r], kx_ref[0, r]], axis=0)
        vw = jnp.concatenate([vp_ref[0, r], vc_ref[0, r], vx_ref[0, r]], axis=0)
        res = _attend(q_ref[0, r].astype(BF16), kw, vw, bias16, hmask)
        for c, (m, l, acc) in enumerate(res):
            acc_ref[r, :, chunk(c)] = _side_by_side(acc, low)
            for e in range(2):
                rows = slice(e * TL, (e + 1) * TL)
                m_ref[c, e, r] = jnp.broadcast_to(m[rows], (TL, LANES))
                l_ref[c, e, r] = l[rows]
        return carry

    lax.fori_loop(0, PLANES, body16, 0, unroll=TILE_GROUP)

    def mid_window(cur, prev, nxt, plane, lb):
        lo = lb * MID_ROWS - MID_LEAD
        if lo < 0:
            return jnp.concatenate([prev[0, plane, TL // 2 + lo:TL // 2],
                                    cur[0, plane, 0:lo + MID_KEYS]], axis=0)
        if lo + MID_KEYS > TL:
            return jnp.concatenate([cur[0, plane, lo:TL],
                                    nxt[0, plane, 0:lo + MID_KEYS - TL]], axis=0)
        return cur[0, plane, lo:lo + MID_KEYS]

    def body4(r4, carry):
        planes = [r4 + MID_DIL * j for j in range(MID_DIL)]
        block_rows = lambda lb: slice(lb * MID_ROWS, (lb + 1) * MID_ROWS)
        for lb in range(TL // MID_ROWS):
            band = band_ref[1]
            if lb == 0:
                bias = stacked_bias(band, col % MID_KEYS, jnp.where(first, MID_LEAD, 0), MID_KEYS)
            elif lb == TL // MID_ROWS - 1:
                bias = stacked_bias(band, col % MID_KEYS, 0,
                                    jnp.where(last, MID_KEYS - MID_LEAD, MID_KEYS))
            else:
                bias = stacked_bias(band)
            q = jnp.concatenate([q_ref[0, p, block_rows(lb)] for p in planes],
                                axis=0).astype(BF16)
            kw = jnp.concatenate([mid_window(kc_ref, kp_ref, kx_ref, p, lb) for p in planes],
                                 axis=0)
            rows = block_rows(lb)
            stacked = lambda ref, c: jnp.concatenate(
                [ref[c, e, p, rows] for e in range(2) for p in planes], axis=0)
            vw = jnp.concatenate([mid_window(vc_ref, vp_ref, vx_ref, p, lb) for p in planes],
                                 axis=0)
            m_old = [stacked(m_ref, c) for c in range(HW // LANES)]
            for c, new in enumerate(_attend(q, kw, vw, bias, hmask, m_old)):
                acc_old = jnp.concatenate([acc_ref[p, rows, chunk(c)] for p in planes], axis=0)
                m_m, l_m, acc_m = _merge(new, m_old[c], stacked(l_ref, c), acc_old, low)
                for j, p in enumerate(planes):
                    piece = slice(j * MID_ROWS, (j + 1) * MID_ROWS)
                    acc_ref[p, rows, chunk(c)] = acc_m[piece]
                    for e in range(2):
                        head_piece = slice(e * TL + j * MID_ROWS, e * TL + (j + 1) * MID_ROWS)
                        m_ref[c, e, p, rows] = m_m[head_piece]
                        l_ref[c, e, p, rows] = l_m[head_piece]
        return carry

    lax.fori_loop(0, MID_DIL, body4, 0)

    n_tiles = SUPER // TL
    for ext_ref, prev, cur, nxt in ((kne_ref, knp_ref, knc_ref, knx_ref),
                                    (vne_ref, vnp_ref, vnc_ref, vnx_ref)):
        ext_ref[0:BAND_HALF] = prev[0]
        ext_ref[BAND_HALF:BAND_HALF + SUPER] = cur[0]
        ext_ref[BAND_HALF + SUPER:] = nxt[0]
    band1 = stacked_bias(band_ref[2])

    def body1(g, carry):
        tiles = [g * TILE_GROUP + i for i in range(TILE_GROUP)]
        tile_rows = lambda t: pl.ds(pl.multiple_of(t * ONE_ROWS, ONE_ROWS), ONE_ROWS)
        tile_keys = lambda t: pl.ds(pl.multiple_of(t * TL, TL), TK)
        scores = []
        for t in tiles:
            lo = jnp.where(first & (t == 0), BAND_HALF, 0)
            hi = jnp.where(last & (t == n_tiles - 1), TK - BAND_HALF, TK)
            bias = band1 + jnp.where((col < lo) | (col >= hi), NEG_INF, 0.0)
            q = q_ref[0, :, tile_rows(t), :].reshape(TL, HW).astype(BF16)
            scores.append(_scores(q, kne_ref[tile_keys(t), :], bias, hmask))
        for t, s in zip(tiles, scores):
            rows = tile_rows(t)
            stacked = lambda ref, c: jnp.concatenate(
                [ref[c, e, :, rows, :].reshape(TL, LANES) for e in range(2)], axis=0)
            m_old = [stacked(m_ref, c) for c in range(HW // LANES)]
            outs = []
            for c, new in enumerate(_softmax_pv(s, vne_ref[tile_keys(t), :], m_old)):
                acc_old = acc_ref[:, rows, chunk(c)].reshape(TL, LANES)
                _, l_m, acc_m = _merge(new, m_old[c], stacked(l_ref, c), acc_old, low)
                outs.append((acc_m / _side_by_side(l_m, low)).astype(BF16))
            merged = jnp.concatenate(outs, axis=1)
            tok = jnp.dot(unperm_ref[...], merged, preferred_element_type=F32)
            o_ref[0, pl.ds(pl.multiple_of(t * TL, TL), TL), :] = tok.astype(BF16)
        return carry

    lax.fori_loop(0, n_tiles // TILE_GROUP, body1, 0)


def _dil_attn(q16, k16, v16, kn, vn, band, unperm):
    B, S, _ = kn.shape
    n_half = S // PLANES // BAND_HALF
    n_tok_half = S // BAND_HALF
    per = TL // BAND_HALF
    per_tok = SUPER // BAND_HALF
    pcur = pl.BlockSpec((1, PLANES, TL, HW), lambda b, s, hh: (b, 0, s, hh))
    pprev = pl.BlockSpec((1, PLANES, BAND_HALF, HW),
                         lambda b, s, hh: (b, 0, jnp.maximum(s * per - 1, 0), hh))
    pnext = pl.BlockSpec((1, PLANES, BAND_HALF, HW),
                         lambda b, s, hh: (b, 0, jnp.minimum((s + 1) * per, n_half - 1), hh))
    tcur = pl.BlockSpec((1, SUPER, HW), lambda b, s, hh: (b, s, hh))
    tprev = pl.BlockSpec((1, BAND_HALF, HW),
                         lambda b, s, hh: (b, jnp.maximum(s * per_tok - 1, 0), hh))
    tnext = pl.BlockSpec((1, BAND_HALF, HW),
                         lambda b, s, hh: (b, jnp.minimum((s + 1) * per_tok, n_tok_half - 1), hh))
    return pl.pallas_call(
        _dil_attn_kernel,
        grid=(B, S // SUPER, ATTN_W // HW),
        in_specs=[pcur, pcur, pprev, pnext, pcur, pprev, pnext,
                  tcur, tprev, tnext, tcur, tprev, tnext,
                  _const_spec((3, TL, TK)), _const_spec((TL, TL))],
        out_specs=tcur,
        out_shape=jax.ShapeDtypeStruct((B, S, ATTN_W), BF16),
        scratch_shapes=[pltpu.VMEM((PLANES, TL, HW), F32),
                        pltpu.VMEM((HW // LANES, 2, PLANES, TL, LANES), F32),
                        pltpu.VMEM((HW // LANES, 2, PLANES, TL, LANES), F32),
                        pltpu.VMEM((SUPER + 2 * BAND_HALF, HW), BF16),
                        pltpu.VMEM((SUPER + 2 * BAND_HALF, HW), BF16)],
        compiler_params=_cparams(3),
        name="dil_attn",
    )(q16, k16, k16, k16, v16, v16, v16, kn, kn, kn, vn, vn, vn, band, unperm)


def _halo_specs(width, n_tiles, rows, tile=TM):
    per = tile // rows
    last = n_tiles * per - 1
    prev = pl.BlockSpec((1, rows, width), lambda b, t: (b, jnp.maximum(t * per - 1, 0), 0))
    nxt = pl.BlockSpec((1, rows, width), lambda b, t: (b, jnp.minimum((t + 1) * per, last), 0))
    return prev, nxt


def _conv_module(buf, n_out, w_ref, b_ref, g_ref, beta_ref):
    rows = n_out + 2 * HALO
    acc = jnp.zeros((n_out, CONV_W), F32) + b_ref[...]
    base = HALO - CONV_K // 2
    for shift in range(SUBLANES):
        rolled = buf if shift == 0 else pltpu.roll(buf, rows - shift, 0)
        for tap in range(CONV_K):
            off = base + tap
            if off % SUBLANES == shift:
                lo = off - shift
                acc = acc + w_ref[tap:tap + 1, :] * rolled[lo:lo + n_out]
    mu = jnp.mean(acc, axis=-1, keepdims=True)
    d = acc - mu
    var = jnp.mean(d * d, axis=-1, keepdims=True)
    z = d * lax.rsqrt(var + NORM_EPS) * g_ref[...] + beta_ref[...]
    return z * _sigmoid(z)


def _out_ffn_kernel(a_ref, ap_ref, ax_ref, c_ref, cp_ref, cx_ref, m_ref, mp_ref, mx_ref,
                    x_ref, xp_ref, xx_ref, wo_ref, g_ref, wu_ref, dw_ref, db_ref, wd_ref,
                    o_ref, gate_ref):
    t = pl.program_id(1)
    nt = pl.num_programs(1)
    has_next, has_prev = t < nt - 1, t > 0
    rows = TF + 2 * HALO

    def ext_rows(cur, nxt, prev):
        return jnp.concatenate([cur[0], nxt[0], prev[0]], axis=0)

    rid = lax.broadcasted_iota(jnp.int32, (rows, 1), 0)
    in_seq = (rid < TF) | ((rid < TF + HALO) & has_next) | ((rid >= TF + HALO) & has_prev)
    mixed = jnp.concatenate([ext_rows(a_ref, ax_ref, ap_ref), ext_rows(c_ref, cx_ref, cp_ref),
                             ext_rows(m_ref, mx_ref, mp_ref)], axis=-1)
    mixed = jnp.where(in_seq, mixed, jnp.zeros_like(mixed))
    h = (jnp.where(in_seq, ext_rows(x_ref, xx_ref, xp_ref), 0.0)
         + jnp.dot(mixed, wo_ref[...], preferred_element_type=F32))
    ext = _rms_rows(h, g_ref[...]).astype(BF16)

    def conv3(f, lo):
        w = dw_ref[:, lo:lo + FF_CH]
        y = (w[0:1] * pltpu.roll(f, 1, 0) + w[1:2] * f + w[2:3] * pltpu.roll(f, rows - 1, 0))
        return y[:TF] + db_ref[:, lo:lo + FF_CH]

    for j in range(D_FF // FF_CH):
        lo_g, lo_u = j * FF_CH, D_FF + j * FF_CH
        fg = conv3(jnp.dot(ext, wu_ref[:, lo_g:lo_g + FF_CH], preferred_element_type=F32), lo_g)
        fu = conv3(jnp.dot(ext, wu_ref[:, lo_u:lo_u + FF_CH], preferred_element_type=F32), lo_u)
        gate_ref[:, lo_g:lo_g + FF_CH] = (fg * _sigmoid(fg) * fu).astype(BF16)
    o_ref[0] = h[:TF] + jnp.dot(gate_ref[...], wd_ref[...], preferred_element_type=F32)


def _out_ffn(attn, cb, mo, x, w_out, g, w_up, dw_w, dw_b, w_down):
    B, S, _ = x.shape

    def with_halos(width):
        prev, nxt = _halo_specs(width, S // TF, HALO, TF)
        return [pl.BlockSpec((1, TF, width), lambda b, t: (b, t, 0)), prev, nxt]

    return pl.pallas_call(
        _out_ffn_kernel,
        grid=(B, S // TF),
        in_specs=with_halos(ATTN_W) + with_halos(CONV_W) + with_halos(MEM_W)
        + with_halos(D_MODEL)
        + [_const_spec((D_MODEL, D_MODEL)), _const_spec((1, D_MODEL)),
           _const_spec((D_MODEL, 2 * D_FF)), _const_spec((FFN_CONV_K, 2 * D_FF)),
           _const_spec((1, 2 * D_FF)), _const_spec((D_FF, D_MODEL))],
        out_specs=pl.BlockSpec((1, TF, D_MODEL), lambda b, t: (b, t, 0)),
        out_shape=jax.ShapeDtypeStruct((B, S, D_MODEL), F32),
        scratch_shapes=[pltpu.VMEM((TF, D_FF), BF16)],
        compiler_params=_cparams(2),
        name="out_ffn",
    )(attn, attn, attn, cb, cb, cb, mo, mo, mo, x, x, x, w_out, g, w_up, dw_w, dw_b, w_down)


def _rope_tables():
    inv_freq = ROPE_THETA ** (-jnp.arange(0, ROT_DIM, 2, dtype=F32) / ROT_DIM)
    invf = jnp.broadcast_to(inv_freq[:, None], (SUBLANES, LANES))
    half = ROT_DIM // 2
    lane = np.arange(LANES) % HEAD_DIM
    lanes = np.zeros((SUBLANES, LANES), np.float32)
    lanes[0] = lane >= ROT_DIM
    lanes[1] = lane < half
    expand = np.zeros((LANES, 2 * LANES), np.float32)
    for l in range(LANES):
        if lane[l] < ROT_DIM:
            j = lane[l] % half
            expand[[j, half + j], l] = 1.0
            expand[[2 * half + j, 3 * half + j], LANES + l] = -1.0 if lane[l] < half else 1.0
    return invf, jnp.asarray(lanes), jnp.asarray(expand, BF16)


def _group_sum_matrix():
    idx = np.arange(MXU_DIM) // HEAD_DIM
    return jnp.asarray(idx[:, None] == idx[None, :], BF16)


def _plane_perm(n):
    out = np.arange(n)
    src = (out % (n // PLANES)) * PLANES + out // (n // PLANES)
    return np.asarray(src[:, None] == np.arange(n)[None, :], np.float32)


def kernel(x, mem, positions, mix_norm_g, mem_norm_g, w_in, w_mem_kv, q_norm_g, k_norm_g, mq_norm_g, mk_norm_g, conv_dw_w, conv_dw_b, conv_ln_g, conv_ln_b, w_out, ffn_norm_g, w_up, ffn_dw_w, ffn_dw_b, w_down):
    B, S, _ = x.shape
    depth = w_in.shape[0]
    pos_row = positions.reshape(B, 1, S)
    invf, lane_tab, expand = _rope_tables()
    gsum = _group_sum_matrix()
    perm = jnp.asarray(_plane_perm(TM), BF16)
    unperm = jnp.asarray(_plane_perm(TL).T, BF16)
    band = _band_tables()
    row = lambda a: a.reshape(1, -1)
    h = x
    for l in range(depth):
        kmt, vm = _mem_kv(mem, row(mem_norm_g[l]), w_mem_kv[l].astype(BF16),
                          row(jnp.tile(mk_norm_g[l], MEM_HEADS)), gsum)
        q16, k16, v16, kn, vn, cb, mo = _in_proj(
            h, pos_row, row(mix_norm_g[l]), w_in[l].astype(BF16),
            row(jnp.tile(q_norm_g[l], ATTN_HEADS)), row(jnp.tile(k_norm_g[l], ATTN_HEADS)),
            row(jnp.tile(mq_norm_g[l], MEM_HEADS)), gsum, invf, lane_tab, expand, perm,
            conv_dw_w[l], row(conv_dw_b[l]), row(conv_ln_g[l]), row(conv_ln_b[l]), kmt, vm)
        attn = _dil_attn(q16, k16, v16, kn, vn, band, unperm)
        h = _out_ffn(attn, cb, mo, h, w_out[l].astype(BF16), row(ffn_norm_g[l]),
                     w_up[l].astype(BF16), ffn_dw_w[l], row(ffn_dw_b[l]), w_down[l].astype(BF16))
    return h
```

```python
import functools

import numpy as np
import jax
import jax.numpy as jnp
from jax import lax
from jax.experimental import pallas as pl
from jax.experimental.pallas import tpu as pltpu

F32 = jnp.float32
BF16 = jnp.bfloat16

D_MODEL = 1024
HEAD_DIM = 64
ATTN_HEADS = 8
ATTN_W = ATTN_HEADS * HEAD_DIM
CONV_W = 256
MEM_HEADS = 4
MEM_W = MEM_HEADS * HEAD_DIM
N_MEM = 256
PLANES = 16
MID_DIL = 4
BAND_HALF = 64
ROPE_THETA = 500000.0
ROT_DIM = HEAD_DIM // 4
CONV_K = 31
FFN_CONV_K = 3
D_FF = 2816
NORM_EPS = 1e-6
NEG_INF = -1e30
SM_SCALE = HEAD_DIM ** -0.5

LANES = 128
SUBLANES = 8
MXU_DIM = 256
BF16_ROWS = 16
VMEM_LIMIT = 56 * 1024 * 1024

TM = 512
TF = 512
TL = 128
TK = TL + 2 * BAND_HALF
SUPER = PLANES * TL
FF_CH = MXU_DIM
HALO = BF16_ROWS


def _cparams(n_axes):
    return pltpu.CompilerParams(dimension_semantics=("parallel",) * n_axes,
                                vmem_limit_bytes=VMEM_LIMIT)


def _const_spec(shape):
    return pl.BlockSpec(shape, lambda *_: (0,) * len(shape), pipeline_mode=pl.Buffered(1))


def _rms_rows(x, g):
    r = lax.rsqrt(jnp.mean(x * x, axis=-1, keepdims=True) + NORM_EPS)
    return x * r * g


def _head_sumsq(t, gsum):
    return [jnp.dot((tc * tc).astype(BF16), gsum, preferred_element_type=F32)
            for tc in (t[:, c * MXU_DIM:(c + 1) * MXU_DIM] for c in range(t.shape[1] // MXU_DIM))]


def _head_scale(t, sumsq, gain):
    outs = []
    for c, ssum in enumerate(sumsq):
        sl = slice(c * MXU_DIM, (c + 1) * MXU_DIM)
        outs.append(t[:, sl] * lax.rsqrt(ssum * (1.0 / HEAD_DIM) + NORM_EPS) * gain[:, sl])
    return outs


def _head_norm(t, gain, gsum):
    return _head_scale(t, _head_sumsq(t, gsum), gain)


def _sigmoid(x):
    return 1.0 / (1.0 + jnp.exp(-x))


def _in_proj_kernel(x_ref, xp_ref, xx_ref, pos_ref, g_ref, w_ref, gq_ref, gk_ref, gm_ref,
                    gsum_ref, invf_ref, lane_ref, expand_ref, perm_ref,
                    cw_ref, cbias_ref, cg_ref, cbeta_ref, kmt_ref, vm_ref,
                    q16_ref, k16_ref, v16_ref, kn_ref, vn_ref, cb_ref, mo_ref):
    t = pl.program_id(1)
    has_prev, has_next = t > 0, t < pl.num_programs(1) - 1
    hn = _rms_rows(x_ref[0], g_ref[...]).astype(BF16)

    hn_halo = _rms_rows(jnp.concatenate([xp_ref[0], xx_ref[0]], axis=0), g_ref[...]).astype(BF16)
    hn_ext = jnp.concatenate([hn_halo[:HALO], hn, hn_halo[HALO:]], axis=0)
    cols = lambda lo, n: w_ref[:, lo:lo + n]
    q = jnp.dot(hn, cols(0, ATTN_W), preferred_element_type=F32)
    k = jnp.dot(hn, cols(ATTN_W, ATTN_W), preferred_element_type=F32)
    vb = jnp.dot(hn, cols(2 * ATTN_W, ATTN_W), preferred_element_type=F32).astype(BF16)
    glu = jnp.dot(hn_ext, cols(3 * ATTN_W, 2 * CONV_W), preferred_element_type=F32)
    qm = jnp.dot(hn, cols(3 * ATTN_W + 2 * CONV_W, MEM_W), preferred_element_type=F32)

    ang = invf_ref[:, 0:1] * pos_ref[0].astype(F32)

    def hi_lo(v):
        hi = v.astype(BF16).astype(F32)
        return [hi, (v - hi).astype(BF16).astype(F32)]

    tab = jnp.concatenate(hi_lo(jnp.cos(ang)) + hi_lo(jnp.sin(ang))
                          + [jnp.zeros((LANES - 4 * SUBLANES, TM), F32)], axis=0)
    cs = jnp.dot(tab.T.astype(BF16), expand_ref[...], preferred_element_type=F32)
    cosv = cs[:, :LANES] + lane_ref[0:1, :]
    sinv = cs[:, LANES:]
    first_half = lane_ref[1:2, :] > 0.5

    def rotary(chunks, scale):
        out = []
        for t in chunks:
            for s in range(MXU_DIM // LANES):
                xc = t[:, s * LANES:(s + 1) * LANES]
                partner = jnp.where(first_half, pltpu.roll(xc, LANES - ROT_DIM // 2, 1),
                                    pltpu.roll(xc, ROT_DIM // 2, 1))
                out.append(((xc * cosv + partner * sinv) * scale).astype(BF16))
        return jnp.concatenate(out, axis=1)

    gsum = gsum_ref[...]
    q_ss, k_ss, qm_ss = _head_sumsq(q, gsum), _head_sumsq(k, gsum), _head_sumsq(qm, gsum)
    qb = rotary(_head_scale(q, q_ss, gq_ref[...]), SM_SCALE)
    kb = rotary(_head_scale(k, k_ss, gk_ref[...]), 1.0)
    kn_ref[0] = kb
    vn_ref[0] = vb
    qkv = jnp.dot(perm_ref[...], jnp.concatenate([qb, kb, vb], axis=1),
                  preferred_element_type=F32)
    rows = TM // PLANES
    for r in range(PLANES):
        blk = qkv[r * rows:(r + 1) * rows]
        q16_ref[0, r] = blk[:, 0:ATTN_W]
        k16_ref[0, r] = blk[:, ATTN_W:2 * ATTN_W].astype(BF16)
        v16_ref[0, r] = blk[:, 2 * ATTN_W:].astype(BF16)

    (qmn,) = _head_scale(qm, qm_ss, gm_ref[...])
    mo_ref[0] = _mem_attn((qmn * SM_SCALE).astype(BF16), kmt_ref, vm_ref)

    rid = lax.broadcasted_iota(jnp.int32, (TM + 2 * HALO, 1), 0)
    in_seq = ((rid >= HALO) | has_prev) & ((rid < TM + HALO) | has_next)
    cbuf = jnp.where(in_seq, glu[:, :CONV_W] * _sigmoid(glu[:, CONV_W:]), 0.0)
    cb_ref[0] = _conv_module(cbuf, TM, cw_ref, cbias_ref, cg_ref, cbeta_ref).astype(BF16)


def _in_proj(x, pos_row, g, w_in, gq, gk, gm, gsum, invf, lane_tab, expand, perm,
             conv_w, conv_b, conv_g, conv_beta, kmt, vm):
    B, S, _ = x.shape
    in_cols = w_in.shape[1]
    tok = lambda w: pl.BlockSpec((1, TM, w), lambda b, t: (b, t, 0))
    xprev, xnext = _halo_specs(D_MODEL, S // TM, HALO)
    plane = pl.BlockSpec((1, PLANES, TM // PLANES, ATTN_W), lambda b, t: (b, 0, t, 0))
    plane_shape = (B, PLANES, S // PLANES, ATTN_W)
    per_batch = lambda r, w: pl.BlockSpec((1, r, w), lambda b, t: (b, 0, 0))
    return pl.pallas_call(
        _in_proj_kernel,
        grid=(B, S // TM),
        in_specs=[tok(D_MODEL), xprev, xnext, pl.BlockSpec((1, 1, TM), lambda b, t: (b, 0, t)),
                  _const_spec((1, D_MODEL)), _const_spec((D_MODEL, in_cols)),
                  _const_spec((1, ATTN_W)), _const_spec((1, ATTN_W)), _const_spec((1, MEM_W)),
                  _const_spec((MXU_DIM, MXU_DIM)), _const_spec((SUBLANES, LANES)),
                  _const_spec((SUBLANES, LANES)), _const_spec((LANES, 2 * LANES)),
                  _const_spec((TM, TM)),
                  _const_spec((CONV_K, CONV_W)), _const_spec((1, CONV_W)),
                  _const_spec((1, CONV_W)), _const_spec((1, CONV_W)),
                  per_batch(MEM_W, N_MEM), per_batch(N_MEM, MEM_W)],
        out_specs=[plane, plane, plane, tok(ATTN_W), tok(ATTN_W), tok(CONV_W), tok(MEM_W)],
        out_shape=[jax.ShapeDtypeStruct(plane_shape, F32),
                   jax.ShapeDtypeStruct(plane_shape, BF16),
                   jax.ShapeDtypeStruct(plane_shape, BF16),
                   jax.ShapeDtypeStruct((B, S, ATTN_W), BF16),
                   jax.ShapeDtypeStruct((B, S, ATTN_W), BF16),
                   jax.ShapeDtypeStruct((B, S, CONV_W), BF16),
                   jax.ShapeDtypeStruct((B, S, MEM_W), BF16)],
        compiler_params=_cparams(2),
        name="in_proj",
    )(x, x, x, pos_row, g, w_in, gq, gk, gm, gsum, invf, lane_tab, expand, perm,
      conv_w, conv_b, conv_g, conv_beta, kmt, vm)


def _mem_kv_kernel(mem_ref, g_ref, w_ref, gk_ref, gsum_ref, kmt_ref, vm_ref):
    mn = _rms_rows(mem_ref[0], g_ref[...]).astype(BF16)
    kv = jnp.dot(mn, w_ref[...], preferred_element_type=F32)
    (km,) = _head_norm(kv[:, :MEM_W], gk_ref[...], gsum_ref[...])
    kmt_ref[0] = km.T.astype(BF16)
    vm_ref[0] = kv[:, MEM_W:].astype(BF16)


def _mem_kv(mem, g, w, gk, gsum):
    B = mem.shape[0]
    return pl.pallas_call(
        _mem_kv_kernel,
        grid=(B,),
        in_specs=[pl.BlockSpec((1, N_MEM, D_MODEL), lambda b: (b, 0, 0)),
                  _const_spec((1, D_MODEL)), _const_spec((D_MODEL, 2 * MEM_W)),
                  _const_spec((1, MEM_W)), _const_spec((MXU_DIM, MXU_DIM))],
        out_specs=[pl.BlockSpec((1, MEM_W, N_MEM), lambda b: (b, 0, 0)),
                   pl.BlockSpec((1, N_MEM, MEM_W), lambda b: (b, 0, 0))],
        out_shape=[jax.ShapeDtypeStruct((B, MEM_W, N_MEM), BF16),
                   jax.ShapeDtypeStruct((B, N_MEM, MEM_W), BF16)],
        compiler_params=_cparams(1),
        name="mem_kv",
    )(mem, g, w, gk, gsum)


def _head_masks():
    lane = lax.broadcasted_iota(jnp.int32, (1, LANES), 1)
    low = lane < HEAD_DIM
    return low, (jnp.where(low, 1.0, 0.0).astype(BF16), jnp.where(low, 0.0, 1.0).astype(BF16))


def _mem_attn(qm, kmt_ref, vm_ref):
    low, hmask = _head_masks()
    out = []
    for c in range(MEM_W // LANES):
        sl = slice(c * LANES, (c + 1) * LANES)
        qc = qm[:, sl]
        halves = []
        for e in range(2):
            s = jnp.dot(qc * hmask[e], kmt_ref[0, sl, :], preferred_element_type=F32)
            m = jnp.max(s, axis=-1, keepdims=True)
            p = jnp.exp(s - m)
            l = jnp.sum(p, axis=-1, keepdims=True)
            o = jnp.dot(p.astype(BF16), vm_ref[0, :, sl], preferred_element_type=F32)
            halves.append(o / l)
        out.append(jnp.where(low, halves[0], halves[1]).astype(BF16))
    return jnp.concatenate(out, axis=1)


HW = ATTN_W // 2
MID_ROWS = TL // MID_DIL
MID_KEYS = TK // MID_DIL
MID_LEAD = (MID_KEYS - MID_ROWS) // 2
ONE_ROWS = TL // PLANES
TILE_GROUP = 8


def _band_tables():
    rho = np.arange(TL)[:, None]
    kap = np.arange(TK)[None, :]
    d16 = kap - BAND_HALF - rho
    j, lq = rho // MID_ROWS, rho % MID_ROWS
    jk, lk = kap // MID_KEYS, kap % MID_KEYS
    d4 = MID_DIL * (lk - MID_LEAD - lq) + (jk - j)
    r, l1 = rho // ONE_ROWS, rho % ONE_ROWS
    d1 = kap - BAND_HALF - PLANES * l1 - r
    tabs = [np.where(np.abs(d) <= BAND_HALF, 0.0, NEG_INF) for d in (d16, d4, d1)]
    return jnp.asarray(np.stack(tabs), F32)


ALL_CHUNKS = tuple(range(HW // LANES))


def _scores(q, kw, bias2, hmask, chunks=ALL_CHUNKS):
    out = []
    for c in chunks:
        sl = slice(c * LANES, (c + 1) * LANES)
        qc = q[:, sl]
        qs = jnp.concatenate([qc * hmask[0], qc * hmask[1]], axis=0)
        out.append(lax.dot_general(qs, kw[:, sl], (((1,), (1,)), ((), ())),
                                   preferred_element_type=F32) + bias2)
    return out


def _softmax_pv(scores, vw, low, hmask, m_old=None, chunks=ALL_CHUNKS):
    ones = jnp.ones((TK, LANES), BF16)
    res = []
    for c, s in zip(chunks, scores):
        m = jnp.max(s, axis=-1, keepdims=True)
        if m_old is None:
            shift = m
        else:
            m = jnp.maximum(m, m_old[c])
            shift = jnp.concatenate([m] * (TK // LANES), axis=1)
        p = jnp.exp((s - shift).astype(BF16))
        v_aug = jnp.concatenate([vw[:, c * LANES:(c + 1) * LANES], ones], axis=1)
        r = jnp.dot(p, v_aug, preferred_element_type=F32)
        res.append((m, jnp.where(low, r[:TL, LANES:], r[TL:, LANES:]),
                    jnp.where(low, r[:TL, :LANES], r[TL:, :LANES])))
    return res


def _attend(q, kw, vw, bias2, low, hmask, m_old=None):
    res = []
    for c in ALL_CHUNKS:
        res += _softmax_pv(_scores(q, kw, bias2, hmask, (c,)), vw, low, hmask, m_old, (c,))
    return res


def _merge(new, m_old, l_old, acc_old, low):
    m, l_n, acc_n = new
    b = jnp.exp(m_old - m)
    b = jnp.where(low, b[:TL], b[TL:])
    return m, l_n + b * l_old, acc_n + b * acc_old


def _dil_attn_kernel(q_ref, kc_ref, kp_ref, kx_ref, vc_ref, vp_ref, vx_ref,
                     knc_ref, knp_ref, knx_ref, vnc_ref, vnp_ref, vnx_ref, band_ref, unperm_ref,
                     o_ref, acc_ref, m_ref, l_ref, kne_ref, vne_ref):
    st = pl.program_id(1)
    first, last = st == 0, st == pl.num_programs(1) - 1
    low, hmask = _head_masks()
    col = lax.broadcasted_iota(jnp.int32, (1, TK), 1)
    chunk = lambda c: slice(c * LANES, (c + 1) * LANES)

    def stacked_bias(band, col_idx=None, lo=0, hi=TK):
        if col_idx is not None:
            band = band + jnp.where((col_idx < lo) | (col_idx >= hi), NEG_INF, 0.0)
        return jnp.concatenate([band, band], axis=0)

    bias16 = stacked_bias(band_ref[0], col, jnp.where(first, BAND_HALF, 0),
                          jnp.where(last, TK - BAND_HALF, TK))

    def pipelined(tiles, score_fn, finish_fn):
        s_next = score_fn(tiles[0])
        for i, t in enumerate(tiles):
            s = s_next
            if i + 1 < len(tiles):
                s_next = score_fn(tiles[i + 1])
            finish_fn(t, s)

    def body16(g, carry):
        def score(r):
            kw = jnp.concatenate([kp_ref[0, r], kc_ref[0, r], kx_ref[0, r]], axis=0)
            return _scores(q_ref[0, r].astype(BF16), kw, bias16, hmask)

        def finish(r, s):
            vw = jnp.concatenate([vp_ref[0, r], vc_ref[0, r], vx_ref[0, r]], axis=0)
            for c, (m, l, acc) in enumerate(_softmax_pv(s, vw, low, hmask)):
                acc_ref[r, :, chunk(c)] = acc
                l_ref[c, r] = l
                for e in range(2):
                    m_ref[c, e, r] = jnp.broadcast_to(m[e * TL:(e + 1) * TL], (TL, LANES))

        pipelined([g * TILE_GROUP + i for i in range(TILE_GROUP)], score, finish)
        return carry

    lax.fori_loop(0, PLANES // TILE_GROUP, body16, 0)

    def mid_window(cur, prev, nxt, plane, lb):
        lo = lb * MID_ROWS - MID_LEAD
        if lo < 0:
            return jnp.concatenate([prev[0, plane, TL // 2 + lo:TL // 2],
                                    cur[0, plane, 0:lo + MID_KEYS]], axis=0)
        if lo + MID_KEYS > TL:
            return jnp.concatenate([cur[0, plane, lo:TL],
                                    nxt[0, plane, 0:lo + MID_KEYS - TL]], axis=0)
        return cur[0, plane, lo:lo + MID_KEYS]

    def body4(r4, carry):
        planes = [r4 + MID_DIL * j for j in range(MID_DIL)]
        block_rows = lambda lb: slice(lb * MID_ROWS, (lb + 1) * MID_ROWS)

        def score(lb):
            band = band_ref[1]
            if lb == 0:
                bias = stacked_bias(band, col % MID_KEYS, jnp.where(first, MID_LEAD, 0), MID_KEYS)
            elif lb == TL // MID_ROWS - 1:
                bias = stacked_bias(band, col % MID_KEYS, 0,
                                    jnp.where(last, MID_KEYS - MID_LEAD, MID_KEYS))
            else:
                bias = stacked_bias(band)
            q = jnp.concatenate([q_ref[0, p, block_rows(lb)] for p in planes],
                                axis=0).astype(BF16)
            kw = jnp.concatenate([mid_window(kc_ref, kp_ref, kx_ref, p, lb) for p in planes],
                                 axis=0)
            return _scores(q, kw, bias, hmask)

        def finish(lb, s):
            rows = block_rows(lb)
            stacked = lambda ref, c: jnp.concatenate(
                [ref[c, e, p, rows] for e in range(2) for p in planes], axis=0)
            vw = jnp.concatenate([mid_window(vc_ref, vp_ref, vx_ref, p, lb) for p in planes],
                                 axis=0)
            m_old = [stacked(m_ref, c) for c in range(HW // LANES)]
            for c, new in enumerate(_softmax_pv(s, vw, low, hmask, m_old)):
                acc_old = jnp.concatenate([acc_ref[p, rows, chunk(c)] for p in planes], axis=0)
                l_old = jnp.concatenate([l_ref[c, p, rows] for p in planes], axis=0)
                m_m, l_m, acc_m = _merge(new, m_old[c], l_old, acc_old, low)
                for j, p in enumerate(planes):
                    piece = slice(j * MID_ROWS, (j + 1) * MID_ROWS)
                    acc_ref[p, rows, chunk(c)] = acc_m[piece]
                    l_ref[c, p, rows] = l_m[piece]
                    for e in range(2):
                        head_piece = slice(e * TL + j * MID_ROWS, e * TL + (j + 1) * MID_ROWS)
                        m_ref[c, e, p, rows] = m_m[head_piece]

        pipelined(list(range(TL // MID_ROWS)), score, finish)
        return carry

    lax.fori_loop(0, MID_DIL, body4, 0)

    n_tiles = SUPER // TL
    for ext_ref, prev, cur, nxt in ((kne_ref, knp_ref, knc_ref, knx_ref),
                                    (vne_ref, vnp_ref, vnc_ref, vnx_ref)):
        ext_ref[0:BAND_HALF] = prev[0]
        ext_ref[BAND_HALF:BAND_HALF + SUPER] = cur[0]
        ext_ref[BAND_HALF + SUPER:] = nxt[0]
    band1 = stacked_bias(band_ref[2])

    def body1(g, carry):
        tiles = [g * TILE_GROUP + i for i in range(TILE_GROUP)]
        tile_rows = lambda t: pl.ds(pl.multiple_of(t * ONE_ROWS, ONE_ROWS), ONE_ROWS)
        tile_keys = lambda t: pl.ds(pl.multiple_of(t * TL, TL), TK)

        def tile_scores(t):
            lo = jnp.where(first & (t == 0), BAND_HALF, 0)
            hi = jnp.where(last & (t == n_tiles - 1), TK - BAND_HALF, TK)
            bias = band1 + jnp.where((col < lo) | (col >= hi), NEG_INF, 0.0)
            q = q_ref[0, :, tile_rows(t), :].reshape(TL, HW).astype(BF16)
            return _scores(q, kne_ref[tile_keys(t), :], bias, hmask)

        def emit(t, merged):
            tok = jnp.dot(unperm_ref[...], merged, preferred_element_type=F32)
            o_ref[0, pl.ds(pl.multiple_of(t * TL, TL), TL), :] = tok.astype(BF16)

        pending = []

        def finish(t, s):
            rows = tile_rows(t)
            stacked = lambda ref, c: jnp.concatenate(
                [ref[c, e, :, rows, :].reshape(TL, LANES) for e in range(2)], axis=0)
            m_old = [stacked(m_ref, c) for c in range(HW // LANES)]
            outs = []
            for c, new in enumerate(_softmax_pv(s, vne_ref[tile_keys(t), :], low, hmask, m_old)):
                acc_old = acc_ref[:, rows, chunk(c)].reshape(TL, LANES)
                l_old = l_ref[c, :, rows, :].reshape(TL, LANES)
                _, l_m, acc_m = _merge(new, m_old[c], l_old, acc_old, low)
                outs.append((acc_m / l_m).astype(BF16))
            if pending:
                emit(*pending.pop())
            pending.append((t, jnp.concatenate(outs, axis=1)))

        pipelined(tiles, tile_scores, finish)
        emit(*pending.pop())
        return carry

    lax.fori_loop(0, n_tiles // TILE_GROUP, body1, 0)


def _dil_attn(q16, k16, v16, kn, vn, band, unperm):
    B, S, _ = kn.shape
    n_half = S // PLANES // BAND_HALF
    n_tok_half = S // BAND_HALF
    per = TL // BAND_HALF
    per_tok = SUPER // BAND_HALF
    pcur = pl.BlockSpec((1, PLANES, TL, HW), lambda b, s, hh: (b, 0, s, hh))
    pprev = pl.BlockSpec((1, PLANES, BAND_HALF, HW),
                         lambda b, s, hh: (b, 0, jnp.maximum(s * per - 1, 0), hh))
    pnext = pl.BlockSpec((1, PLANES, BAND_HALF, HW),
                         lambda b, s, hh: (b, 0, jnp.minimum((s + 1) * per, n_half - 1), hh))
    tcur = pl.BlockSpec((1, SUPER, HW), lambda b, s, hh: (b, s, hh))
    tprev = pl.BlockSpec((1, BAND_HALF, HW),
                         lambda b, s, hh: (b, jnp.maximum(s * per_tok - 1, 0), hh))
    tnext = pl.BlockSpec((1, BAND_HALF, HW),
                         lambda b, s, hh: (b, jnp.minimum((s + 1) * per_tok, n_tok_half - 1), hh))
    return pl.pallas_call(
        _dil_attn_kernel,
        grid=(B, S // SUPER, ATTN_W // HW),
        in_specs=[pcur, pcur, pprev, pnext, pcur, pprev, pnext,
                  tcur, tprev, tnext, tcur, tprev, tnext,
                  _const_spec((3, TL, TK)), _const_spec((TL, TL))],
        out_specs=tcur,
        out_shape=jax.ShapeDtypeStruct((B, S, ATTN_W), BF16),
        scratch_shapes=[pltpu.VMEM((PLANES, TL, HW), F32),
                        pltpu.VMEM((HW // LANES, 2, PLANES, TL, LANES), F32),
                        pltpu.VMEM((HW // LANES, PLANES, TL, LANES), F32),
                        pltpu.VMEM((SUPER + 2 * BAND_HALF, HW), BF16),
                        pltpu.VMEM((SUPER + 2 * BAND_HALF, HW), BF16)],
        compiler_params=_cparams(3),
        name="dil_attn",
    )(q16, k16, k16, k16, v16, v16, v16, kn, kn, kn, vn, vn, vn, band, unperm)


def _halo_specs(width, n_tiles, rows, tile=TM):
    per = tile // rows
    last = n_tiles * per - 1
    prev = pl.BlockSpec((1, rows, width), lambda b, t: (b, jnp.maximum(t * per - 1, 0), 0))
    nxt = pl.BlockSpec((1, rows, width), lambda b, t: (b, jnp.minimum((t + 1) * per, last), 0))
    return prev, nxt


def _conv_module(buf, n_out, w_ref, b_ref, g_ref, beta_ref):
    rows = n_out + 2 * HALO
    acc = jnp.zeros((n_out, CONV_W), F32) + b_ref[...]
    base = HALO - CONV_K // 2
    for shift in range(SUBLANES):
        rolled = buf if shift == 0 else pltpu.roll(buf, rows - shift, 0)
        for tap in range(CONV_K):
            off = base + tap
            if off % SUBLANES == shift:
                lo = off - shift
                acc = acc + w_ref[tap:tap + 1, :] * rolled[lo:lo + n_out]
    mu = jnp.mean(acc, axis=-1, keepdims=True)
    d = acc - mu
    var = jnp.mean(d * d, axis=-1, keepdims=True)
    z = d * lax.rsqrt(var + NORM_EPS) * g_ref[...] + beta_ref[...]
    return z * _sigmoid(z)


def _out_ffn_kernel(a_ref, ap_ref, ax_ref, c_ref, cp_ref, cx_ref, m_ref, mp_ref, mx_ref,
                    x_ref, xp_ref, xx_ref, wo_ref, g_ref, wu_ref, dw_ref, db_ref, wd_ref,
                    o_ref, gate_ref):
    t = pl.program_id(1)
    nt = pl.num_programs(1)
    has_next, has_prev = t < nt - 1, t > 0
    rows = TF + 2 * HALO

    def ext_rows(cur, nxt, prev):
        return jnp.concatenate([cur[0], nxt[0], prev[0]], axis=0)

    rid = lax.broadcasted_iota(jnp.int32, (rows, 1), 0)
    in_seq = (rid < TF) | ((rid < TF + HALO) & has_next) | ((rid >= TF + HALO) & has_prev)
    mixed = jnp.concatenate([ext_rows(a_ref, ax_ref, ap_ref), ext_rows(c_ref, cx_ref, cp_ref),
                             ext_rows(m_ref, mx_ref, mp_ref)], axis=-1)
    mixed = jnp.where(in_seq, mixed, jnp.zeros_like(mixed))
    h = (jnp.where(in_seq, ext_rows(x_ref, xx_ref, xp_ref), 0.0)
         + jnp.dot(mixed, wo_ref[...], preferred_element_type=F32))
    ext = _rms_rows(h, g_ref[...]).astype(BF16)

    def conv3(f, lo):
        w = dw_ref[:, lo:lo + FF_CH]
        y = (w[0:1] * pltpu.roll(f, 1, 0) + w[1:2] * f + w[2:3] * pltpu.roll(f, rows - 1, 0))
        return y[:TF] + db_ref[:, lo:lo + FF_CH]

    n_chunks = D_FF // FF_CH
    out = h[:TF]
    for j in range(n_chunks):
        lo_g, lo_u = j * FF_CH, D_FF + j * FF_CH
        f_g = jnp.dot(ext, wu_ref[:, lo_g:lo_g + FF_CH], preferred_element_type=F32)
        f_u = jnp.dot(ext, wu_ref[:, lo_u:lo_u + FF_CH], preferred_element_type=F32)
        if j == n_chunks - 1:
            out = out + jnp.dot(gate_ref[:, :lo_g], wd_ref[:lo_g, :], preferred_element_type=F32)
        fg, fu = conv3(f_g, lo_g), conv3(f_u, lo_u)
        gate_ref[:, lo_g:lo_g + FF_CH] = (fg * _sigmoid(fg) * fu).astype(BF16)
    o_ref[0] = out + jnp.dot(gate_ref[:, D_FF - FF_CH:], wd_ref[D_FF - FF_CH:, :],
                             preferred_element_type=F32)


def _out_ffn(attn, cb, mo, x, w_out, g, w_up, dw_w, dw_b, w_down):
    B, S, _ = x.shape

    def with_halos(width):
        prev, nxt = _halo_specs(width, S // TF, HALO, TF)
        return [pl.BlockSpec((1, TF, width), lambda b, t: (b, t, 0)), prev, nxt]

    return pl.pallas_call(
        _out_ffn_kernel,
        grid=(B, S // TF),
        in_specs=with_halos(ATTN_W) + with_halos(CONV_W) + with_halos(MEM_W)
        + with_halos(D_MODEL)
        + [_const_spec((D_MODEL, D_MODEL)), _const_spec((1, D_MODEL)),
           _const_spec((D_MODEL, 2 * D_FF)), _const_spec((FFN_CONV_K, 2 * D_FF)),
           _const_spec((1, 2 * D_FF)), _const_spec((D_FF, D_MODEL))],
        out_specs=pl.BlockSpec((1, TF, D_MODEL), lambda b, t: (b, t, 0)),
        out_shape=jax.ShapeDtypeStruct((B, S, D_MODEL), F32),
        scratch_shapes=[pltpu.VMEM((TF, D_FF), BF16)],
        compiler_params=_cparams(2),
        name="out_ffn",
    )(attn, attn, attn, cb, cb, cb, mo, mo, mo, x, x, x, w_out, g, w_up, dw_w, dw_b, w_down)


def _rope_tables():
    inv_freq = ROPE_THETA ** (-jnp.arange(0, ROT_DIM, 2, dtype=F32) / ROT_DIM)
    invf = jnp.broadcast_to(inv_freq[:, None], (SUBLANES, LANES))
    half = ROT_DIM // 2
    lane = np.arange(LANES) % HEAD_DIM
    lanes = np.zeros((SUBLANES, LANES), np.float32)
    lanes[0] = lane >= ROT_DIM
    lanes[1] = lane < half
    expand = np.zeros((LANES, 2 * LANES), np.float32)
    for l in range(LANES):
        if lane[l] < ROT_DIM:
            j = lane[l] % half
            expand[[j, half + j], l] = 1.0
            expand[[2 * half + j, 3 * half + j], LANES + l] = -1.0 if lane[l] < half else 1.0
    return invf, jnp.asarray(lanes), jnp.asarray(expand, BF16)


def _group_sum_matrix():
    idx = np.arange(MXU_DIM) // HEAD_DIM
    return jnp.asarray(idx[:, None] == idx[None, :], BF16)


def _plane_perm(n):
    out = np.arange(n)
    src = (out % (n // PLANES)) * PLANES + out // (n // PLANES)
    return np.asarray(src[:, None] == np.arange(n)[None, :], np.float32)


def kernel(x, mem, positions, mix_norm_g, mem_norm_g, w_in, w_mem_kv, q_norm_g, k_norm_g, mq_norm_g, mk_norm_g, conv_dw_w, conv_dw_b, conv_ln_g, conv_ln_b, w_out, ffn_norm_g, w_up, ffn_dw_w, ffn_dw_b, w_down):
    B, S, _ = x.shape
    depth = w_in.shape[0]
    pos_row = positions.reshape(B, 1, S)
    invf, lane_tab, expand = _rope_tables()
    gsum = _group_sum_matrix()
    perm = jnp.asarray(_plane_perm(TM), BF16)
    unperm = jnp.asarray(_plane_perm(TL).T, BF16)
    band = _band_tables()
    row = lambda a: a.reshape(1, -1)
    h = x
    for l in range(depth):
        kmt, vm = _mem_kv(mem, row(mem_norm_g[l]), w_mem_kv[l].astype(BF16),
                          row(jnp.tile(mk_norm_g[l], MEM_HEADS)), gsum)
        q16, k16, v16, kn, vn, cb, mo = _in_proj(
            h, pos_row, row(mix_norm_g[l]), w_in[l].astype(BF16),
            row(jnp.tile(q_norm_g[l], ATTN_HEADS)), row(jnp.tile(k_norm_g[l], ATTN_HEADS)),
            row(jnp.tile(mq_norm_g[l], MEM_HEADS)), gsum, invf, lane_tab, expand, perm,
            conv_dw_w[l], row(conv_dw_b[l]), row(conv_ln_g[l]), row(conv_ln_b[l]), kmt, vm)
        attn = _dil_attn(q16, k16, v16, kn, vn, band, unperm)
        h = _out_ffn(attn, cb, mo, h, w_out[l].astype(BF16), row(ffn_norm_g[l]),
                     w_up[l].astype(BF16), ffn_dw_w[l], row(ffn_dw_b[l]), w_down[l].astype(BF16))
    return h
```

```python
import functools

import numpy as np
import jax
import jax.numpy as jnp
from jax import lax
from jax.experimental import pallas as pl
from jax.experimental.pallas import tpu as pltpu

F32 = jnp.float32
BF16 = jnp.bfloat16

D_MODEL = 1024
HEAD_DIM = 64
ATTN_HEADS = 8
ATTN_W = ATTN_HEADS * HEAD_DIM
CONV_W = 256
MEM_HEADS = 4
MEM_W = MEM_HEADS * HEAD_DIM
N_MEM = 256
PLANES = 16
MID_DIL = 4
BAND_HALF = 64
ROPE_THETA = 500000.0
ROT_DIM = HEAD_DIM // 4
CONV_K = 31
FFN_CONV_K = 3
D_FF = 2816
NORM_EPS = 1e-6
NEG_INF = -1e30
SM_SCALE = HEAD_DIM ** -0.5

LANES = 128
SUBLANES = 8
MXU_DIM = 256
BF16_ROWS = 16
VMEM_LIMIT = 56 * 1024 * 1024

TM = 512
TF = 512
TL = 128
TK = TL + 2 * BAND_HALF
SUPER = PLANES * TL
FF_CH = MXU_DIM
HALO = BF16_ROWS


def _cparams(n_axes):
    return pltpu.CompilerParams(dimension_semantics=("parallel",) * n_axes,
                                vmem_limit_bytes=VMEM_LIMIT)


def _const_spec(shape):
    return pl.BlockSpec(shape, lambda *_: (0,) * len(shape), pipeline_mode=pl.Buffered(1))


def _rms_rows(x, g):
    r = lax.rsqrt(jnp.mean(x * x, axis=-1, keepdims=True) + NORM_EPS)
    return x * r * g


def _head_sumsq(t, gsum):
    return [jnp.dot((tc * tc).astype(BF16), gsum, preferred_element_type=F32)
            for tc in (t[:, c * MXU_DIM:(c + 1) * MXU_DIM] for c in range(t.shape[1] // MXU_DIM))]


def _head_scale(t, sumsq, gain):
    outs = []
    for c, ssum in enumerate(sumsq):
        sl = slice(c * MXU_DIM, (c + 1) * MXU_DIM)
        outs.append(t[:, sl] * lax.rsqrt(ssum * (1.0 / HEAD_DIM) + NORM_EPS) * gain[:, sl])
    return outs


def _head_norm(t, gain, gsum):
    return _head_scale(t, _head_sumsq(t, gsum), gain)


def _sigmoid(x):
    return 1.0 / (1.0 + jnp.exp(-x))


IN_COLS = 3 * ATTN_W + 2 * CONV_W + MEM_W
IN_CAST = MXU_DIM
N_CAST_IN = IN_COLS // IN_CAST


def _in_proj_kernel(x_ref, xp_ref, xx_ref, pos_ref, g_ref, w32_ref, gq_ref, gk_ref, gm_ref,
                    gsum_ref, invf_ref, lane_ref, expand_ref,
                    cw_ref, cbias_ref, cg_ref, cbeta_ref, kmt_ref, vm_ref,
                    q16_ref, k16_ref, v16_ref, kn_ref, vn_ref, cb_ref, mo_ref,
                    slab_ref, mid_ref, w_ref, *, n_tiles):
    step = pl.program_id(0)

    for c in range(N_CAST_IN):
        @pl.when(step == c)
        def _(c=c):
            w_ref[:, c * IN_CAST:(c + 1) * IN_CAST] = w32_ref[...].astype(BF16)

    @pl.when(step >= N_CAST_IN)
    def _():
        _in_proj_tile(x_ref, xp_ref, xx_ref, pos_ref, g_ref, w_ref, gq_ref, gk_ref, gm_ref,
                      gsum_ref, invf_ref, lane_ref, expand_ref,
                      cw_ref, cbias_ref, cg_ref, cbeta_ref, kmt_ref, vm_ref,
                      q16_ref, k16_ref, v16_ref, kn_ref, vn_ref, cb_ref, mo_ref,
                      slab_ref, mid_ref, lax.rem(step - N_CAST_IN, n_tiles), n_tiles)


def _in_proj_tile(x_ref, xp_ref, xx_ref, pos_ref, g_ref, w_ref, gq_ref, gk_ref, gm_ref,
                  gsum_ref, invf_ref, lane_ref, expand_ref,
                  cw_ref, cbias_ref, cg_ref, cbeta_ref, kmt_ref, vm_ref,
                  q16_ref, k16_ref, v16_ref, kn_ref, vn_ref, cb_ref, mo_ref,
                  slab_ref, mid_ref, t, nt):
    has_prev, has_next = t > 0, t < nt - 1
    hn = _rms_rows(x_ref[0], g_ref[...]).astype(BF16)

    hn_halo = _rms_rows(jnp.concatenate([xp_ref[0], xx_ref[0]], axis=0), g_ref[...]).astype(BF16)
    hn_ext = jnp.concatenate([hn_halo[:HALO], hn, hn_halo[HALO:]], axis=0)
    cols = lambda lo, n: w_ref[:, lo:lo + n]
    q = jnp.dot(hn, cols(0, ATTN_W), preferred_element_type=F32)
    k = jnp.dot(hn, cols(ATTN_W, ATTN_W), preferred_element_type=F32)
    v = jnp.dot(hn, cols(2 * ATTN_W, ATTN_W), preferred_element_type=F32)
    glu = jnp.dot(hn_ext, cols(3 * ATTN_W, 2 * CONV_W), preferred_element_type=F32)
    qm = jnp.dot(hn, cols(3 * ATTN_W + 2 * CONV_W, MEM_W), preferred_element_type=F32)

    ang = invf_ref[:, 0:1] * pos_ref[0].astype(F32)

    def hi_lo(v):
        hi = v.astype(BF16).astype(F32)
        return [hi, (v - hi).astype(BF16).astype(F32)]

    tab = jnp.concatenate(hi_lo(jnp.cos(ang)) + hi_lo(jnp.sin(ang))
                          + [jnp.zeros((LANES - 4 * SUBLANES, TM), F32)], axis=0)
    cs = jnp.dot(tab.T.astype(BF16), expand_ref[...], preferred_element_type=F32)
    cosv = cs[:, :LANES] + lane_ref[0:1, :]
    sinv = cs[:, LANES:]
    first_half = lane_ref[1:2, :] > 0.5

    def rotary(chunks, scale):
        out = []
        for t in chunks:
            for s in range(MXU_DIM // LANES):
                xc = t[:, s * LANES:(s + 1) * LANES]
                partner = jnp.where(first_half, pltpu.roll(xc, LANES - ROT_DIM // 2, 1),
                                    pltpu.roll(xc, ROT_DIM // 2, 1))
                out.append((xc * cosv + partner * sinv) * scale)
        return out

    gsum = gsum_ref[...]
    q_ss, k_ss, qm_ss = _head_sumsq(q, gsum), _head_sumsq(k, gsum), _head_sumsq(qm, gsum)
    q_chunks = rotary(_head_scale(q, q_ss, gq_ref[...]), SM_SCALE)
    k_chunks = rotary(_head_scale(k, k_ss, gk_ref[...]), 1.0)
    v_chunks = [v[:, s * LANES:(s + 1) * LANES] for s in range(ATTN_W // LANES)]
    kn_ref[0] = jnp.concatenate(k_chunks, axis=1).astype(BF16)
    vn_ref[0] = v.astype(BF16)

    quarter, rows = TM // MID_DIL, TM // PLANES
    for ch, val in enumerate(q_chunks + k_chunks + v_chunks):
        slab_ref[ch] = val
    for ch in range(3 * ATTN_W // LANES):
        for r_lo in range(MID_DIL):
            mid_ref[ch, r_lo] = slab_ref[ch, pl.ds(r_lo, quarter, stride=MID_DIL), :]
    n_ch = ATTN_W // LANES
    for r in range(PLANES):
        r_lo, r_hi = r % MID_DIL, r // MID_DIL
        pieces = [mid_ref[ch, r_lo, pl.ds(r_hi, rows, stride=MID_DIL), :]
                  for ch in range(3 * n_ch)]
        q16_ref[0, r] = jnp.concatenate(pieces[:n_ch], axis=1)
        k16_ref[0, r] = jnp.concatenate(pieces[n_ch:2 * n_ch], axis=1).astype(BF16)
        v16_ref[0, r] = jnp.concatenate(pieces[2 * n_ch:], axis=1).astype(BF16)

    (qmn,) = _head_scale(qm, qm_ss, gm_ref[...])
    mo_ref[0] = _mem_attn((qmn * SM_SCALE).astype(BF16), kmt_ref, vm_ref)

    rid = lax.broadcasted_iota(jnp.int32, (TM + 2 * HALO, 1), 0)
    in_seq = ((rid >= HALO) | has_prev) & ((rid < TM + HALO) | has_next)
    cbuf = jnp.where(in_seq, glu[:, :CONV_W] * _sigmoid(glu[:, CONV_W:]), 0.0)
    cb_ref[0] = _conv_module(cbuf, TM, cw_ref, cbias_ref, cg_ref, cbeta_ref).astype(BF16)


def _in_proj(x, pos_row, g, w_in, gq, gk, gm, gsum, invf, lane_tab, expand,
             conv_w, conv_b, conv_g, conv_beta, kmt, vm):
    B, S, _ = x.shape
    nt = S // TM
    per = TM // HALO

    def tile_of(step):
        i = jnp.maximum(step - N_CAST_IN, 0)
        return i // nt, i % nt

    def at_tile(fn):
        return lambda step: fn(*tile_of(step))

    tok = lambda w: pl.BlockSpec((1, TM, w), at_tile(lambda b, t: (b, t, 0)))
    xprev = pl.BlockSpec((1, HALO, D_MODEL),
                         at_tile(lambda b, t: (b, jnp.maximum(t * per - 1, 0), 0)))
    xnext = pl.BlockSpec((1, HALO, D_MODEL),
                         at_tile(lambda b, t: (b, jnp.minimum((t + 1) * per, nt * per - 1), 0)))
    plane = pl.BlockSpec((1, PLANES, TM // PLANES, ATTN_W), at_tile(lambda b, t: (b, 0, t, 0)))
    plane_shape = (B, PLANES, S // PLANES, ATTN_W)
    per_batch = lambda r, w: pl.BlockSpec((1, r, w), at_tile(lambda b, t: (b, 0, 0)))
    return pl.pallas_call(
        functools.partial(_in_proj_kernel, n_tiles=nt),
        grid=(N_CAST_IN + B * nt,),
        in_specs=[tok(D_MODEL), xprev, xnext,
                  pl.BlockSpec((1, 1, TM), at_tile(lambda b, t: (b, 0, t))),
                  _const_spec((1, D_MODEL)),
                  pl.BlockSpec((D_MODEL, IN_CAST),
                               lambda step: (0, jnp.minimum(step, N_CAST_IN - 1))),
                  _const_spec((1, ATTN_W)), _const_spec((1, ATTN_W)), _const_spec((1, MEM_W)),
                  _const_spec((MXU_DIM, MXU_DIM)), _const_spec((SUBLANES, LANES)),
                  _const_spec((SUBLANES, LANES)), _const_spec((LANES, 2 * LANES)),
                  _const_spec((CONV_K, CONV_W)), _const_spec((1, CONV_W)),
                  _const_spec((1, CONV_W)), _const_spec((1, CONV_W)),
                  per_batch(MEM_W, N_MEM), per_batch(N_MEM, MEM_W)],
        out_specs=[plane, plane, plane, tok(ATTN_W), tok(ATTN_W), tok(CONV_W), tok(MEM_W)],
        out_shape=[jax.ShapeDtypeStruct(plane_shape, F32),
                   jax.ShapeDtypeStruct(plane_shape, BF16),
                   jax.ShapeDtypeStruct(plane_shape, BF16),
                   jax.ShapeDtypeStruct((B, S, ATTN_W), BF16),
                   jax.ShapeDtypeStruct((B, S, ATTN_W), BF16),
                   jax.ShapeDtypeStruct((B, S, CONV_W), BF16),
                   jax.ShapeDtypeStruct((B, S, MEM_W), BF16)],
        scratch_shapes=[pltpu.VMEM((3 * ATTN_W // LANES, TM, LANES), F32),
                        pltpu.VMEM((3 * ATTN_W // LANES, MID_DIL, TM // MID_DIL, LANES), F32),
                        pltpu.VMEM((D_MODEL, IN_COLS), BF16)],
        compiler_params=pltpu.CompilerParams(dimension_semantics=("arbitrary",),
                                             vmem_limit_bytes=VMEM_LIMIT),
        name="in_proj",
    )(x, x, x, pos_row, g, w_in, gq, gk, gm, gsum, invf, lane_tab, expand,
      conv_w, conv_b, conv_g, conv_beta, kmt, vm)


def _mem_kv_kernel(mem_ref, g_ref, w_ref, gk_ref, gsum_ref, kmt_ref, vm_ref):
    mn = _rms_rows(mem_ref[0], g_ref[...]).astype(BF16)
    kv = jnp.dot(mn, w_ref[...].astype(BF16), preferred_element_type=F32)
    (km,) = _head_norm(kv[:, :MEM_W], gk_ref[...], gsum_ref[...])
    kmt_ref[0] = km.T.astype(BF16)
    vm_ref[0] = kv[:, MEM_W:].astype(BF16)


def _mem_kv(mem, g, w, gk, gsum):
    B = mem.shape[0]
    return pl.pallas_call(
        _mem_kv_kernel,
        grid=(B,),
        in_specs=[pl.BlockSpec((1, N_MEM, D_MODEL), lambda b: (b, 0, 0)),
                  _const_spec((1, D_MODEL)), _const_spec((D_MODEL, 2 * MEM_W)),
                  _const_spec((1, MEM_W)), _const_spec((MXU_DIM, MXU_DIM))],
        out_specs=[pl.BlockSpec((1, MEM_W, N_MEM), lambda b: (b, 0, 0)),
                   pl.BlockSpec((1, N_MEM, MEM_W), lambda b: (b, 0, 0))],
        out_shape=[jax.ShapeDtypeStruct((B, MEM_W, N_MEM), BF16),
                   jax.ShapeDtypeStruct((B, N_MEM, MEM_W), BF16)],
        compiler_params=_cparams(1),
        name="mem_kv",
    )(mem, g, w, gk, gsum)


def _head_masks():
    lane = lax.broadcasted_iota(jnp.int32, (1, LANES), 1)
    low = lane < HEAD_DIM
    return low, (jnp.where(low, 1.0, 0.0).astype(BF16), jnp.where(low, 0.0, 1.0).astype(BF16))


def _mem_attn(qm, kmt_ref, vm_ref):
    low, hmask = _head_masks()
    out = []
    for c in range(MEM_W // LANES):
        sl = slice(c * LANES, (c + 1) * LANES)
        qc = qm[:, sl]
        halves = []
        for e in range(2):
            s = jnp.dot(qc * hmask[e], kmt_ref[0, sl, :], preferred_element_type=F32)
            m = jnp.max(s, axis=-1, keepdims=True)
            p = jnp.exp(s - m)
            l = jnp.sum(p, axis=-1, keepdims=True)
            o = jnp.dot(p.astype(BF16), vm_ref[0, :, sl], preferred_element_type=F32)
            halves.append(o / l)
        out.append(jnp.where(low, halves[0], halves[1]).astype(BF16))
    return jnp.concatenate(out, axis=1)


HW = ATTN_W // 2
MID_ROWS = TL // MID_DIL
MID_KEYS = TK // MID_DIL
MID_LEAD = (MID_KEYS - MID_ROWS) // 2
ONE_ROWS = TL // PLANES
TILE_GROUP = 8


def _band_tables():
    rho = np.arange(TL)[:, None]
    kap = np.arange(TK)[None, :]
    d16 = kap - BAND_HALF - rho
    j, lq = rho // MID_ROWS, rho % MID_ROWS
    jk, lk = kap // MID_KEYS, kap % MID_KEYS
    d4 = MID_DIL * (lk - MID_LEAD - lq) + (jk - j)
    r, l1 = rho // ONE_ROWS, rho % ONE_ROWS
    d1 = kap - BAND_HALF - PLANES * l1 - r
    tabs = [np.where(np.abs(d) <= BAND_HALF, 0.0, NEG_INF) for d in (d16, d4, d1)]
    return jnp.asarray(np.stack(tabs), F32)


ALL_CHUNKS = tuple(range(HW // LANES))


def _scores(q, kw, bias2, hmask, chunks=ALL_CHUNKS):
    out = []
    for c in chunks:
        sl = slice(c * LANES, (c + 1) * LANES)
        qc = q[:, sl]
        qs = jnp.concatenate([qc * hmask[0], qc * hmask[1]], axis=0)
        out.append(lax.dot_general(qs, kw[:, sl], (((1,), (1,)), ((), ())),
                                   preferred_element_type=F32) + bias2)
    return out


def _softmax_pv(scores, vw, low, hmask, m_old=None, chunks=ALL_CHUNKS):
    ones = jnp.ones((TK, LANES), BF16)
    res = []
    for c, s in zip(chunks, scores):
        m = jnp.max(s, axis=-1, keepdims=True)
        if m_old is None:
            shift = m
        else:
            m = jnp.maximum(m, m_old[c])
            shift = jnp.concatenate([m] * (TK // LANES), axis=1)
        p = jnp.exp((s - shift).astype(BF16))
        v_aug = jnp.concatenate([vw[:, c * LANES:(c + 1) * LANES], ones], axis=1)
        r = jnp.dot(p, v_aug, preferred_element_type=F32)
        res.append((m, jnp.where(low, r[:TL, LANES:], r[TL:, LANES:]),
                    jnp.where(low, r[:TL, :LANES], r[TL:, :LANES])))
    return res


def _attend(q, kw, vw, bias2, low, hmask, m_old=None):
    res = []
    for c in ALL_CHUNKS:
        res += _softmax_pv(_scores(q, kw, bias2, hmask, (c,)), vw, low, hmask, m_old, (c,))
    return res


def _merge(new, m_old, l_old, acc_old, low):
    m, l_n, acc_n = new
    b = jnp.exp(m_old - m)
    b = jnp.where(low, b[:TL], b[TL:])
    return m, l_n + b * l_old, acc_n + b * acc_old


def _dil_attn_kernel(q_ref, kc_ref, kp_ref, kx_ref, vc_ref, vp_ref, vx_ref,
                     knc_ref, knp_ref, knx_ref, vnc_ref, vnp_ref, vnx_ref, band_ref, unperm_ref,
                     o_ref, acc_ref, m_ref, l_ref, kne_ref, vne_ref):
    st = pl.program_id(1)
    first, last = st == 0, st == pl.num_programs(1) - 1
    low, hmask = _head_masks()
    col = lax.broadcasted_iota(jnp.int32, (1, TK), 1)
    chunk = lambda c: slice(c * LANES, (c + 1) * LANES)

    def stacked_bias(band, col_idx=None, lo=0, hi=TK):
        if col_idx is not None:
            band = band + jnp.where((col_idx < lo) | (col_idx >= hi), NEG_INF, 0.0)
        return jnp.concatenate([band, band], axis=0)

    bias16 = stacked_bias(band_ref[0], col, jnp.where(first, BAND_HALF, 0),
                          jnp.where(last, TK - BAND_HALF, TK))

    def pipelined(tiles, score_fn, finish_fn):
        s_next = score_fn(tiles[0])
        for i, t in enumerate(tiles):
            s = s_next
            if i + 1 < len(tiles):
                s_next = score_fn(tiles[i + 1])
            finish_fn(t, s)

    def body16(g, carry):
        def score(r):
            kw = jnp.concatenate([kp_ref[0, r], kc_ref[0, r], kx_ref[0, r]], axis=0)
            return _scores(q_ref[0, r].astype(BF16), kw, bias16, hmask)

        def finish(r, s):
            vw = jnp.concatenate([vp_ref[0, r], vc_ref[0, r], vx_ref[0, r]], axis=0)
            for c, (m, l, acc) in enumerate(_softmax_pv(s, vw, low, hmask)):
                acc_ref[r, :, chunk(c)] = acc
                l_ref[c, r] = l
                for e in range(2):
                    m_ref[c, e, r] = jnp.broadcast_to(m[e * TL:(e + 1) * TL], (TL, LANES))

        pipelined([g * TILE_GROUP + i for i in range(TILE_GROUP)], score, finish)
        return carry

    lax.fori_loop(0, PLANES // TILE_GROUP, body16, 0)

    def mid_window(cur, prev, nxt, plane, lb):
        lo = lb * MID_ROWS - MID_LEAD
        if lo < 0:
            return jnp.concatenate([prev[0, plane, TL // 2 + lo:TL // 2],
                                    cur[0, plane, 0:lo + MID_KEYS]], axis=0)
        if lo + MID_KEYS > TL:
            return jnp.concatenate([cur[0, plane, lo:TL],
                                    nxt[0, plane, 0:lo + MID_KEYS - TL]], axis=0)
        return cur[0, plane, lo:lo + MID_KEYS]

    def body4(r4, carry):
        planes = [r4 + MID_DIL * j for j in range(MID_DIL)]
        block_rows = lambda lb: slice(lb * MID_ROWS, (lb + 1) * MID_ROWS)

        def score(lb):
            band = band_ref[1]
            if lb == 0:
                bias = stacked_bias(band, col % MID_KEYS, jnp.where(first, MID_LEAD, 0), MID_KEYS)
            elif lb == TL // MID_ROWS - 1:
                bias = stacked_bias(band, col % MID_KEYS, 0,
                                    jnp.where(last, MID_KEYS - MID_LEAD, MID_KEYS))
            else:
                bias = stacked_bias(band)
            q = jnp.concatenate([q_ref[0, p, block_rows(lb)] for p in planes],
                                axis=0).astype(BF16)
            kw = jnp.concatenate([mid_window(kc_ref, kp_ref, kx_ref, p, lb) for p in planes],
                                 axis=0)
            return _scores(q, kw, bias, hmask)

        def finish(lb, s):
            rows = block_rows(lb)
            stacked = lambda ref, c: jnp.concatenate(
                [ref[c, e, p, rows] for e in range(2) for p in planes], axis=0)
            vw = jnp.concatenate([mid_window(vc_ref, vp_ref, vx_ref, p, lb) for p in planes],
                                 axis=0)
            m_old = [stacked(m_ref, c) for c in range(HW // LANES)]
            for c, new in enumerate(_softmax_pv(s, vw, low, hmask, m_old)):
                acc_old = jnp.concatenate([acc_ref[p, rows, chunk(c)] for p in planes], axis=0)
                l_old = jnp.concatenate([l_ref[c, p, rows] for p in planes], axis=0)
                m_m, l_m, acc_m = _merge(new, m_old[c], l_old, acc_old, low)
                for j, p in enumerate(planes):
                    piece = slice(j * MID_ROWS, (j + 1) * MID_ROWS)
                    acc_ref[p, rows, chunk(c)] = acc_m[piece]
                    l_ref[c, p, rows] = l_m[piece]
                    for e in range(2):
                        head_piece = slice(e * TL + j * MID_ROWS, e * TL + (j + 1) * MID_ROWS)
                        m_ref[c, e, p, rows] = m_m[head_piece]

        pipelined(list(range(TL // MID_ROWS)), score, finish)
        return carry

    lax.fori_loop(0, MID_DIL, body4, 0)

    n_tiles = SUPER // TL
    for ext_ref, prev, cur, nxt in ((kne_ref, knp_ref, knc_ref, knx_ref),
                                    (vne_ref, vnp_ref, vnc_ref, vnx_ref)):
        ext_ref[0:BAND_HALF] = prev[0]
        ext_ref[BAND_HALF:BAND_HALF + SUPER] = cur[0]
        ext_ref[BAND_HALF + SUPER:] = nxt[0]
    band1 = stacked_bias(band_ref[2])

    def body1(g, carry):
        tiles = [g * TILE_GROUP + i for i in range(TILE_GROUP)]
        tile_rows = lambda t: pl.ds(pl.multiple_of(t * ONE_ROWS, ONE_ROWS), ONE_ROWS)
        tile_keys = lambda t: pl.ds(pl.multiple_of(t * TL, TL), TK)

        def tile_scores(t):
            lo = jnp.where(first & (t == 0), BAND_HALF, 0)
            hi = jnp.where(last & (t == n_tiles - 1), TK - BAND_HALF, TK)
            bias = band1 + jnp.where((col < lo) | (col >= hi), NEG_INF, 0.0)
            q = q_ref[0, :, tile_rows(t), :].reshape(TL, HW).astype(BF16)
            return _scores(q, kne_ref[tile_keys(t), :], bias, hmask)

        def emit(t, merged):
            tok = jnp.dot(unperm_ref[...], merged, preferred_element_type=F32)
            o_ref[0, pl.ds(pl.multiple_of(t * TL, TL), TL), :] = tok.astype(BF16)

        pending = []

        def finish(t, s):
            rows = tile_rows(t)
            stacked = lambda ref, c: jnp.concatenate(
                [ref[c, e, :, rows, :].reshape(TL, LANES) for e in range(2)], axis=0)
            m_old = [stacked(m_ref, c) for c in range(HW // LANES)]
            outs = []
            for c, new in enumerate(_softmax_pv(s, vne_ref[tile_keys(t), :], low, hmask, m_old)):
                acc_old = acc_ref[:, rows, chunk(c)].reshape(TL, LANES)
                l_old = l_ref[c, :, rows, :].reshape(TL, LANES)
                _, l_m, acc_m = _merge(new, m_old[c], l_old, acc_old, low)
                outs.append((acc_m / l_m).astype(BF16))
            if pending:
                emit(*pending.pop())
            pending.append((t, jnp.concatenate(outs, axis=1)))

        pipelined(tiles, tile_scores, finish)
        emit(*pending.pop())
        return carry

    lax.fori_loop(0, n_tiles // TILE_GROUP, body1, 0)


def _dil_attn(q16, k16, v16, kn, vn, band, unperm):
    B, S, _ = kn.shape
    n_half = S // PLANES // BAND_HALF
    n_tok_half = S // BAND_HALF
    per = TL // BAND_HALF
    per_tok = SUPER // BAND_HALF
    pcur = pl.BlockSpec((1, PLANES, TL, HW), lambda b, s, hh: (b, 0, s, hh))
    pprev = pl.BlockSpec((1, PLANES, BAND_HALF, HW),
                         lambda b, s, hh: (b, 0, jnp.maximum(s * per - 1, 0), hh))
    pnext = pl.BlockSpec((1, PLANES, BAND_HALF, HW),
                         lambda b, s, hh: (b, 0, jnp.minimum((s + 1) * per, n_half - 1), hh))
    tcur = pl.BlockSpec((1, SUPER, HW), lambda b, s, hh: (b, s, hh))
    tprev = pl.BlockSpec((1, BAND_HALF, HW),
                         lambda b, s, hh: (b, jnp.maximum(s * per_tok - 1, 0), hh))
    tnext = pl.BlockSpec((1, BAND_HALF, HW),
                         lambda b, s, hh: (b, jnp.minimum((s + 1) * per_tok, n_tok_half - 1), hh))
    return pl.pallas_call(
        _dil_attn_kernel,
        grid=(B, S // SUPER, ATTN_W // HW),
        in_specs=[pcur, pcur, pprev, pnext, pcur, pprev, pnext,
                  tcur, tprev, tnext, tcur, tprev, tnext,
                  _const_spec((3, TL, TK)), _const_spec((TL, TL))],
        out_specs=tcur,
        out_shape=jax.ShapeDtypeStruct((B, S, ATTN_W), BF16),
        scratch_shapes=[pltpu.VMEM((PLANES, TL, HW), F32),
                        pltpu.VMEM((HW // LANES, 2, PLANES, TL, LANES), F32),
                        pltpu.VMEM((HW // LANES, PLANES, TL, LANES), F32),
                        pltpu.VMEM((SUPER + 2 * BAND_HALF, HW), BF16),
                        pltpu.VMEM((SUPER + 2 * BAND_HALF, HW), BF16)],
        compiler_params=_cparams(3),
        name="dil_attn",
    )(q16, k16, k16, k16, v16, v16, v16, kn, kn, kn, vn, vn, vn, band, unperm)


def _halo_specs(width, n_tiles, rows, tile=TM):
    per = tile // rows
    last = n_tiles * per - 1
    prev = pl.BlockSpec((1, rows, width), lambda b, t: (b, jnp.maximum(t * per - 1, 0), 0))
    nxt = pl.BlockSpec((1, rows, width), lambda b, t: (b, jnp.minimum((t + 1) * per, last), 0))
    return prev, nxt


def _conv_module(buf, n_out, w_ref, b_ref, g_ref, beta_ref):
    rows = n_out + 2 * HALO
    acc = jnp.zeros((n_out, CONV_W), F32) + b_ref[...]
    base = HALO - CONV_K // 2
    for shift in range(SUBLANES):
        rolled = buf if shift == 0 else pltpu.roll(buf, rows - shift, 0)
        for tap in range(CONV_K):
            off = base + tap
            if off % SUBLANES == shift:
                lo = off - shift
                acc = acc + w_ref[tap:tap + 1, :] * rolled[lo:lo + n_out]
    mu = jnp.mean(acc, axis=-1, keepdims=True)
    d = acc - mu
    var = jnp.mean(d * d, axis=-1, keepdims=True)
    z = d * lax.rsqrt(var + NORM_EPS) * g_ref[...] + beta_ref[...]
    return z * _sigmoid(z)


N_CAST = 11
UP_CAST = 2 * D_FF // N_CAST
DOWN_CAST = D_FF // N_CAST
OUT_CAST = LANES


def _out_ffn_kernel(a_ref, ap_ref, ax_ref, c_ref, cp_ref, cx_ref, m_ref, mp_ref, mx_ref,
                    x_ref, xp_ref, xx_ref, wo32_ref, g_ref, wu32_ref, dw_ref, db_ref, wd32_ref,
                    o_ref, gate_ref, wo_ref, wu_ref, wd_ref, *, n_tiles):
    step = pl.program_id(0)

    for c in range(N_CAST):
        @pl.when(step == c)
        def _(c=c):
            wu_ref[:, c * UP_CAST:(c + 1) * UP_CAST] = wu32_ref[...].astype(BF16)
            wd_ref[c * DOWN_CAST:(c + 1) * DOWN_CAST, :] = wd32_ref[...].astype(BF16)
            if (c + 1) * OUT_CAST <= D_MODEL:
                wo_ref[c * OUT_CAST:(c + 1) * OUT_CAST, :] = wo32_ref[...].astype(BF16)

    @pl.when(step >= N_CAST)
    def _():
        _out_ffn_tile(a_ref, ap_ref, ax_ref, c_ref, cp_ref, cx_ref, m_ref, mp_ref, mx_ref,
                      x_ref, xp_ref, xx_ref, wo_ref, g_ref, wu_ref, dw_ref, db_ref, wd_ref,
                      o_ref, gate_ref, lax.rem(step - N_CAST, n_tiles), n_tiles)


def _out_ffn_tile(a_ref, ap_ref, ax_ref, c_ref, cp_ref, cx_ref, m_ref, mp_ref, mx_ref,
                  x_ref, xp_ref, xx_ref, wo_ref, g_ref, wu_ref, dw_ref, db_ref, wd_ref,
                  o_ref, gate_ref, t, nt):
    has_next, has_prev = t < nt - 1, t > 0
    rows = TF + HALO

    def ext_rows(cur, nxt, prev):
        halo = jnp.concatenate([nxt[0, :SUBLANES].astype(F32), prev[0, SUBLANES:].astype(F32)],
                               axis=0)
        return jnp.concatenate([cur[0], halo.astype(cur.dtype)], axis=0)

    rid = lax.broadcasted_iota(jnp.int32, (rows, 1), 0)
    in_seq = ((rid < TF) | ((rid < TF + SUBLANES) & has_next)
              | ((rid >= TF + SUBLANES) & has_prev))
    mixed = jnp.concatenate([ext_rows(a_ref, ax_ref, ap_ref), ext_rows(c_ref, cx_ref, cp_ref),
                             ext_rows(m_ref, mx_ref, mp_ref)], axis=-1)
    mixed = jnp.where(in_seq, mixed, jnp.zeros_like(mixed))
    h = (jnp.where(in_seq, ext_rows(x_ref, xx_ref, xp_ref), 0.0)
         + jnp.dot(mixed, wo_ref[...], preferred_element_type=F32))
    ext = _rms_rows(h, g_ref[...]).astype(BF16)

    def conv3(f, lo):
        w = dw_ref[:, lo:lo + FF_CH]
        y = (w[0:1] * pltpu.roll(f, 1, 0) + w[1:2] * f + w[2:3] * pltpu.roll(f, rows - 1, 0))
        return y[:TF] + db_ref[:, lo:lo + FF_CH]

    n_chunks = D_FF // FF_CH
    out = h[:TF]
    for j in range(n_chunks):
        lo_g, lo_u = j * FF_CH, D_FF + j * FF_CH
        f_g = jnp.dot(ext, wu_ref[:, lo_g:lo_g + FF_CH], preferred_element_type=F32)
        f_u = jnp.dot(ext, wu_ref[:, lo_u:lo_u + FF_CH], preferred_element_type=F32)
        if j == n_chunks - 1:
            out = out + jnp.dot(gate_ref[:, :lo_g], wd_ref[:lo_g, :], preferred_element_type=F32)
        fg, fu = conv3(f_g, lo_g), conv3(f_u, lo_u)
        gate_ref[:, lo_g:lo_g + FF_CH] = (fg * _sigmoid(fg) * fu).astype(BF16)
    o_ref[0] = out + jnp.dot(gate_ref[:, D_FF - FF_CH:], wd_ref[D_FF - FF_CH:, :],
                             preferred_element_type=F32)


def _out_ffn(attn, cb, mo, x, w_out, g, w_up, dw_w, dw_b, w_down):
    B, S, _ = x.shape
    nt = S // TF
    per = TF // HALO

    def tile_of(step):
        i = jnp.maximum(step - N_CAST, 0)
        return i // nt, i % nt

    def with_halos(width):
        def cur(step):
            b, t = tile_of(step)
            return b, t, 0

        def prev(step):
            b, t = tile_of(step)
            return b, jnp.maximum(t * per - 1, 0), 0

        def nxt(step):
            b, t = tile_of(step)
            return b, jnp.minimum((t + 1) * per, nt * per - 1), 0

        return [pl.BlockSpec((1, TF, width), cur), pl.BlockSpec((1, HALO, width), prev),
                pl.BlockSpec((1, HALO, width), nxt)]

    cast_chunk = lambda limit: (lambda step: jnp.minimum(step, limit - 1))
    out_i, up_i, down_i = cast_chunk(D_MODEL // OUT_CAST), cast_chunk(N_CAST), cast_chunk(N_CAST)
    return pl.pallas_call(
        functools.partial(_out_ffn_kernel, n_tiles=nt),
        grid=(N_CAST + B * nt,),
        in_specs=with_halos(ATTN_W) + with_halos(CONV_W) + with_halos(MEM_W)
        + with_halos(D_MODEL)
        + [pl.BlockSpec((OUT_CAST, D_MODEL), lambda s: (out_i(s), 0)), _const_spec((1, D_MODEL)),
           pl.BlockSpec((D_MODEL, UP_CAST), lambda s: (0, up_i(s))),
           _const_spec((FFN_CONV_K, 2 * D_FF)), _const_spec((1, 2 * D_FF)),
           pl.BlockSpec((DOWN_CAST, D_MODEL), lambda s: (down_i(s), 0))],
        out_specs=pl.BlockSpec((1, TF, D_MODEL), lambda s: (*tile_of(s), 0)),
        out_shape=jax.ShapeDtypeStruct((B, S, D_MODEL), F32),
        scratch_shapes=[pltpu.VMEM((TF, D_FF), BF16), pltpu.VMEM((D_MODEL, D_MODEL), BF16),
                        pltpu.VMEM((D_MODEL, 2 * D_FF), BF16), pltpu.VMEM((D_FF, D_MODEL), BF16)],
        compiler_params=pltpu.CompilerParams(dimension_semantics=("arbitrary",),
                                             vmem_limit_bytes=VMEM_LIMIT),
        name="out_ffn",
    )(attn, attn, attn, cb, cb, cb, mo, mo, mo, x, x, x, w_out, g, w_up, dw_w, dw_b, w_down)


def _rope_tables():
    inv_freq = ROPE_THETA ** (-jnp.arange(0, ROT_DIM, 2, dtype=F32) / ROT_DIM)
    invf = jnp.broadcast_to(inv_freq[:, None], (SUBLANES, LANES))
    half = ROT_DIM // 2
    lane = np.arange(LANES) % HEAD_DIM
    lanes = np.zeros((SUBLANES, LANES), np.float32)
    lanes[0] = lane >= ROT_DIM
    lanes[1] = lane < half
    expand = np.zeros((LANES, 2 * LANES), np.float32)
    for l in range(LANES):
        if lane[l] < ROT_DIM:
            j = lane[l] % half
            expand[[j, half + j], l] = 1.0
            expand[[2 * half + j, 3 * half + j], LANES + l] = -1.0 if lane[l] < half else 1.0
    return invf, jnp.asarray(lanes), jnp.asarray(expand, BF16)


def _group_sum_matrix():
    idx = np.arange(MXU_DIM) // HEAD_DIM
    return jnp.asarray(idx[:, None] == idx[None, :], BF16)


def _plane_perm(n):
    out = np.arange(n)
    src = (out % (n // PLANES)) * PLANES + out // (n // PLANES)
    return np.asarray(src[:, None] == np.arange(n)[None, :], np.float32)


def kernel(x, mem, positions, mix_norm_g, mem_norm_g, w_in, w_mem_kv, q_norm_g, k_norm_g, mq_norm_g, mk_norm_g, conv_dw_w, conv_dw_b, conv_ln_g, conv_ln_b, w_out, ffn_norm_g, w_up, ffn_dw_w, ffn_dw_b, w_down):
    B, S, _ = x.shape
    depth = w_in.shape[0]
    pos_row = positions.reshape(B, 1, S)
    invf, lane_tab, expand = _rope_tables()
    gsum = _group_sum_matrix()
    unperm = jnp.asarray(_plane_perm(TL).T, BF16)
    band = _band_tables()
    row = lambda a: a.reshape(1, -1)
    h = x
    for l in range(depth):
        kmt, vm = _mem_kv(mem, row(mem_norm_g[l]), w_mem_kv[l],
                          row(jnp.tile(mk_norm_g[l], MEM_HEADS)), gsum)
        q16, k16, v16, kn, vn, cb, mo = _in_proj(
            h, pos_row, row(mix_norm_g[l]), w_in[l],
            row(jnp.tile(q_norm_g[l], ATTN_HEADS)), row(jnp.tile(k_norm_g[l], ATTN_HEADS)),
            row(jnp.tile(mq_norm_g[l], MEM_HEADS)), gsum, invf, lane_tab, expand,
            conv_dw_w[l], row(conv_dw_b[l]), row(conv_ln_g[l]), row(conv_ln_b[l]), kmt, vm)
        attn = _dil_attn(q16, k16, v16, kn, vn, band, unperm)
        h = _out_ffn(attn, cb, mo, h, w_out[l], row(ffn_norm_g[l]), w_up[l], ffn_dw_w[l],
                     row(ffn_dw_b[l]), w_down[l])
    return h
```

```python
import functools

import numpy as np
import jax
import jax.numpy as jnp
from jax import lax
from jax.experimental import pallas as pl
from jax.experimental.pallas import tpu as pltpu

F32 = jnp.float32
BF16 = jnp.bfloat16

D_MODEL = 1024
HEAD_DIM = 64
ATTN_HEADS = 8
ATTN_W = ATTN_HEADS * HEAD_DIM
CONV_W = 256
MEM_HEADS = 4
MEM_W = MEM_HEADS * HEAD_DIM
N_MEM = 256
PLANES = 16
MID_DIL = 4
BAND_HALF = 64
ROPE_THETA = 500000.0
ROT_DIM = HEAD_DIM // 4
CONV_K = 31
FFN_CONV_K = 3
D_FF = 2816
NORM_EPS = 1e-6
NEG_INF = -1e30
SM_SCALE = HEAD_DIM ** -0.5

LANES = 128
SUBLANES = 8
MXU_DIM = 256
BF16_ROWS = 16
VMEM_LIMIT = 56 * 1024 * 1024

TM = 512
TF = 512
TL = 128
TK = TL + 2 * BAND_HALF
SUPER = PLANES * TL
FF_CH = MXU_DIM
HALO = BF16_ROWS


def _cparams(n_axes):
    return pltpu.CompilerParams(dimension_semantics=("parallel",) * n_axes,
                                vmem_limit_bytes=VMEM_LIMIT)


def _const_spec(shape):
    return pl.BlockSpec(shape, lambda *_: (0,) * len(shape), pipeline_mode=pl.Buffered(1))


def _rms_rows(x, g):
    r = lax.rsqrt(jnp.mean(x * x, axis=-1, keepdims=True) + NORM_EPS)
    return x * r * g


def _head_sumsq(t, gsum):
    return [jnp.dot((tc * tc).astype(BF16), gsum, preferred_element_type=F32)
            for tc in (t[:, c * MXU_DIM:(c + 1) * MXU_DIM] for c in range(t.shape[1] // MXU_DIM))]


def _head_scale(t, sumsq, gain):
    outs = []
    for c, ssum in enumerate(sumsq):
        sl = slice(c * MXU_DIM, (c + 1) * MXU_DIM)
        outs.append(t[:, sl] * lax.rsqrt(ssum * (1.0 / HEAD_DIM) + NORM_EPS) * gain[:, sl])
    return outs


def _head_norm(t, gain, gsum):
    return _head_scale(t, _head_sumsq(t, gsum), gain)


def _sigmoid(x):
    return 1.0 / (1.0 + jnp.exp(-x))


IN_COLS = 3 * ATTN_W + 2 * CONV_W + MEM_W
IN_CAST = MXU_DIM
N_CAST_IN = IN_COLS // IN_CAST


def _in_proj_kernel(x_ref, xp_ref, xx_ref, pos_ref, g_ref, w32_ref, gq_ref, gk_ref, gm_ref,
                    gsum_ref, invf_ref, lane_ref, expand_ref, perm_ref,
                    cw_ref, cbias_ref, cg_ref, cbeta_ref, kmt_ref, vm_ref,
                    q16_ref, k16_ref, v16_ref, kn_ref, vn_ref, cb_ref, mo_ref,
                    w_ref, *, n_tiles):
    step = pl.program_id(0)

    for c in range(N_CAST_IN):
        @pl.when(step == c)
        def _(c=c):
            w_ref[:, c * IN_CAST:(c + 1) * IN_CAST] = w32_ref[...].astype(BF16)

    @pl.when(step >= N_CAST_IN)
    def _():
        _in_proj_tile(x_ref, xp_ref, xx_ref, pos_ref, g_ref, w_ref, gq_ref, gk_ref, gm_ref,
                      gsum_ref, invf_ref, lane_ref, expand_ref, perm_ref,
                      cw_ref, cbias_ref, cg_ref, cbeta_ref, kmt_ref, vm_ref,
                      q16_ref, k16_ref, v16_ref, kn_ref, vn_ref, cb_ref, mo_ref,
                      lax.rem(step - N_CAST_IN, n_tiles), n_tiles)


def _in_proj_tile(x_ref, xp_ref, xx_ref, pos_ref, g_ref, w_ref, gq_ref, gk_ref, gm_ref,
                  gsum_ref, invf_ref, lane_ref, expand_ref, perm_ref,
                  cw_ref, cbias_ref, cg_ref, cbeta_ref, kmt_ref, vm_ref,
                  q16_ref, k16_ref, v16_ref, kn_ref, vn_ref, cb_ref, mo_ref,
                  t, nt):
    has_prev, has_next = t > 0, t < nt - 1
    hn = _rms_rows(x_ref[0], g_ref[...]).astype(BF16)

    hn_halo = _rms_rows(jnp.concatenate([xp_ref[0], xx_ref[0]], axis=0), g_ref[...]).astype(BF16)
    hn_ext = jnp.concatenate([hn_halo[:HALO], hn, hn_halo[HALO:]], axis=0)
    cols = lambda lo, n: w_ref[:, lo:lo + n]
    q = jnp.dot(hn, cols(0, ATTN_W), preferred_element_type=F32)
    k = jnp.dot(hn, cols(ATTN_W, ATTN_W), preferred_element_type=F32)
    v = jnp.dot(hn, cols(2 * ATTN_W, ATTN_W), preferred_element_type=F32)
    glu = jnp.dot(hn_ext, cols(3 * ATTN_W, 2 * CONV_W), preferred_element_type=F32)
    qm = jnp.dot(hn, cols(3 * ATTN_W + 2 * CONV_W, MEM_W), preferred_element_type=F32)

    ang = invf_ref[:, 0:1] * pos_ref[0].astype(F32)

    def hi_lo(v):
        hi = v.astype(BF16).astype(F32)
        return [hi, (v - hi).astype(BF16).astype(F32)]

    tab = jnp.concatenate(hi_lo(jnp.cos(ang)) + hi_lo(jnp.sin(ang))
                          + [jnp.zeros((LANES - 4 * SUBLANES, TM), F32)], axis=0)
    cs = jnp.dot(tab.T.astype(BF16), expand_ref[...], preferred_element_type=F32)
    cosv = cs[:, :LANES] + lane_ref[0:1, :]
    sinv = cs[:, LANES:]
    first_half = lane_ref[1:2, :] > 0.5

    def rotary(chunks, scale):
        out = []
        for t in chunks:
            for s in range(MXU_DIM // LANES):
                xc = t[:, s * LANES:(s + 1) * LANES]
                partner = jnp.where(first_half, pltpu.roll(xc, LANES - ROT_DIM // 2, 1),
                                    pltpu.roll(xc, ROT_DIM // 2, 1))
                out.append((xc * cosv + partner * sinv) * scale)
        return out

    gsum = gsum_ref[...]
    q_ss, k_ss, qm_ss = _head_sumsq(q, gsum), _head_sumsq(k, gsum), _head_sumsq(qm, gsum)
    q_chunks = rotary(_head_scale(q, q_ss, gq_ref[...]), SM_SCALE)
    k_chunks = rotary(_head_scale(k, k_ss, gk_ref[...]), 1.0)
    qb = jnp.concatenate(q_chunks, axis=1).astype(BF16)
    kb = jnp.concatenate(k_chunks, axis=1).astype(BF16)
    vb = v.astype(BF16)
    kn_ref[0] = kb
    vn_ref[0] = vb
    qkv = jnp.dot(perm_ref[...], jnp.concatenate([qb, kb, vb], axis=1),
                  preferred_element_type=F32)
    rows = TM // PLANES
    for r in range(PLANES):
        blk = qkv[r * rows:(r + 1) * rows]
        q16_ref[0, r] = blk[:, 0:ATTN_W]
        k16_ref[0, r] = blk[:, ATTN_W:2 * ATTN_W].astype(BF16)
        v16_ref[0, r] = blk[:, 2 * ATTN_W:].astype(BF16)

    (qmn,) = _head_scale(qm, qm_ss, gm_ref[...])
    mo_ref[0] = _mem_attn((qmn * SM_SCALE).astype(BF16), kmt_ref, vm_ref)

    rid = lax.broadcasted_iota(jnp.int32, (TM + 2 * HALO, 1), 0)
    in_seq = ((rid >= HALO) | has_prev) & ((rid < TM + HALO) | has_next)
    cbuf = jnp.where(in_seq, glu[:, :CONV_W] * _sigmoid(glu[:, CONV_W:]), 0.0)
    cb_ref[0] = _conv_module(cbuf, TM, cw_ref, cbias_ref, cg_ref, cbeta_ref).astype(BF16)


def _in_proj(x, pos_row, g, w_in, gq, gk, gm, gsum, invf, lane_tab, expand, perm,
             conv_w, conv_b, conv_g, conv_beta, kmt, vm):
    B, S, _ = x.shape
    nt = S // TM
    per = TM // HALO

    def tile_of(step):
        i = jnp.maximum(step - N_CAST_IN, 0)
        return i // nt, i % nt

    def at_tile(fn):
        return lambda step: fn(*tile_of(step))

    tok = lambda w: pl.BlockSpec((1, TM, w), at_tile(lambda b, t: (b, t, 0)))
    xprev = pl.BlockSpec((1, HALO, D_MODEL),
                         at_tile(lambda b, t: (b, jnp.maximum(t * per - 1, 0), 0)))
    xnext = pl.BlockSpec((1, HALO, D_MODEL),
                         at_tile(lambda b, t: (b, jnp.minimum((t + 1) * per, nt * per - 1), 0)))
    plane = pl.BlockSpec((1, PLANES, TM // PLANES, ATTN_W), at_tile(lambda b, t: (b, 0, t, 0)))
    plane_shape = (B, PLANES, S // PLANES, ATTN_W)
    per_batch = lambda r, w: pl.BlockSpec((1, r, w), at_tile(lambda b, t: (b, 0, 0)))
    return pl.pallas_call(
        functools.partial(_in_proj_kernel, n_tiles=nt),
        grid=(N_CAST_IN + B * nt,),
        in_specs=[tok(D_MODEL), xprev, xnext,
                  pl.BlockSpec((1, 1, TM), at_tile(lambda b, t: (b, 0, t))),
                  _const_spec((1, D_MODEL)),
                  pl.BlockSpec((D_MODEL, IN_CAST),
                               lambda step: (0, jnp.minimum(step, N_CAST_IN - 1))),
                  _const_spec((1, ATTN_W)), _const_spec((1, ATTN_W)), _const_spec((1, MEM_W)),
                  _const_spec((MXU_DIM, MXU_DIM)), _const_spec((SUBLANES, LANES)),
                  _const_spec((SUBLANES, LANES)), _const_spec((LANES, 2 * LANES)),
                  _const_spec((TM, TM)),
                  _const_spec((CONV_K, CONV_W)), _const_spec((1, CONV_W)),
                  _const_spec((1, CONV_W)), _const_spec((1, CONV_W)),
                  per_batch(MEM_W, N_MEM), per_batch(N_MEM, MEM_W)],
        out_specs=[plane, plane, plane, tok(ATTN_W), tok(ATTN_W), tok(CONV_W), tok(MEM_W)],
        out_shape=[jax.ShapeDtypeStruct(plane_shape, F32),
                   jax.ShapeDtypeStruct(plane_shape, BF16),
                   jax.ShapeDtypeStruct(plane_shape, BF16),
                   jax.ShapeDtypeStruct((B, S, ATTN_W), BF16),
                   jax.ShapeDtypeStruct((B, S, ATTN_W), BF16),
                   jax.ShapeDtypeStruct((B, S, CONV_W), BF16),
                   jax.ShapeDtypeStruct((B, S, MEM_W), BF16)],
        scratch_shapes=[pltpu.VMEM((D_MODEL, IN_COLS), BF16)],
        compiler_params=pltpu.CompilerParams(dimension_semantics=("arbitrary",),
                                             vmem_limit_bytes=VMEM_LIMIT),
        name="in_proj",
    )(x, x, x, pos_row, g, w_in, gq, gk, gm, gsum, invf, lane_tab, expand, perm,
      conv_w, conv_b, conv_g, conv_beta, kmt, vm)


def _mem_kv_kernel(mem_ref, g_ref, w_ref, gk_ref, gsum_ref, kmt_ref, vm_ref):
    mn = _rms_rows(mem_ref[0], g_ref[...]).astype(BF16)
    kv = jnp.dot(mn, w_ref[...].astype(BF16), preferred_element_type=F32)
    (km,) = _head_norm(kv[:, :MEM_W], gk_ref[...], gsum_ref[...])
    kmt_ref[0] = km.T.astype(BF16)
    vm_ref[0] = kv[:, MEM_W:].astype(BF16)


def _mem_kv(mem, g, w, gk, gsum):
    B = mem.shape[0]
    return pl.pallas_call(
        _mem_kv_kernel,
        grid=(B,),
        in_specs=[pl.BlockSpec((1, N_MEM, D_MODEL), lambda b: (b, 0, 0)),
                  _const_spec((1, D_MODEL)), _const_spec((D_MODEL, 2 * MEM_W)),
                  _const_spec((1, MEM_W)), _const_spec((MXU_DIM, MXU_DIM))],
        out_specs=[pl.BlockSpec((1, MEM_W, N_MEM), lambda b: (b, 0, 0)),
                   pl.BlockSpec((1, N_MEM, MEM_W), lambda b: (b, 0, 0))],
        out_shape=[jax.ShapeDtypeStruct((B, MEM_W, N_MEM), BF16),
                   jax.ShapeDtypeStruct((B, N_MEM, MEM_W), BF16)],
        compiler_params=_cparams(1),
        name="mem_kv",
    )(mem, g, w, gk, gsum)


def _head_masks():
    lane = lax.broadcasted_iota(jnp.int32, (1, LANES), 1)
    low = lane < HEAD_DIM
    return low, (jnp.where(low, 1.0, 0.0).astype(BF16), jnp.where(low, 0.0, 1.0).astype(BF16))


def _mem_attn(qm, kmt_ref, vm_ref):
    low, hmask = _head_masks()
    out = []
    for c in range(MEM_W // LANES):
        sl = slice(c * LANES, (c + 1) * LANES)
        qc = qm[:, sl]
        halves = []
        for e in range(2):
            s = jnp.dot(qc * hmask[e], kmt_ref[0, sl, :], preferred_element_type=F32)
            m = jnp.max(s, axis=-1, keepdims=True)
            p = jnp.exp(s - m)
            l = jnp.sum(p, axis=-1, keepdims=True)
            o = jnp.dot(p.astype(BF16), vm_ref[0, :, sl], preferred_element_type=F32)
            halves.append(o / l)
        out.append(jnp.where(low, halves[0], halves[1]).astype(BF16))
    return jnp.concatenate(out, axis=1)


HW = ATTN_W // 2
MID_ROWS = TL // MID_DIL
MID_KEYS = TK // MID_DIL
MID_LEAD = (MID_KEYS - MID_ROWS) // 2
ONE_ROWS = TL // PLANES
TILE_GROUP = 8


def _band_tables():
    rho = np.arange(TL)[:, None]
    kap = np.arange(TK)[None, :]
    d16 = kap - BAND_HALF - rho
    j, lq = rho // MID_ROWS, rho % MID_ROWS
    jk, lk = kap // MID_KEYS, kap % MID_KEYS
    d4 = MID_DIL * (lk - MID_LEAD - lq) + (jk - j)
    r, l1 = rho // ONE_ROWS, rho % ONE_ROWS
    d1 = kap - BAND_HALF - PLANES * l1 - r
    tabs = [np.where(np.abs(d) <= BAND_HALF, 0.0, NEG_INF) for d in (d16, d4, d1)]
    return jnp.asarray(np.stack(tabs), F32)


ALL_CHUNKS = tuple(range(HW // LANES))


def _scores(q, kw, bias2, hmask, chunks=ALL_CHUNKS):
    out = []
    for c in chunks:
        sl = slice(c * LANES, (c + 1) * LANES)
        qc = q[:, sl]
        qs = jnp.concatenate([qc * hmask[0], qc * hmask[1]], axis=0)
        out.append(lax.dot_general(qs, kw[:, sl], (((1,), (1,)), ((), ())),
                                   preferred_element_type=F32) + bias2)
    return out


def _softmax_pv(scores, vw, low, hmask, m_old=None, chunks=ALL_CHUNKS):
    ones = jnp.ones((TK, LANES), BF16)
    res = []
    for c, s in zip(chunks, scores):
        m = jnp.max(s, axis=-1, keepdims=True)
        if m_old is None:
            shift = m
        else:
            m = jnp.maximum(m, m_old[c])
            shift = jnp.concatenate([m] * (TK // LANES), axis=1)
        p = jnp.exp((s - shift).astype(BF16))
        v_aug = jnp.concatenate([vw[:, c * LANES:(c + 1) * LANES], ones], axis=1)
        r = jnp.dot(p, v_aug, preferred_element_type=F32)
        res.append((m, jnp.where(low, r[:TL, LANES:], r[TL:, LANES:]),
                    jnp.where(low, r[:TL, :LANES], r[TL:, :LANES])))
    return res


def _attend(q, kw, vw, bias2, low, hmask, m_old=None):
    res = []
    for c in ALL_CHUNKS:
        res += _softmax_pv(_scores(q, kw, bias2, hmask, (c,)), vw, low, hmask, m_old, (c,))
    return res


def _merge(new, m_old, l_old, acc_old, low):
    m, l_n, acc_n = new
    b = jnp.exp(m_old - m)
    b = jnp.where(low, b[:TL], b[TL:])
    return m, l_n + b * l_old, acc_n + b * acc_old


def _dil_attn_kernel(q_ref, kc_ref, kp_ref, kx_ref, vc_ref, vp_ref, vx_ref,
                     knc_ref, knp_ref, knx_ref, vnc_ref, vnp_ref, vnx_ref, band_ref, unperm_ref,
                     o_ref, acc_ref, m_ref, l_ref, kne_ref, vne_ref):
    st = pl.program_id(1)
    first, last = st == 0, st == pl.num_programs(1) - 1
    low, hmask = _head_masks()
    col = lax.broadcasted_iota(jnp.int32, (1, TK), 1)
    chunk = lambda c: slice(c * LANES, (c + 1) * LANES)

    def stacked_bias(band, col_idx=None, lo=0, hi=TK):
        if col_idx is not None:
            band = band + jnp.where((col_idx < lo) | (col_idx >= hi), NEG_INF, 0.0)
        return jnp.concatenate([band, band], axis=0)

    bias16 = stacked_bias(band_ref[0], col, jnp.where(first, BAND_HALF, 0),
                          jnp.where(last, TK - BAND_HALF, TK))

    def pipelined(tiles, score_fn, finish_fn):
        s_next = score_fn(tiles[0])
        for i, t in enumerate(tiles):
            s = s_next
            if i + 1 < len(tiles):
                s_next = score_fn(tiles[i + 1])
            finish_fn(t, s)

    def body16(g, carry):
        def score(r):
            kw = jnp.concatenate([kp_ref[0, r], kc_ref[0, r], kx_ref[0, r]], axis=0)
            return _scores(q_ref[0, r].astype(BF16), kw, bias16, hmask)

        def finish(r, s):
            vw = jnp.concatenate([vp_ref[0, r], vc_ref[0, r], vx_ref[0, r]], axis=0)
            for c, (m, l, acc) in enumerate(_softmax_pv(s, vw, low, hmask)):
                acc_ref[r, :, chunk(c)] = acc
                l_ref[c, r] = l
                for e in range(2):
                    m_ref[c, e, r] = jnp.broadcast_to(m[e * TL:(e + 1) * TL], (TL, LANES))

        pipelined([g * TILE_GROUP + i for i in range(TILE_GROUP)], score, finish)
        return carry

    lax.fori_loop(0, PLANES // TILE_GROUP, body16, 0)

    def mid_window(cur, prev, nxt, plane, lb):
        lo = lb * MID_ROWS - MID_LEAD
        if lo < 0:
            return jnp.concatenate([prev[0, plane, TL // 2 + lo:TL // 2],
                                    cur[0, plane, 0:lo + MID_KEYS]], axis=0)
        if lo + MID_KEYS > TL:
            return jnp.concatenate([cur[0, plane, lo:TL],
                                    nxt[0, plane, 0:lo + MID_KEYS - TL]], axis=0)
        return cur[0, plane, lo:lo + MID_KEYS]

    def body4(r4, carry):
        planes = [r4 + MID_DIL * j for j in range(MID_DIL)]
        block_rows = lambda lb: slice(lb * MID_ROWS, (lb + 1) * MID_ROWS)

        def score(lb):
            band = band_ref[1]
            if lb == 0:
                bias = stacked_bias(band, col % MID_KEYS, jnp.where(first, MID_LEAD, 0), MID_KEYS)
            elif lb == TL // MID_ROWS - 1:
                bias = stacked_bias(band, col % MID_KEYS, 0,
                                    jnp.where(last, MID_KEYS - MID_LEAD, MID_KEYS))
            else:
                bias = stacked_bias(band)
            q = jnp.concatenate([q_ref[0, p, block_rows(lb)] for p in planes],
                                axis=0).astype(BF16)
            kw = jnp.concatenate([mid_window(kc_ref, kp_ref, kx_ref, p, lb) for p in planes],
                                 axis=0)
            return _scores(q, kw, bias, hmask)

        def finish(lb, s):
            rows = block_rows(lb)
            stacked = lambda ref, c: jnp.concatenate(
                [ref[c, e, p, rows] for e in range(2) for p in planes], axis=0)
            vw = jnp.concatenate([mid_window(vc_ref, vp_ref, vx_ref, p, lb) for p in planes],
                                 axis=0)
            m_old = [stacked(m_ref, c) for c in range(HW // LANES)]
            for c, new in enumerate(_softmax_pv(s, vw, low, hmask, m_old)):
                acc_old = jnp.concatenate([acc_ref[p, rows, chunk(c)] for p in planes], axis=0)
                l_old = jnp.concatenate([l_ref[c, p, rows] for p in planes], axis=0)
                m_m, l_m, acc_m = _merge(new, m_old[c], l_old, acc_old, low)
                for j, p in enumerate(planes):
                    piece = slice(j * MID_ROWS, (j + 1) * MID_ROWS)
                    acc_ref[p, rows, chunk(c)] = acc_m[piece]
                    l_ref[c, p, rows] = l_m[piece]
                    for e in range(2):
                        head_piece = slice(e * TL + j * MID_ROWS, e * TL + (j + 1) * MID_ROWS)
                        m_ref[c, e, p, rows] = m_m[head_piece]

        pipelined(list(range(TL // MID_ROWS)), score, finish)
        return carry

    lax.fori_loop(0, MID_DIL, body4, 0)

    n_tiles = SUPER // TL
    for ext_ref, prev, cur, nxt in ((kne_ref, knp_ref, knc_ref, knx_ref),
                                    (vne_ref, vnp_ref, vnc_ref, vnx_ref)):
        ext_ref[0:BAND_HALF] = prev[0]
        ext_ref[BAND_HALF:BAND_HALF + SUPER] = cur[0]
        ext_ref[BAND_HALF + SUPER:] = nxt[0]
    band1 = stacked_bias(band_ref[2])

    def body1(g, carry):
        tiles = [g * TILE_GROUP + i for i in range(TILE_GROUP)]
        tile_rows = lambda t: pl.ds(pl.multiple_of(t * ONE_ROWS, ONE_ROWS), ONE_ROWS)
        tile_keys = lambda t: pl.ds(pl.multiple_of(t * TL, TL), TK)

        def tile_scores(t):
            lo = jnp.where(first & (t == 0), BAND_HALF, 0)
            hi = jnp.where(last & (t == n_tiles - 1), TK - BAND_HALF, TK)
            bias = band1 + jnp.where((col < lo) | (col >= hi), NEG_INF, 0.0)
            q = q_ref[0, :, tile_rows(t), :].reshape(TL, HW).astype(BF16)
            return _scores(q, kne_ref[tile_keys(t), :], bias, hmask)

        def emit(t, merged):
            tok = jnp.dot(unperm_ref[...], merged, preferred_element_type=F32)
            o_ref[0, pl.ds(pl.multiple_of(t * TL, TL), TL), :] = tok.astype(BF16)

        pending = []

        def finish(t, s):
            rows = tile_rows(t)
            stacked = lambda ref, c: jnp.concatenate(
                [ref[c, e, :, rows, :].reshape(TL, LANES) for e in range(2)], axis=0)
            m_old = [stacked(m_ref, c) for c in range(HW // LANES)]
            outs = []
            for c, new in enumerate(_softmax_pv(s, vne_ref[tile_keys(t), :], low, hmask, m_old)):
                acc_old = acc_ref[:, rows, chunk(c)].reshape(TL, LANES)
                l_old = l_ref[c, :, rows, :].reshape(TL, LANES)
                _, l_m, acc_m = _merge(new, m_old[c], l_old, acc_old, low)
                outs.append((acc_m / l_m).astype(BF16))
            if pending:
                emit(*pending.pop())
            pending.append((t, jnp.concatenate(outs, axis=1)))

        pipelined(tiles, tile_scores, finish)
        emit(*pending.pop())
        return carry

    lax.fori_loop(0, n_tiles // TILE_GROUP, body1, 0)


def _dil_attn(q16, k16, v16, kn, vn, band, unperm):
    B, S, _ = kn.shape
    n_half = S // PLANES // BAND_HALF
    n_tok_half = S // BAND_HALF
    per = TL // BAND_HALF
    per_tok = SUPER // BAND_HALF
    pcur = pl.BlockSpec((1, PLANES, TL, HW), lambda b, s, hh: (b, 0, s, hh))
    pprev = pl.BlockSpec((1, PLANES, BAND_HALF, HW),
                         lambda b, s, hh: (b, 0, jnp.maximum(s * per - 1, 0), hh))
    pnext = pl.BlockSpec((1, PLANES, BAND_HALF, HW),
                         lambda b, s, hh: (b, 0, jnp.minimum((s + 1) * per, n_half - 1), hh))
    tcur = pl.BlockSpec((1, SUPER, HW), lambda b, s, hh: (b, s, hh))
    tprev = pl.BlockSpec((1, BAND_HALF, HW),
                         lambda b, s, hh: (b, jnp.maximum(s * per_tok - 1, 0), hh))
    tnext = pl.BlockSpec((1, BAND_HALF, HW),
                         lambda b, s, hh: (b, jnp.minimum((s + 1) * per_tok, n_tok_half - 1), hh))
    return pl.pallas_call(
        _dil_attn_kernel,
        grid=(B, S // SUPER, ATTN_W // HW),
        in_specs=[pcur, pcur, pprev, pnext, pcur, pprev, pnext,
                  tcur, tprev, tnext, tcur, tprev, tnext,
                  _const_spec((3, TL, TK)), _const_spec((TL, TL))],
        out_specs=tcur,
        out_shape=jax.ShapeDtypeStruct((B, S, ATTN_W), BF16),
        scratch_shapes=[pltpu.VMEM((PLANES, TL, HW), F32),
                        pltpu.VMEM((HW // LANES, 2, PLANES, TL, LANES), F32),
                        pltpu.VMEM((HW // LANES, PLANES, TL, LANES), F32),
                        pltpu.VMEM((SUPER + 2 * BAND_HALF, HW), BF16),
                        pltpu.VMEM((SUPER + 2 * BAND_HALF, HW), BF16)],
        compiler_params=_cparams(3),
        name="dil_attn",
    )(q16, k16, k16, k16, v16, v16, v16, kn, kn, kn, vn, vn, vn, band, unperm)


def _halo_specs(width, n_tiles, rows, tile=TM):
    per = tile // rows
    last = n_tiles * per - 1
    prev = pl.BlockSpec((1, rows, width), lambda b, t: (b, jnp.maximum(t * per - 1, 0), 0))
    nxt = pl.BlockSpec((1, rows, width), lambda b, t: (b, jnp.minimum((t + 1) * per, last), 0))
    return prev, nxt


def _conv_module(buf, n_out, w_ref, b_ref, g_ref, beta_ref):
    rows = n_out + 2 * HALO
    acc = jnp.zeros((n_out, CONV_W), F32) + b_ref[...]
    base = HALO - CONV_K // 2
    for shift in range(SUBLANES):
        rolled = buf if shift == 0 else pltpu.roll(buf, rows - shift, 0)
        for tap in range(CONV_K):
            off = base + tap
            if off % SUBLANES == shift:
                lo = off - shift
                acc = acc + w_ref[tap:tap + 1, :] * rolled[lo:lo + n_out]
    mu = jnp.mean(acc, axis=-1, keepdims=True)
    d = acc - mu
    var = jnp.mean(d * d, axis=-1, keepdims=True)
    z = d * lax.rsqrt(var + NORM_EPS) * g_ref[...] + beta_ref[...]
    return z * _sigmoid(z)


N_CAST = 11
UP_CAST = 2 * D_FF // N_CAST
DOWN_CAST = D_FF // N_CAST
OUT_CAST = LANES


def _out_ffn_kernel(a_ref, ap_ref, ax_ref, c_ref, cp_ref, cx_ref, m_ref, mp_ref, mx_ref,
                    x_ref, xp_ref, xx_ref, wo32_ref, g_ref, wu32_ref, dw_ref, db_ref, wd32_ref,
                    o_ref, gate_ref, wo_ref, wu_ref, wd_ref, *, n_tiles):
    step = pl.program_id(0)

    for c in range(N_CAST):
        @pl.when(step == c)
        def _(c=c):
            for half in range(UP_CAST // FF_CH):
                src = c * UP_CAST + half * FF_CH
                j, is_up = (src // FF_CH, 0) if src < D_FF else ((src - D_FF) // FF_CH, 1)
                dst = (2 * j + is_up) * FF_CH
                wu_ref[:, dst:dst + FF_CH] = wu32_ref[:, half * FF_CH:(half + 1) * FF_CH].astype(BF16)
            wd_ref[c * DOWN_CAST:(c + 1) * DOWN_CAST, :] = wd32_ref[...].astype(BF16)
            if (c + 1) * OUT_CAST <= D_MODEL:
                wo_ref[c * OUT_CAST:(c + 1) * OUT_CAST, :] = wo32_ref[...].astype(BF16)

    @pl.when(step >= N_CAST)
    def _():
        _out_ffn_tile(a_ref, ap_ref, ax_ref, c_ref, cp_ref, cx_ref, m_ref, mp_ref, mx_ref,
                      x_ref, xp_ref, xx_ref, wo_ref, g_ref, wu_ref, dw_ref, db_ref, wd_ref,
                      o_ref, gate_ref, lax.rem(step - N_CAST, n_tiles), n_tiles)


def _out_ffn_tile(a_ref, ap_ref, ax_ref, c_ref, cp_ref, cx_ref, m_ref, mp_ref, mx_ref,
                  x_ref, xp_ref, xx_ref, wo_ref, g_ref, wu_ref, dw_ref, db_ref, wd_ref,
                  o_ref, gate_ref, t, nt):
    has_next, has_prev = t < nt - 1, t > 0
    rows = TF + HALO

    def ext_rows(cur, nxt, prev):
        halo = jnp.concatenate([nxt[0, :SUBLANES].astype(F32), prev[0, SUBLANES:].astype(F32)],
                               axis=0)
        return jnp.concatenate([cur[0], halo.astype(cur.dtype)], axis=0)

    rid = lax.broadcasted_iota(jnp.int32, (rows, 1), 0)
    in_seq = ((rid < TF) | ((rid < TF + SUBLANES) & has_next)
              | ((rid >= TF + SUBLANES) & has_prev))
    mixed = jnp.concatenate([ext_rows(a_ref, ax_ref, ap_ref), ext_rows(c_ref, cx_ref, cp_ref),
                             ext_rows(m_ref, mx_ref, mp_ref)], axis=-1)
    mixed = jnp.where(in_seq, mixed, jnp.zeros_like(mixed))
    h = (jnp.where(in_seq, ext_rows(x_ref, xx_ref, xp_ref), 0.0)
         + jnp.dot(mixed, wo_ref[...], preferred_element_type=F32))
    ext = _rms_rows(h, g_ref[...]).astype(BF16)

    def conv3(f, lo):
        w = dw_ref[:, lo:lo + FF_CH]
        y = (w[0:1] * pltpu.roll(f, 1, 0) + w[1:2] * f + w[2:3] * pltpu.roll(f, rows - 1, 0))
        return y[:TF] + db_ref[:, lo:lo + FF_CH]

    n_chunks = D_FF // FF_CH
    out = h[:TF]
    for j in range(n_chunks):
        lo_g, lo_u = j * FF_CH, D_FF + j * FF_CH
        f = jnp.dot(ext, wu_ref[:, 2 * lo_g:2 * lo_g + 2 * FF_CH], preferred_element_type=F32)
        if j == n_chunks - 1:
            out = out + jnp.dot(gate_ref[:, :lo_g], wd_ref[:lo_g, :], preferred_element_type=F32)
        fg, fu = conv3(f[:, :FF_CH], lo_g), conv3(f[:, FF_CH:], lo_u)
        gate_ref[:, lo_g:lo_g + FF_CH] = (fg * _sigmoid(fg) * fu).astype(BF16)
    o_ref[0] = out + jnp.dot(gate_ref[:, D_FF - FF_CH:], wd_ref[D_FF - FF_CH:, :],
                             preferred_element_type=F32)


def _out_ffn(attn, cb, mo, x, w_out, g, w_up, dw_w, dw_b, w_down):
    B, S, _ = x.shape
    nt = S // TF
    per = TF // HALO

    def tile_of(step):
        i = jnp.maximum(step - N_CAST, 0)
        return i // nt, i % nt

    def with_halos(width):
        def cur(step):
            b, t = tile_of(step)
            return b, t, 0

        def prev(step):
            b, t = tile_of(step)
            return b, jnp.maximum(t * per - 1, 0), 0

        def nxt(step):
            b, t = tile_of(step)
            return b, jnp.minimum((t + 1) * per, nt * per - 1), 0

        return [pl.BlockSpec((1, TF, width), cur), pl.BlockSpec((1, HALO, width), prev),
                pl.BlockSpec((1, HALO, width), nxt)]

    cast_chunk = lambda limit: (lambda step: jnp.minimum(step, limit - 1))
    out_i, up_i, down_i = cast_chunk(D_MODEL // OUT_CAST), cast_chunk(N_CAST), cast_chunk(N_CAST)
    return pl.pallas_call(
        functools.partial(_out_ffn_kernel, n_tiles=nt),
        grid=(N_CAST + B * nt,),
        in_specs=with_halos(ATTN_W) + with_halos(CONV_W) + with_halos(MEM_W)
        + with_halos(D_MODEL)
        + [pl.BlockSpec((OUT_CAST, D_MODEL), lambda s: (out_i(s), 0)), _const_spec((1, D_MODEL)),
           pl.BlockSpec((D_MODEL, UP_CAST), lambda s: (0, up_i(s))),
           _const_spec((FFN_CONV_K, 2 * D_FF)), _const_spec((1, 2 * D_FF)),
           pl.BlockSpec((DOWN_CAST, D_MODEL), lambda s: (down_i(s), 0))],
        out_specs=pl.BlockSpec((1, TF, D_MODEL), lambda s: (*tile_of(s), 0)),
        out_shape=jax.ShapeDtypeStruct((B, S, D_MODEL), F32),
        scratch_shapes=[pltpu.VMEM((TF, D_FF), BF16), pltpu.VMEM((D_MODEL, D_MODEL), BF16),
                        pltpu.VMEM((D_MODEL, 2 * D_FF), BF16), pltpu.VMEM((D_FF, D_MODEL), BF16)],
        compiler_params=pltpu.CompilerParams(dimension_semantics=("arbitrary",),
                                             vmem_limit_bytes=VMEM_LIMIT),
        name="out_ffn",
    )(attn, attn, attn, cb, cb, cb, mo, mo, mo, x, x, x, w_out, g, w_up, dw_w, dw_b, w_down)


def _rope_tables():
    inv_freq = ROPE_THETA ** (-jnp.arange(0, ROT_DIM, 2, dtype=F32) / ROT_DIM)
    invf = jnp.broadcast_to(inv_freq[:, None], (SUBLANES, LANES))
    half = ROT_DIM // 2
    lane = np.arange(LANES) % HEAD_DIM
    lanes = np.zeros((SUBLANES, LANES), np.float32)
    lanes[0] = lane >= ROT_DIM
    lanes[1] = lane < half
    expand = np.zeros((LANES, 2 * LANES), np.float32)
    for l in range(LANES):
        if lane[l] < ROT_DIM:
            j = lane[l] % half
            expand[[j, half + j], l] = 1.0
            expand[[2 * half + j, 3 * half + j], LANES + l] = -1.0 if lane[l] < half else 1.0
    return invf, jnp.asarray(lanes), jnp.asarray(expand, BF16)


def _group_sum_matrix():
    idx = np.arange(MXU_DIM) // HEAD_DIM
    return jnp.asarray(idx[:, None] == idx[None, :], BF16)


def _plane_perm(n):
    out = np.arange(n)
    src = (out % (n // PLANES)) * PLANES + out // (n // PLANES)
    return np.asarray(src[:, None] == np.arange(n)[None, :], np.float32)


def kernel(x, mem, positions, mix_norm_g, mem_norm_g, w_in, w_mem_kv, q_norm_g, k_norm_g, mq_norm_g, mk_norm_g, conv_dw_w, conv_dw_b, conv_ln_g, conv_ln_b, w_out, ffn_norm_g, w_up, ffn_dw_w, ffn_dw_b, w_down):
    B, S, _ = x.shape
    depth = w_in.shape[0]
    pos_row = positions.reshape(B, 1, S)
    invf, lane_tab, expand = _rope_tables()
    gsum = _group_sum_matrix()
    perm = jnp.asarray(_plane_perm(TM), BF16)
    unperm = jnp.asarray(_plane_perm(TL).T, BF16)
    band = _band_tables()
    row = lambda a: a.reshape(1, -1)
    h = x
    for l in range(depth):
        kmt, vm = _mem_kv(mem, row(mem_norm_g[l]), w_mem_kv[l],
                          row(jnp.tile(mk_norm_g[l], MEM_HEADS)), gsum)
        q16, k16, v16, kn, vn, cb, mo = _in_proj(
            h, pos_row, row(mix_norm_g[l]), w_in[l],
            row(jnp.tile(q_norm_g[l], ATTN_HEADS)), row(jnp.tile(k_norm_g[l], ATTN_HEADS)),
            row(jnp.tile(mq_norm_g[l], MEM_HEADS)), gsum, invf, lane_tab, expand, perm,
            conv_dw_w[l], row(conv_dw_b[l]), row(conv_ln_g[l]), row(conv_ln_b[l]), kmt, vm)
        attn = _dil_attn(q16, k16, v16, kn, vn, band, unperm)
        h = _out_ffn(attn, cb, mo, h, w_out[l], row(ffn_norm_g[l]), w_up[l], ffn_dw_w[l],
                     row(ffn_dw_b[l]), w_down[l])
    return h
```

```python
import functools

import numpy as np
import jax
import jax.numpy as jnp
from jax import lax
from jax.experimental import pallas as pl
from jax.experimental.pallas import tpu as pltpu

F32 = jnp.float32
BF16 = jnp.bfloat16

D_MODEL = 1024
HEAD_DIM = 64
ATTN_HEADS = 8
ATTN_W = ATTN_HEADS * HEAD_DIM
CONV_W = 256
MEM_HEADS = 4
MEM_W = MEM_HEADS * HEAD_DIM
N_MEM = 256
PLANES = 16
MID_DIL = 4
BAND_HALF = 64
ROPE_THETA = 500000.0
ROT_DIM = HEAD_DIM // 4
CONV_K = 31
FFN_CONV_K = 3
D_FF = 2816
NORM_EPS = 1e-6
NEG_INF = -1e30
SM_SCALE = HEAD_DIM ** -0.5

LANES = 128
SUBLANES = 8
MXU_DIM = 256
BF16_ROWS = 16
VMEM_LIMIT = 56 * 1024 * 1024

TM = 512
TF = 512
TL = 128
TK = TL + 2 * BAND_HALF
SUPER = PLANES * TL
FF_CH = MXU_DIM
HALO = BF16_ROWS


def _cparams(n_axes):
    return pltpu.CompilerParams(dimension_semantics=("parallel",) * n_axes,
                                vmem_limit_bytes=VMEM_LIMIT)


def _const_spec(shape):
    return pl.BlockSpec(shape, lambda *_: (0,) * len(shape), pipeline_mode=pl.Buffered(1))


def _rms_rows(x, g):
    r = lax.rsqrt(jnp.mean(x * x, axis=-1, keepdims=True) + NORM_EPS)
    return x * r * g


def _head_sumsq(t, gsum):
    return [jnp.dot((tc * tc).astype(BF16), gsum, preferred_element_type=F32)
            for tc in (t[:, c * MXU_DIM:(c + 1) * MXU_DIM] for c in range(t.shape[1] // MXU_DIM))]


def _head_scale(t, sumsq, gain):
    outs = []
    for c, ssum in enumerate(sumsq):
        sl = slice(c * MXU_DIM, (c + 1) * MXU_DIM)
        outs.append(t[:, sl] * lax.rsqrt(ssum * (1.0 / HEAD_DIM) + NORM_EPS) * gain[:, sl])
    return outs


def _head_norm(t, gain, gsum):
    return _head_scale(t, _head_sumsq(t, gsum), gain)


def _sigmoid(x):
    return 1.0 / (1.0 + jnp.exp(-x))


IN_COLS = 3 * ATTN_W + 2 * CONV_W + MEM_W
IN_CAST = MXU_DIM
N_CAST_IN = IN_COLS // IN_CAST


def _in_proj_kernel(x_ref, xp_ref, xx_ref, pos_ref, g_ref, w32_ref, gq_ref, gk_ref, gm_ref,
                    gsum_ref, invf_ref, lane_ref, expand_ref, perm_ref,
                    cw_ref, cbias_ref, cg_ref, cbeta_ref, kmt_ref, vm_ref,
                    q16_ref, k16_ref, v16_ref, kn_ref, vn_ref, cb_ref, mo_ref,
                    w_ref, *, n_tiles):
    step = pl.program_id(0)

    for c in range(N_CAST_IN):
        @pl.when(step == c)
        def _(c=c):
            w_ref[:, c * IN_CAST:(c + 1) * IN_CAST] = w32_ref[...].astype(BF16)

    @pl.when(step >= N_CAST_IN)
    def _():
        _in_proj_tile(x_ref, xp_ref, xx_ref, pos_ref, g_ref, w_ref, gq_ref, gk_ref, gm_ref,
                      gsum_ref, invf_ref, lane_ref, expand_ref, perm_ref,
                      cw_ref, cbias_ref, cg_ref, cbeta_ref, kmt_ref, vm_ref,
                      q16_ref, k16_ref, v16_ref, kn_ref, vn_ref, cb_ref, mo_ref,
                      lax.rem(step - N_CAST_IN, n_tiles), n_tiles)


def _in_proj_tile(x_ref, xp_ref, xx_ref, pos_ref, g_ref, w_ref, gq_ref, gk_ref, gm_ref,
                  gsum_ref, invf_ref, lane_ref, expand_ref, perm_ref,
                  cw_ref, cbias_ref, cg_ref, cbeta_ref, kmt_ref, vm_ref,
                  q16_ref, k16_ref, v16_ref, kn_ref, vn_ref, cb_ref, mo_ref,
                  t, nt):
    has_prev, has_next = t > 0, t < nt - 1
    hn = _rms_rows(x_ref[0], g_ref[...]).astype(BF16)

    hn_halo = _rms_rows(jnp.concatenate([xp_ref[0], xx_ref[0]], axis=0), g_ref[...]).astype(BF16)
    hn_ext = jnp.concatenate([hn_halo[:HALO], hn, hn_halo[HALO:]], axis=0)
    cols = lambda lo, n: w_ref[:, lo:lo + n]
    q = jnp.dot(hn, cols(0, ATTN_W), preferred_element_type=F32)
    k = jnp.dot(hn, cols(ATTN_W, ATTN_W), preferred_element_type=F32)
    v = jnp.dot(hn, cols(2 * ATTN_W, ATTN_W), preferred_element_type=F32)
    glu = jnp.dot(hn_ext, cols(3 * ATTN_W, 2 * CONV_W), preferred_element_type=F32)
    qm = jnp.dot(hn, cols(3 * ATTN_W + 2 * CONV_W, MEM_W), preferred_element_type=F32)

    ang = invf_ref[:, 0:1] * pos_ref[0].astype(F32)

    def hi_lo(v):
        hi = v.astype(BF16).astype(F32)
        return [hi, (v - hi).astype(BF16).astype(F32)]

    tab = jnp.concatenate(hi_lo(jnp.cos(ang)) + hi_lo(jnp.sin(ang))
                          + [jnp.zeros((LANES - 4 * SUBLANES, TM), F32)], axis=0)
    cs = jnp.dot(tab.T.astype(BF16), expand_ref[...], preferred_element_type=F32)
    cosv = cs[:, :LANES] + lane_ref[0:1, :]
    sinv = cs[:, LANES:]
    first_half = lane_ref[1:2, :] > 0.5

    def rotary(chunks, scale):
        out = []
        for t in chunks:
            for s in range(MXU_DIM // LANES):
                xc = t[:, s * LANES:(s + 1) * LANES]
                partner = jnp.where(first_half, pltpu.roll(xc, LANES - ROT_DIM // 2, 1),
                                    pltpu.roll(xc, ROT_DIM // 2, 1))
                out.append((xc * cosv + partner * sinv) * scale)
        return out

    gsum = gsum_ref[...]
    q_ss, k_ss, qm_ss = _head_sumsq(q, gsum), _head_sumsq(k, gsum), _head_sumsq(qm, gsum)
    q_chunks = rotary(_head_scale(q, q_ss, gq_ref[...]), SM_SCALE)
    k_chunks = rotary(_head_scale(k, k_ss, gk_ref[...]), 1.0)
    qb = jnp.concatenate(q_chunks, axis=1).astype(BF16)
    kb = jnp.concatenate(k_chunks, axis=1).astype(BF16)
    vb = v.astype(BF16)
    kn_ref[0] = kb
    vn_ref[0] = vb
    qkv = jnp.dot(perm_ref[...], jnp.concatenate([qb, kb, vb], axis=1),
                  preferred_element_type=F32)
    rows = TM // PLANES
    for r in range(PLANES):
        blk = qkv[r * rows:(r + 1) * rows]
        q16_ref[0, r] = blk[:, 0:ATTN_W]
        k16_ref[0, r] = blk[:, ATTN_W:2 * ATTN_W].astype(BF16)
        v16_ref[0, r] = blk[:, 2 * ATTN_W:].astype(BF16)

    (qmn,) = _head_scale(qm, qm_ss, gm_ref[...])
    mo_ref[0] = _mem_attn((qmn * SM_SCALE).astype(BF16), kmt_ref, vm_ref)

    rid = lax.broadcasted_iota(jnp.int32, (TM + 2 * HALO, 1), 0)
    in_seq = ((rid >= HALO) | has_prev) & ((rid < TM + HALO) | has_next)
    cbuf = jnp.where(in_seq, glu[:, :CONV_W] * _sigmoid(glu[:, CONV_W:]), 0.0)
    cb_ref[0] = _conv_module(cbuf, TM, cw_ref, cbias_ref, cg_ref, cbeta_ref).astype(BF16)


def _in_proj(x, pos_row, g, w_in, gq, gk, gm, gsum, invf, lane_tab, expand, perm,
             conv_w, conv_b, conv_g, conv_beta, kmt, vm):
    B, S, _ = x.shape
    nt = S // TM
    per = TM // HALO

    def tile_of(step):
        i = jnp.maximum(step - N_CAST_IN, 0)
        return i // nt, i % nt

    def at_tile(fn):
        return lambda step: fn(*tile_of(step))

    tok = lambda w: pl.BlockSpec((1, TM, w), at_tile(lambda b, t: (b, t, 0)))
    xprev = pl.BlockSpec((1, HALO, D_MODEL),
                         at_tile(lambda b, t: (b, jnp.maximum(t * per - 1, 0), 0)))
    xnext = pl.BlockSpec((1, HALO, D_MODEL),
                         at_tile(lambda b, t: (b, jnp.minimum((t + 1) * per, nt * per - 1), 0)))
    plane = pl.BlockSpec((1, PLANES, TM // PLANES, ATTN_W), at_tile(lambda b, t: (b, 0, t, 0)))
    plane_shape = (B, PLANES, S // PLANES, ATTN_W)
    per_batch = lambda r, w: pl.BlockSpec((1, r, w), at_tile(lambda b, t: (b, 0, 0)))
    return pl.pallas_call(
        functools.partial(_in_proj_kernel, n_tiles=nt),
        grid=(N_CAST_IN + B * nt,),
        in_specs=[tok(D_MODEL), xprev, xnext,
                  pl.BlockSpec((1, 1, TM), at_tile(lambda b, t: (b, 0, t))),
                  _const_spec((1, D_MODEL)),
                  pl.BlockSpec((D_MODEL, IN_CAST),
                               lambda step: (0, jnp.minimum(step, N_CAST_IN - 1))),
                  _const_spec((1, ATTN_W)), _const_spec((1, ATTN_W)), _const_spec((1, MEM_W)),
                  _const_spec((MXU_DIM, MXU_DIM)), _const_spec((SUBLANES, LANES)),
                  _const_spec((SUBLANES, LANES)), _const_spec((LANES, 2 * LANES)),
                  _const_spec((TM, TM)),
                  _const_spec((CONV_K, CONV_W)), _const_spec((1, CONV_W)),
                  _const_spec((1, CONV_W)), _const_spec((1, CONV_W)),
                  per_batch(MEM_W, N_MEM), per_batch(N_MEM, MEM_W)],
        out_specs=[plane, plane, plane, tok(ATTN_W), tok(ATTN_W), tok(CONV_W), tok(MEM_W)],
        out_shape=[jax.ShapeDtypeStruct(plane_shape, F32),
                   jax.ShapeDtypeStruct(plane_shape, BF16),
                   jax.ShapeDtypeStruct(plane_shape, BF16),
                   jax.ShapeDtypeStruct((B, S, ATTN_W), BF16),
                   jax.ShapeDtypeStruct((B, S, ATTN_W), BF16),
                   jax.ShapeDtypeStruct((B, S, CONV_W), BF16),
                   jax.ShapeDtypeStruct((B, S, MEM_W), BF16)],
        scratch_shapes=[pltpu.VMEM((D_MODEL, IN_COLS), BF16)],
        compiler_params=pltpu.CompilerParams(dimension_semantics=("arbitrary",),
                                             vmem_limit_bytes=VMEM_LIMIT),
        name="in_proj",
    )(x, x, x, pos_row, g, w_in, gq, gk, gm, gsum, invf, lane_tab, expand, perm,
      conv_w, conv_b, conv_g, conv_beta, kmt, vm)


def _mem_kv_kernel(mem_ref, g_ref, w_ref, gk_ref, gsum_ref, kmt_ref, vm_ref):
    mn = _rms_rows(mem_ref[0], g_ref[...]).astype(BF16)
    kv = jnp.dot(mn, w_ref[...].astype(BF16), preferred_element_type=F32)
    (km,) = _head_norm(kv[:, :MEM_W], gk_ref[...], gsum_ref[...])
    kmt_ref[0] = km.T.astype(BF16)
    vm_ref[0] = kv[:, MEM_W:].astype(BF16)


def _mem_kv(mem, g, w, gk, gsum):
    B = mem.shape[0]
    return pl.pallas_call(
        _mem_kv_kernel,
        grid=(B,),
        in_specs=[pl.BlockSpec((1, N_MEM, D_MODEL), lambda b: (b, 0, 0)),
                  _const_spec((1, D_MODEL)), _const_spec((D_MODEL, 2 * MEM_W)),
                  _const_spec((1, MEM_W)), _const_spec((MXU_DIM, MXU_DIM))],
        out_specs=[pl.BlockSpec((1, MEM_W, N_MEM), lambda b: (b, 0, 0)),
                   pl.BlockSpec((1, N_MEM, MEM_W), lambda b: (b, 0, 0))],
        out_shape=[jax.ShapeDtypeStruct((B, MEM_W, N_MEM), BF16),
                   jax.ShapeDtypeStruct((B, N_MEM, MEM_W), BF16)],
        compiler_params=_cparams(1),
        name="mem_kv",
    )(mem, g, w, gk, gsum)


def _head_masks():
    lane = lax.broadcasted_iota(jnp.int32, (1, LANES), 1)
    low = lane < HEAD_DIM
    return low, (jnp.where(low, 1.0, 0.0).astype(BF16), jnp.where(low, 0.0, 1.0).astype(BF16))


def _mem_attn(qm, kmt_ref, vm_ref):
    low, hmask = _head_masks()
    out = []
    for c in range(MEM_W // LANES):
        sl = slice(c * LANES, (c + 1) * LANES)
        qc = qm[:, sl]
        halves = []
        for e in range(2):
            s = jnp.dot(qc * hmask[e], kmt_ref[0, sl, :], preferred_element_type=F32)
            m = jnp.max(s, axis=-1, keepdims=True)
            p = jnp.exp(s - m)
            l = jnp.sum(p, axis=-1, keepdims=True)
            o = jnp.dot(p.astype(BF16), vm_ref[0, :, sl], preferred_element_type=F32)
            halves.append(o / l)
        out.append(jnp.where(low, halves[0], halves[1]).astype(BF16))
    return jnp.concatenate(out, axis=1)


HW = ATTN_W // 2
MID_ROWS = TL // MID_DIL
MID_KEYS = TK // MID_DIL
MID_LEAD = (MID_KEYS - MID_ROWS) // 2
ONE_ROWS = TL // PLANES
TILE_GROUP = 16


def _band_tables():
    rho = np.arange(TL)[:, None]
    kap = np.arange(TK)[None, :]
    d16 = kap - BAND_HALF - rho
    j, lq = rho // MID_ROWS, rho % MID_ROWS
    jk, lk = kap // MID_KEYS, kap % MID_KEYS
    d4 = MID_DIL * (lk - MID_LEAD - lq) + (jk - j)
    r, l1 = rho // ONE_ROWS, rho % ONE_ROWS
    d1 = kap - BAND_HALF - PLANES * l1 - r
    tabs = [np.where(np.abs(d) <= BAND_HALF, 0.0, NEG_INF) for d in (d16, d4, d1)]
    return jnp.asarray(np.stack(tabs), F32)


ALL_CHUNKS = tuple(range(HW // LANES))


def _scores(q, kw, bias2, hmask, chunks=ALL_CHUNKS):
    out = []
    for c in chunks:
        sl = slice(c * LANES, (c + 1) * LANES)
        qc = q[:, sl]
        qs = jnp.concatenate([qc * hmask[0], qc * hmask[1]], axis=0)
        out.append(lax.dot_general(qs, kw[:, sl], (((1,), (1,)), ((), ())),
                                   preferred_element_type=F32) + bias2)
    return out


def _softmax_pv(scores, vw, low, hmask, m_old=None, chunks=ALL_CHUNKS):
    ones = jnp.ones((TK, LANES), BF16)
    res = []
    for c, s in zip(chunks, scores):
        m = jnp.max(s, axis=-1, keepdims=True)
        if m_old is None:
            shift = m
        else:
            m = jnp.maximum(m, m_old[c])
            shift = jnp.concatenate([m] * (TK // LANES), axis=1)
        p = jnp.exp((s - shift).astype(BF16))
        v_aug = jnp.concatenate([vw[:, c * LANES:(c + 1) * LANES], ones], axis=1)
        r = jnp.dot(p, v_aug, preferred_element_type=F32)
        res.append((m, jnp.where(low, r[:TL, LANES:], r[TL:, LANES:]),
                    jnp.where(low, r[:TL, :LANES], r[TL:, :LANES])))
    return res


def _attend(q, kw, vw, bias2, low, hmask, m_old=None):
    res = []
    for c in ALL_CHUNKS:
        res += _softmax_pv(_scores(q, kw, bias2, hmask, (c,)), vw, low, hmask, m_old, (c,))
    return res


def _merge(new, m_old, l_old, acc_old, low):
    m, l_n, acc_n = new
    b = jnp.exp(m_old - m)
    b = jnp.where(low, b[:TL], b[TL:])
    return m, l_n + b * l_old, acc_n + b * acc_old


def _dil_attn_kernel(q_ref, kc_ref, kp_ref, kx_ref, vc_ref, vp_ref, vx_ref,
                     knc_ref, knp_ref, knx_ref, vnc_ref, vnp_ref, vnx_ref, band_ref, unperm_ref,
                     o_ref, acc_ref, m_ref, l_ref, kne_ref, vne_ref):
    st = pl.program_id(1)
    first, last = st == 0, st == pl.num_programs(1) - 1
    low, hmask = _head_masks()
    col = lax.broadcasted_iota(jnp.int32, (1, TK), 1)
    chunk = lambda c: slice(c * LANES, (c + 1) * LANES)

    def stacked_bias(band, col_idx=None, lo=0, hi=TK):
        if col_idx is not None:
            band = band + jnp.where((col_idx < lo) | (col_idx >= hi), NEG_INF, 0.0)
        return jnp.concatenate([band, band], axis=0)

    bias16 = stacked_bias(band_ref[0], col, jnp.where(first, BAND_HALF, 0),
                          jnp.where(last, TK - BAND_HALF, TK))

    def pipelined(tiles, score_fn, finish_fn):
        s_next = score_fn(tiles[0])
        for i, t in enumerate(tiles):
            s = s_next
            if i + 1 < len(tiles):
                s_next = score_fn(tiles[i + 1])
            finish_fn(t, s)

    def body16(g, carry):
        def score(r):
            kw = jnp.concatenate([kp_ref[0, r], kc_ref[0, r], kx_ref[0, r]], axis=0)
            return _scores(q_ref[0, r].astype(BF16), kw, bias16, hmask)

        def finish(r, s):
            vw = jnp.concatenate([vp_ref[0, r], vc_ref[0, r], vx_ref[0, r]], axis=0)
            for c, (m, l, acc) in enumerate(_softmax_pv(s, vw, low, hmask)):
                acc_ref[r, :, chunk(c)] = acc
                l_ref[c, r] = l
                for e in range(2):
                    m_ref[c, e, r] = jnp.broadcast_to(m[e * TL:(e + 1) * TL], (TL, LANES))

        pipelined([g * TILE_GROUP + i for i in range(TILE_GROUP)], score, finish)
        return carry

    lax.fori_loop(0, PLANES // TILE_GROUP, body16, 0)

    def mid_window(cur, prev, nxt, plane, lb):
        lo = lb * MID_ROWS - MID_LEAD
        if lo < 0:
            return jnp.concatenate([prev[0, plane, TL // 2 + lo:TL // 2],
                                    cur[0, plane, 0:lo + MID_KEYS]], axis=0)
        if lo + MID_KEYS > TL:
            return jnp.concatenate([cur[0, plane, lo:TL],
                                    nxt[0, plane, 0:lo + MID_KEYS - TL]], axis=0)
        return cur[0, plane, lo:lo + MID_KEYS]

    def body4(g, carry):
        block_rows = lambda lb: slice(lb * MID_ROWS, (lb + 1) * MID_ROWS)
        plane_set = lambda r4: [r4 + MID_DIL * j for j in range(MID_DIL)]

        def score(tile):
            r4, lb = tile
            planes = plane_set(r4)
            band = band_ref[1]
            if lb == 0:
                bias = stacked_bias(band, col % MID_KEYS, jnp.where(first, MID_LEAD, 0), MID_KEYS)
            elif lb == TL // MID_ROWS - 1:
                bias = stacked_bias(band, col % MID_KEYS, 0,
                                    jnp.where(last, MID_KEYS - MID_LEAD, MID_KEYS))
            else:
                bias = stacked_bias(band)
            q = jnp.concatenate([q_ref[0, p, block_rows(lb)] for p in planes],
                                axis=0).astype(BF16)
            kw = jnp.concatenate([mid_window(kc_ref, kp_ref, kx_ref, p, lb) for p in planes],
                                 axis=0)
            return _scores(q, kw, bias, hmask)

        def finish(tile, s):
            r4, lb = tile
            planes = plane_set(r4)
            rows = block_rows(lb)
            stacked = lambda ref, c: jnp.concatenate(
                [ref[c, e, p, rows] for e in range(2) for p in planes], axis=0)
            vw = jnp.concatenate([mid_window(vc_ref, vp_ref, vx_ref, p, lb) for p in planes],
                                 axis=0)
            m_old = [stacked(m_ref, c) for c in range(HW // LANES)]
            for c, new in enumerate(_softmax_pv(s, vw, low, hmask, m_old)):
                acc_old = jnp.concatenate([acc_ref[p, rows, chunk(c)] for p in planes], axis=0)
                l_old = jnp.concatenate([l_ref[c, p, rows] for p in planes], axis=0)
                m_m, l_m, acc_m = _merge(new, m_old[c], l_old, acc_old, low)
                for j, p in enumerate(planes):
                    piece = slice(j * MID_ROWS, (j + 1) * MID_ROWS)
                    acc_ref[p, rows, chunk(c)] = acc_m[piece]
                    l_ref[c, p, rows] = l_m[piece]
                    for e in range(2):
                        head_piece = slice(e * TL + j * MID_ROWS, e * TL + (j + 1) * MID_ROWS)
                        m_ref[c, e, p, rows] = m_m[head_piece]

        per_body = TILE_GROUP // (TL // MID_ROWS)
        pipelined([(g * per_body + i, lb) for i in range(per_body)
                   for lb in range(TL // MID_ROWS)], score, finish)
        return carry

    lax.fori_loop(0, MID_DIL * (TL // MID_ROWS) // TILE_GROUP, body4, 0)

    n_tiles = SUPER // TL
    for ext_ref, prev, cur, nxt in ((kne_ref, knp_ref, knc_ref, knx_ref),
                                    (vne_ref, vnp_ref, vnc_ref, vnx_ref)):
        ext_ref[0:BAND_HALF] = prev[0]
        ext_ref[BAND_HALF:BAND_HALF + SUPER] = cur[0]
        ext_ref[BAND_HALF + SUPER:] = nxt[0]
    band1 = stacked_bias(band_ref[2])

    def body1(g, carry):
        tiles = [g * TILE_GROUP + i for i in range(TILE_GROUP)]
        tile_rows = lambda t: pl.ds(pl.multiple_of(t * ONE_ROWS, ONE_ROWS), ONE_ROWS)
        tile_keys = lambda t: pl.ds(pl.multiple_of(t * TL, TL), TK)

        def tile_scores(t):
            lo = jnp.where(first & (t == 0), BAND_HALF, 0)
            hi = jnp.where(last & (t == n_tiles - 1), TK - BAND_HALF, TK)
            bias = band1 + jnp.where((col < lo) | (col >= hi), NEG_INF, 0.0)
            q = q_ref[0, :, tile_rows(t), :].reshape(TL, HW).astype(BF16)
            return _scores(q, kne_ref[tile_keys(t), :], bias, hmask)

        def emit(t, merged):
            tok = jnp.dot(unperm_ref[...], merged, preferred_element_type=F32)
            o_ref[0, pl.ds(pl.multiple_of(t * TL, TL), TL), :] = tok.astype(BF16)

        pending = []

        def finish(t, s):
            rows = tile_rows(t)
            stacked = lambda ref, c: jnp.concatenate(
                [ref[c, e, :, rows, :].reshape(TL, LANES) for e in range(2)], axis=0)
            m_old = [stacked(m_ref, c) for c in range(HW // LANES)]
            outs = []
            for c, new in enumerate(_softmax_pv(s, vne_ref[tile_keys(t), :], low, hmask, m_old)):
                acc_old = acc_ref[:, rows, chunk(c)].reshape(TL, LANES)
                l_old = l_ref[c, :, rows, :].reshape(TL, LANES)
                _, l_m, acc_m = _merge(new, m_old[c], l_old, acc_old, low)
                outs.append((acc_m / l_m).astype(BF16))
            if pending:
                emit(*pending.pop())
            pending.append((t, jnp.concatenate(outs, axis=1)))

        pipelined(tiles, tile_scores, finish)
        emit(*pending.pop())
        return carry

    lax.fori_loop(0, n_tiles // TILE_GROUP, body1, 0)


def _dil_attn(q16, k16, v16, kn, vn, band, unperm):
    B, S, _ = kn.shape
    n_half = S // PLANES // BAND_HALF
    n_tok_half = S // BAND_HALF
    per = TL // BAND_HALF
    per_tok = SUPER // BAND_HALF
    pcur = pl.BlockSpec((1, PLANES, TL, HW), lambda b, s, hh: (b, 0, s, hh))
    pprev = pl.BlockSpec((1, PLANES, BAND_HALF, HW),
                         lambda b, s, hh: (b, 0, jnp.maximum(s * per - 1, 0), hh))
    pnext = pl.BlockSpec((1, PLANES, BAND_HALF, HW),
                         lambda b, s, hh: (b, 0, jnp.minimum((s + 1) * per, n_half - 1), hh))
    tcur = pl.BlockSpec((1, SUPER, HW), lambda b, s, hh: (b, s, hh))
    tprev = pl.BlockSpec((1, BAND_HALF, HW),
                         lambda b, s, hh: (b, jnp.maximum(s * per_tok - 1, 0), hh))
    tnext = pl.BlockSpec((1, BAND_HALF, HW),
                         lambda b, s, hh: (b, jnp.minimum((s + 1) * per_tok, n_tok_half - 1), hh))
    return pl.pallas_call(
        _dil_attn_kernel,
        grid=(B, S // SUPER, ATTN_W // HW),
        in_specs=[pcur, pcur, pprev, pnext, pcur, pprev, pnext,
                  tcur, tprev, tnext, tcur, tprev, tnext,
                  _const_spec((3, TL, TK)), _const_spec((TL, TL))],
        out_specs=tcur,
        out_shape=jax.ShapeDtypeStruct((B, S, ATTN_W), BF16),
        scratch_shapes=[pltpu.VMEM((PLANES, TL, HW), F32),
                        pltpu.VMEM((HW // LANES, 2, PLANES, TL, LANES), F32),
                        pltpu.VMEM((HW // LANES, PLANES, TL, LANES), F32),
                        pltpu.VMEM((SUPER + 2 * BAND_HALF, HW), BF16),
                        pltpu.VMEM((SUPER + 2 * BAND_HALF, HW), BF16)],
        compiler_params=_cparams(3),
        name="dil_attn",
    )(q16, k16, k16, k16, v16, v16, v16, kn, kn, kn, vn, vn, vn, band, unperm)


def _halo_specs(width, n_tiles, rows, tile=TM):
    per = tile // rows
    last = n_tiles * per - 1
    prev = pl.BlockSpec((1, rows, width), lambda b, t: (b, jnp.maximum(t * per - 1, 0), 0))
    nxt = pl.BlockSpec((1, rows, width), lambda b, t: (b, jnp.minimum((t + 1) * per, last), 0))
    return prev, nxt


def _conv_module(buf, n_out, w_ref, b_ref, g_ref, beta_ref):
    rows = n_out + 2 * HALO
    acc = jnp.zeros((n_out, CONV_W), F32) + b_ref[...]
    base = HALO - CONV_K // 2
    for shift in range(SUBLANES):
        rolled = buf if shift == 0 else pltpu.roll(buf, rows - shift, 0)
        for tap in range(CONV_K):
            off = base + tap
            if off % SUBLANES == shift:
                lo = off - shift
                acc = acc + w_ref[tap:tap + 1, :] * rolled[lo:lo + n_out]
    mu = jnp.mean(acc, axis=-1, keepdims=True)
    d = acc - mu
    var = jnp.mean(d * d, axis=-1, keepdims=True)
    z = d * lax.rsqrt(var + NORM_EPS) * g_ref[...] + beta_ref[...]
    return z * _sigmoid(z)


N_CAST = 11
UP_CAST = 2 * D_FF // N_CAST
DOWN_CAST = D_FF // N_CAST
OUT_CAST = LANES


def _out_ffn_kernel(a_ref, ap_ref, ax_ref, c_ref, cp_ref, cx_ref, m_ref, mp_ref, mx_ref,
                    x_ref, xp_ref, xx_ref, wo32_ref, g_ref, wu32_ref, dw_ref, db_ref, wd32_ref,
                    o_ref, gate_ref, wo_ref, wu_ref, wd_ref, *, n_tiles):
    step = pl.program_id(0)

    for c in range(N_CAST):
        @pl.when(step == c)
        def _(c=c):
            for half in range(UP_CAST // FF_CH):
                src = c * UP_CAST + half * FF_CH
                j, is_up = (src // FF_CH, 0) if src < D_FF else ((src - D_FF) // FF_CH, 1)
                dst = (2 * j + is_up) * FF_CH
                wu_ref[:, dst:dst + FF_CH] = wu32_ref[:, half * FF_CH:(half + 1) * FF_CH].astype(BF16)
            wd_ref[c * DOWN_CAST:(c + 1) * DOWN_CAST, :] = wd32_ref[...].astype(BF16)
            if (c + 1) * OUT_CAST <= D_MODEL:
                wo_ref[c * OUT_CAST:(c + 1) * OUT_CAST, :] = wo32_ref[...].astype(BF16)

    @pl.when(step >= N_CAST)
    def _():
        _out_ffn_tile(a_ref, ap_ref, ax_ref, c_ref, cp_ref, cx_ref, m_ref, mp_ref, mx_ref,
                      x_ref, xp_ref, xx_ref, wo_ref, g_ref, wu_ref, dw_ref, db_ref, wd_ref,
                      o_ref, gate_ref, lax.rem(step - N_CAST, n_tiles), n_tiles)


def _out_ffn_tile(a_ref, ap_ref, ax_ref, c_ref, cp_ref, cx_ref, m_ref, mp_ref, mx_ref,
                  x_ref, xp_ref, xx_ref, wo_ref, g_ref, wu_ref, dw_ref, db_ref, wd_ref,
                  o_ref, gate_ref, t, nt):
    has_next, has_prev = t < nt - 1, t > 0
    rows = TF + HALO

    def ext_rows(cur, nxt, prev):
        halo = jnp.concatenate([nxt[0, :SUBLANES].astype(F32), prev[0, SUBLANES:].astype(F32)],
                               axis=0)
        return jnp.concatenate([cur[0], halo.astype(cur.dtype)], axis=0)

    rid = lax.broadcasted_iota(jnp.int32, (rows, 1), 0)
    in_seq = ((rid < TF) | ((rid < TF + SUBLANES) & has_next)
              | ((rid >= TF + SUBLANES) & has_prev))
    mixed = jnp.concatenate([ext_rows(a_ref, ax_ref, ap_ref), ext_rows(c_ref, cx_ref, cp_ref),
                             ext_rows(m_ref, mx_ref, mp_ref)], axis=-1)
    mixed = jnp.where(in_seq, mixed, jnp.zeros_like(mixed))
    h = (jnp.where(in_seq, ext_rows(x_ref, xx_ref, xp_ref), 0.0)
         + jnp.dot(mixed, wo_ref[...], preferred_element_type=F32))
    ext = _rms_rows(h, g_ref[...]).astype(BF16)

    def conv3(f, lo):
        w = dw_ref[:, lo:lo + FF_CH]
        y = (w[0:1] * pltpu.roll(f, 1, 0) + w[1:2] * f + w[2:3] * pltpu.roll(f, rows - 1, 0))
        return y[:TF] + db_ref[:, lo:lo + FF_CH]

    n_chunks = D_FF // FF_CH
    out = h[:TF]
    for j in range(n_chunks):
        lo_g, lo_u = j * FF_CH, D_FF + j * FF_CH
        f = jnp.dot(ext, wu_ref[:, 2 * lo_g:2 * lo_g + 2 * FF_CH], preferred_element_type=F32)
        if j == n_chunks - 1:
            out = out + jnp.dot(gate_ref[:, :lo_g], wd_ref[:lo_g, :], preferred_element_type=F32)
        fg, fu = conv3(f[:, :FF_CH], lo_g), conv3(f[:, FF_CH:], lo_u)
        gate_ref[:, lo_g:lo_g + FF_CH] = (fg * _sigmoid(fg) * fu).astype(BF16)
    o_ref[0] = out + jnp.dot(gate_ref[:, D_FF - FF_CH:], wd_ref[D_FF - FF_CH:, :],
                             preferred_element_type=F32)


def _out_ffn(attn, cb, mo, x, w_out, g, w_up, dw_w, dw_b, w_down):
    B, S, _ = x.shape
    nt = S // TF
    per = TF // HALO

    def tile_of(step):
        i = jnp.maximum(step - N_CAST, 0)
        return i // nt, i % nt

    def with_halos(width):
        def cur(step):
            b, t = tile_of(step)
            return b, t, 0

        def prev(step):
            b, t = tile_of(step)
            return b, jnp.maximum(t * per - 1, 0), 0

        def nxt(step):
            b, t = tile_of(step)
            return b, jnp.minimum((t + 1) * per, nt * per - 1), 0

        return [pl.BlockSpec((1, TF, width), cur), pl.BlockSpec((1, HALO, width), prev),
                pl.BlockSpec((1, HALO, width), nxt)]

    cast_chunk = lambda limit: (lambda step: jnp.minimum(step, limit - 1))
    out_i, up_i, down_i = cast_chunk(D_MODEL // OUT_CAST), cast_chunk(N_CAST), cast_chunk(N_CAST)
    return pl.pallas_call(
        functools.partial(_out_ffn_kernel, n_tiles=nt),
        grid=(N_CAST + B * nt,),
        in_specs=with_halos(ATTN_W) + with_halos(CONV_W) + with_halos(MEM_W)
        + with_halos(D_MODEL)
        + [pl.BlockSpec((OUT_CAST, D_MODEL), lambda s: (out_i(s), 0)), _const_spec((1, D_MODEL)),
           pl.BlockSpec((D_MODEL, UP_CAST), lambda s: (0, up_i(s))),
           _const_spec((FFN_CONV_K, 2 * D_FF)), _const_spec((1, 2 * D_FF)),
           pl.BlockSpec((DOWN_CAST, D_MODEL), lambda s: (down_i(s), 0))],
        out_specs=pl.BlockSpec((1, TF, D_MODEL), lambda s: (*tile_of(s), 0)),
        out_shape=jax.ShapeDtypeStruct((B, S, D_MODEL), F32),
        scratch_shapes=[pltpu.VMEM((TF, D_FF), BF16), pltpu.VMEM((D_MODEL, D_MODEL), BF16),
                        pltpu.VMEM((D_MODEL, 2 * D_FF), BF16), pltpu.VMEM((D_FF, D_MODEL), BF16)],
        compiler_params=pltpu.CompilerParams(dimension_semantics=("arbitrary",),
                                             vmem_limit_bytes=VMEM_LIMIT),
        name="out_ffn",
    )(attn, attn, attn, cb, cb, cb, mo, mo, mo, x, x, x, w_out, g, w_up, dw_w, dw_b, w_down)


def _rope_tables():
    inv_freq = ROPE_THETA ** (-jnp.arange(0, ROT_DIM, 2, dtype=F32) / ROT_DIM)
    invf = jnp.broadcast_to(inv_freq[:, None], (SUBLANES, LANES))
    half = ROT_DIM // 2
    lane = np.arange(LANES) % HEAD_DIM
    lanes = np.zeros((SUBLANES, LANES), np.float32)
    lanes[0] = lane >= ROT_DIM
    lanes[1] = lane < half
    expand = np.zeros((LANES, 2 * LANES), np.float32)
    for l in range(LANES):
        if lane[l] < ROT_DIM:
            j = lane[l] % half
            expand[[j, half + j], l] = 1.0
            expand[[2 * half + j, 3 * half + j], LANES + l] = -1.0 if lane[l] < half else 1.0
    return invf, jnp.asarray(lanes), jnp.asarray(expand, BF16)


def _group_sum_matrix():
    idx = np.arange(MXU_DIM) // HEAD_DIM
    return jnp.asarray(idx[:, None] == idx[None, :], BF16)


def _plane_perm(n):
    out = np.arange(n)
    src = (out % (n // PLANES)) * PLANES + out // (n // PLANES)
    return np.asarray(src[:, None] == np.arange(n)[None, :], np.float32)


def kernel(x, mem, positions, mix_norm_g, mem_norm_g, w_in, w_mem_kv, q_norm_g, k_norm_g, mq_norm_g, mk_norm_g, conv_dw_w, conv_dw_b, conv_ln_g, conv_ln_b, w_out, ffn_norm_g, w_up, ffn_dw_w, ffn_dw_b, w_down):
    B, S, _ = x.shape
    depth = w_in.shape[0]
    pos_row = positions.reshape(B, 1, S)
    invf, lane_tab, expand = _rope_tables()
    gsum = _group_sum_matrix()
    perm = jnp.asarray(_plane_perm(TM), BF16)
    unperm = jnp.asarray(_plane_perm(TL).T, BF16)
    band = _band_tables()
    row = lambda a: a.reshape(1, -1)
    h = x
    for l in range(depth):
        kmt, vm = _mem_kv(mem, row(mem_norm_g[l]), w_mem_kv[l],
                          row(jnp.tile(mk_norm_g[l], MEM_HEADS)), gsum)
        q16, k16, v16, kn, vn, cb, mo = _in_proj(
            h, pos_row, row(mix_norm_g[l]), w_in[l],
            row(jnp.tile(q_norm_g[l], ATTN_HEADS)), row(jnp.tile(k_norm_g[l], ATTN_HEADS)),
            row(jnp.tile(mq_norm_g[l], MEM_HEADS)), gsum, invf, lane_tab, expand, perm,
            conv_dw_w[l], row(conv_dw_b[l]), row(conv_ln_g[l]), row(conv_ln_b[l]), kmt, vm)
        attn = _dil_attn(q16, k16, v16, kn, vn, band, unperm)
        h = _out_ffn(attn, cb, mo, h, w_out[l], row(ffn_norm_g[l]), w_up[l], ffn_dw_w[l],
                     row(ffn_dw_b[l]), w_down[l])
    return h
```

```python
import functools

import numpy as np
import jax
import jax.numpy as jnp
from jax import lax
from jax.experimental import pallas as pl
from jax.experimental.pallas import tpu as pltpu

F32 = jnp.float32
BF16 = jnp.bfloat16

D_MODEL = 1024
HEAD_DIM = 64
ATTN_HEADS = 8
ATTN_W = ATTN_HEADS * HEAD_DIM
CONV_W = 256
MEM_HEADS = 4
MEM_W = MEM_HEADS * HEAD_DIM
N_MEM = 256
PLANES = 16
MID_DIL = 4
BAND_HALF = 64
ROPE_THETA = 500000.0
ROT_DIM = HEAD_DIM // 4
CONV_K = 31
FFN_CONV_K = 3
D_FF = 2816
NORM_EPS = 1e-6
NEG_INF = -1e30
SM_SCALE = HEAD_DIM ** -0.5

LANES = 128
SUBLANES = 8
MXU_DIM = 256
BF16_ROWS = 16
VMEM_LIMIT = 56 * 1024 * 1024

TM = 512
SUB_ROWS = 256
TF = 512
TL = 128
TK = TL + 2 * BAND_HALF
SUPER = PLANES * TL
FF_CH = MXU_DIM
HALO = BF16_ROWS


def _cparams(n_axes):
    return pltpu.CompilerParams(dimension_semantics=("parallel",) * n_axes,
                                vmem_limit_bytes=VMEM_LIMIT)


def _const_spec(shape):
    return pl.BlockSpec(shape, lambda *_: (0,) * len(shape), pipeline_mode=pl.Buffered(1))


def _rms_rows(x, g):
    r = lax.rsqrt(jnp.mean(x * x, axis=-1, keepdims=True) + NORM_EPS)
    return x * r * g


def _head_sumsq(t, gsum):
    return [jnp.dot((tc * tc).astype(BF16), gsum, preferred_element_type=F32)
            for tc in (t[:, c * MXU_DIM:(c + 1) * MXU_DIM] for c in range(t.shape[1] // MXU_DIM))]


def _head_scale(t, sumsq, gain):
    outs = []
    for c, ssum in enumerate(sumsq):
        sl = slice(c * MXU_DIM, (c + 1) * MXU_DIM)
        outs.append(t[:, sl] * lax.rsqrt(ssum * (1.0 / HEAD_DIM) + NORM_EPS) * gain[:, sl])
    return outs


def _head_norm(t, gain, gsum):
    return _head_scale(t, _head_sumsq(t, gsum), gain)


def _sigmoid(x):
    return 1.0 / (1.0 + jnp.exp(-x))


IN_COLS = 3 * ATTN_W + 2 * CONV_W + MEM_W
IN_CAST = MXU_DIM
N_CAST_IN = IN_COLS // IN_CAST


def _in_proj_kernel(x_ref, xp_ref, xx_ref, pos_ref, g_ref, w32_ref, gq_ref, gk_ref, gm_ref,
                    gsum_ref, invf_ref, lane_ref, expand_ref, perm_ref,
                    cw_ref, cbias_ref, cg_ref, cbeta_ref, kmt_ref, vm_ref,
                    q16_ref, k16_ref, v16_ref, kn_ref, vn_ref, cb_ref, mo_ref,
                    w_ref, *, n_tiles):
    step = pl.program_id(0)

    for c in range(N_CAST_IN):
        @pl.when(step == c)
        def _(c=c):
            w_ref[:, c * IN_CAST:(c + 1) * IN_CAST] = w32_ref[...].astype(BF16)

    @pl.when(step >= N_CAST_IN)
    def _():
        _in_proj_tile(x_ref, xp_ref, xx_ref, pos_ref, g_ref, w_ref, gq_ref, gk_ref, gm_ref,
                      gsum_ref, invf_ref, lane_ref, expand_ref, perm_ref,
                      cw_ref, cbias_ref, cg_ref, cbeta_ref, kmt_ref, vm_ref,
                      q16_ref, k16_ref, v16_ref, kn_ref, vn_ref, cb_ref, mo_ref,
                      lax.rem(step - N_CAST_IN, n_tiles), n_tiles)


def _in_proj_tile(x_ref, xp_ref, xx_ref, pos_ref, g_ref, w_ref, gq_ref, gk_ref, gm_ref,
                  gsum_ref, invf_ref, lane_ref, expand_ref, perm_ref,
                  cw_ref, cbias_ref, cg_ref, cbeta_ref, kmt_ref, vm_ref,
                  q16_ref, k16_ref, v16_ref, kn_ref, vn_ref, cb_ref, mo_ref,
                  t, nt):
    has_prev, has_next = t > 0, t < nt - 1
    hn = _rms_rows(x_ref[0], g_ref[...]).astype(BF16)

    hn_halo = _rms_rows(jnp.concatenate([xp_ref[0], xx_ref[0]], axis=0), g_ref[...]).astype(BF16)
    hn_ext = jnp.concatenate([hn_halo[:HALO], hn, hn_halo[HALO:]], axis=0)
    cols = lambda lo, n: w_ref[:, lo:lo + n]
    groups = [slice(lo, lo + SUB_ROWS) for lo in range(0, TM, SUB_ROWS)]
    raw = [[jnp.dot(hn[rows], cols(lo, n), preferred_element_type=F32)
            for lo, n in ((0, ATTN_W), (ATTN_W, ATTN_W), (2 * ATTN_W, ATTN_W),
                          (3 * ATTN_W + 2 * CONV_W, MEM_W))] for rows in groups]
    glu = jnp.dot(hn_ext, cols(3 * ATTN_W, 2 * CONV_W), preferred_element_type=F32)

    ang = invf_ref[:, 0:1] * pos_ref[0].astype(F32)

    def hi_lo(v):
        hi = v.astype(BF16).astype(F32)
        return [hi, (v - hi).astype(BF16).astype(F32)]

    tab = jnp.concatenate(hi_lo(jnp.cos(ang)) + hi_lo(jnp.sin(ang))
                          + [jnp.zeros((LANES - 4 * SUBLANES, TM), F32)], axis=0)
    cs = jnp.dot(tab.T.astype(BF16), expand_ref[...], preferred_element_type=F32)
    cosv = cs[:, :LANES] + lane_ref[0:1, :]
    sinv = cs[:, LANES:]
    first_half = lane_ref[1:2, :] > 0.5

    def rotary(chunks, rows, scale):
        out = []
        for t in chunks:
            for s in range(MXU_DIM // LANES):
                xc = t[:, s * LANES:(s + 1) * LANES]
                partner = jnp.where(first_half, pltpu.roll(xc, LANES - ROT_DIM // 2, 1),
                                    pltpu.roll(xc, ROT_DIM // 2, 1))
                out.append(((xc * cosv[rows] + partner * sinv[rows]) * scale).astype(BF16))
        return jnp.concatenate(out, axis=1)

    gsum = gsum_ref[...]
    sumsq = [[_head_sumsq(t, gsum) for t in (q, k, qm)] for q, k, _, qm in raw]
    per_plane = SUB_ROWS // PLANES
    for i, (rows, (q, k, v, qm), (q_ss, k_ss, qm_ss)) in enumerate(zip(groups, raw, sumsq)):
        qb = rotary(_head_scale(q, q_ss, gq_ref[...]), rows, SM_SCALE)
        kb = rotary(_head_scale(k, k_ss, gk_ref[...]), rows, 1.0)
        vb = v.astype(BF16)
        kn_ref[0, rows] = kb
        vn_ref[0, rows] = vb
        qkv = jnp.dot(perm_ref[...], jnp.concatenate([qb, kb, vb], axis=1),
                      preferred_element_type=F32)
        dst = slice(i * per_plane, (i + 1) * per_plane)
        for r in range(PLANES):
            blk = qkv[r * per_plane:(r + 1) * per_plane]
            q16_ref[0, r, dst] = blk[:, 0:ATTN_W]
            k16_ref[0, r, dst] = blk[:, ATTN_W:2 * ATTN_W].astype(BF16)
            v16_ref[0, r, dst] = blk[:, 2 * ATTN_W:].astype(BF16)
        (qmn,) = _head_scale(qm, qm_ss, gm_ref[...])
        mo_ref[0, rows] = _mem_attn((qmn * SM_SCALE).astype(BF16), kmt_ref, vm_ref)

    rid = lax.broadcasted_iota(jnp.int32, (TM + 2 * HALO, 1), 0)
    in_seq = ((rid >= HALO) | has_prev) & ((rid < TM + HALO) | has_next)
    cbuf = jnp.where(in_seq, glu[:, :CONV_W] * _sigmoid(glu[:, CONV_W:]), 0.0)
    cb_ref[0] = _conv_rows(_conv_rolls(cbuf), 0, TM, cw_ref, cbias_ref, cg_ref,
                           cbeta_ref).astype(BF16)


def _in_proj(x, pos_row, g, w_in, gq, gk, gm, gsum, invf, lane_tab, expand, perm,
             conv_w, conv_b, conv_g, conv_beta, kmt, vm):
    B, S, _ = x.shape
    nt = S // TM
    per = TM // HALO

    def tile_of(step):
        i = jnp.maximum(step - N_CAST_IN, 0)
        return i // nt, i % nt

    def at_tile(fn):
        return lambda step: fn(*tile_of(step))

    tok = lambda w: pl.BlockSpec((1, TM, w), at_tile(lambda b, t: (b, t, 0)))
    xprev = pl.BlockSpec((1, HALO, D_MODEL),
                         at_tile(lambda b, t: (b, jnp.maximum(t * per - 1, 0), 0)))
    xnext = pl.BlockSpec((1, HALO, D_MODEL),
                         at_tile(lambda b, t: (b, jnp.minimum((t + 1) * per, nt * per - 1), 0)))
    plane = pl.BlockSpec((1, PLANES, TM // PLANES, ATTN_W), at_tile(lambda b, t: (b, 0, t, 0)))
    plane_shape = (B, PLANES, S // PLANES, ATTN_W)
    per_batch = lambda r, w: pl.BlockSpec((1, r, w), at_tile(lambda b, t: (b, 0, 0)))
    return pl.pallas_call(
        functools.partial(_in_proj_kernel, n_tiles=nt),
        grid=(N_CAST_IN + B * nt,),
        in_specs=[tok(D_MODEL), xprev, xnext,
                  pl.BlockSpec((1, 1, TM), at_tile(lambda b, t: (b, 0, t))),
                  _const_spec((1, D_MODEL)),
                  pl.BlockSpec((D_MODEL, IN_CAST),
                               lambda step: (0, jnp.minimum(step, N_CAST_IN - 1))),
                  _const_spec((1, ATTN_W)), _const_spec((1, ATTN_W)), _const_spec((1, MEM_W)),
                  _const_spec((MXU_DIM, MXU_DIM)), _const_spec((SUBLANES, LANES)),
                  _const_spec((SUBLANES, LANES)), _const_spec((LANES, 2 * LANES)),
                  _const_spec((SUB_ROWS, SUB_ROWS)),
                  _const_spec((CONV_K, CONV_W)), _const_spec((1, CONV_W)),
                  _const_spec((1, CONV_W)), _const_spec((1, CONV_W)),
                  per_batch(MEM_W, N_MEM), per_batch(N_MEM, MEM_W)],
        out_specs=[plane, plane, plane, tok(ATTN_W), tok(ATTN_W), tok(CONV_W), tok(MEM_W)],
        out_shape=[jax.ShapeDtypeStruct(plane_shape, F32),
                   jax.ShapeDtypeStruct(plane_shape, BF16),
                   jax.ShapeDtypeStruct(plane_shape, BF16),
                   jax.ShapeDtypeStruct((B, S, ATTN_W), BF16),
                   jax.ShapeDtypeStruct((B, S, ATTN_W), BF16),
                   jax.ShapeDtypeStruct((B, S, CONV_W), BF16),
                   jax.ShapeDtypeStruct((B, S, MEM_W), BF16)],
        scratch_shapes=[pltpu.VMEM((D_MODEL, IN_COLS), BF16)],
        compiler_params=pltpu.CompilerParams(dimension_semantics=("arbitrary",),
                                             vmem_limit_bytes=VMEM_LIMIT),
        name="in_proj",
    )(x, x, x, pos_row, g, w_in, gq, gk, gm, gsum, invf, lane_tab, expand, perm,
      conv_w, conv_b, conv_g, conv_beta, kmt, vm)


def _mem_kv_kernel(mem_ref, g_ref, w_ref, gk_ref, gsum_ref, kmt_ref, vm_ref):
    mn = _rms_rows(mem_ref[0], g_ref[...]).astype(BF16)
    kv = jnp.dot(mn, w_ref[...].astype(BF16), preferred_element_type=F32)
    (km,) = _head_norm(kv[:, :MEM_W], gk_ref[...], gsum_ref[...])
    kmt_ref[0] = km.T.astype(BF16)
    vm_ref[0] = kv[:, MEM_W:].astype(BF16)


def _mem_kv(mem, g, w, gk, gsum):
    B = mem.shape[0]
    return pl.pallas_call(
        _mem_kv_kernel,
        grid=(B,),
        in_specs=[pl.BlockSpec((1, N_MEM, D_MODEL), lambda b: (b, 0, 0)),
                  _const_spec((1, D_MODEL)), _const_spec((D_MODEL, 2 * MEM_W)),
                  _const_spec((1, MEM_W)), _const_spec((MXU_DIM, MXU_DIM))],
        out_specs=[pl.BlockSpec((1, MEM_W, N_MEM), lambda b: (b, 0, 0)),
                   pl.BlockSpec((1, N_MEM, MEM_W), lambda b: (b, 0, 0))],
        out_shape=[jax.ShapeDtypeStruct((B, MEM_W, N_MEM), BF16),
                   jax.ShapeDtypeStruct((B, N_MEM, MEM_W), BF16)],
        compiler_params=_cparams(1),
        name="mem_kv",
    )(mem, g, w, gk, gsum)


def _head_masks():
    lane = lax.broadcasted_iota(jnp.int32, (1, LANES), 1)
    low = lane < HEAD_DIM
    return low, (jnp.where(low, 1.0, 0.0).astype(BF16), jnp.where(low, 0.0, 1.0).astype(BF16))


def _mem_attn(qm, kmt_ref, vm_ref):
    low, hmask = _head_masks()
    chunks = [slice(c * LANES, (c + 1) * LANES) for c in range(MEM_W // LANES)]
    scores = [[jnp.dot(qm[:, sl] * hmask[e], kmt_ref[0, sl, :], preferred_element_type=F32)
               for e in range(2)] for sl in chunks]
    out = []
    for sl, pair in zip(chunks, scores):
        halves = []
        for s in pair:
            m = jnp.max(s, axis=-1, keepdims=True)
            p = jnp.exp(s - m)
            l = jnp.sum(p, axis=-1, keepdims=True)
            o = jnp.dot(p.astype(BF16), vm_ref[0, :, sl], preferred_element_type=F32)
            halves.append(o / l)
        out.append(jnp.where(low, halves[0], halves[1]).astype(BF16))
    return jnp.concatenate(out, axis=1)


HW = ATTN_W // 2
MID_ROWS = TL // MID_DIL
MID_KEYS = TK // MID_DIL
MID_LEAD = (MID_KEYS - MID_ROWS) // 2
ONE_ROWS = TL // PLANES
TILE_GROUP = 16


def _band_tables():
    rho = np.arange(TL)[:, None]
    kap = np.arange(TK)[None, :]
    d16 = kap - BAND_HALF - rho
    j, lq = rho // MID_ROWS, rho % MID_ROWS
    jk, lk = kap // MID_KEYS, kap % MID_KEYS
    d4 = MID_DIL * (lk - MID_LEAD - lq) + (jk - j)
    r, l1 = rho // ONE_ROWS, rho % ONE_ROWS
    d1 = kap - BAND_HALF - PLANES * l1 - r
    tabs = [np.where(np.abs(d) <= BAND_HALF, 0.0, NEG_INF) for d in (d16, d4, d1)]
    return jnp.asarray(np.stack(tabs), F32)


ALL_CHUNKS = tuple(range(HW // LANES))


def _scores(q, kw, bias2, hmask, chunks=ALL_CHUNKS):
    out = []
    for c in chunks:
        sl = slice(c * LANES, (c + 1) * LANES)
        qc = q[:, sl]
        qs = jnp.concatenate([qc * hmask[0], qc * hmask[1]], axis=0)
        out.append(lax.dot_general(qs, kw[:, sl], (((1,), (1,)), ((), ())),
                                   preferred_element_type=F32) + bias2)
    return out


def _softmax_pv(scores, vw, low, hmask, m_old=None, chunks=ALL_CHUNKS):
    ones = jnp.ones((TK, LANES), BF16)
    res = []
    for c, s in zip(chunks, scores):
        m = jnp.max(s, axis=-1, keepdims=True)
        if m_old is None:
            shift = m
        else:
            m = jnp.maximum(m, m_old[c])
            shift = jnp.concatenate([m] * (TK // LANES), axis=1)
        p = jnp.exp((s - shift).astype(BF16))
        v_aug = jnp.concatenate([vw[:, c * LANES:(c + 1) * LANES], ones], axis=1)
        r = jnp.dot(p, v_aug, preferred_element_type=F32)
        res.append((m, jnp.where(low, r[:TL, LANES:], r[TL:, LANES:]),
                    jnp.where(low, r[:TL, :LANES], r[TL:, :LANES])))
    return res


def _attend(q, kw, vw, bias2, low, hmask, m_old=None):
    res = []
    for c in ALL_CHUNKS:
        res += _softmax_pv(_scores(q, kw, bias2, hmask, (c,)), vw, low, hmask, m_old, (c,))
    return res


def _merge(new, m_old, l_old, acc_old, low):
    m, l_n, acc_n = new
    b = jnp.exp(m_old - m)
    b = jnp.where(low, b[:TL], b[TL:])
    return m, l_n + b * l_old, acc_n + b * acc_old


def _dil_attn_kernel(q_ref, kc_ref, kp_ref, kx_ref, vc_ref, vp_ref, vx_ref,
                     knc_ref, knp_ref, knx_ref, vnc_ref, vnp_ref, vnx_ref, band_ref, unperm_ref,
                     o_ref, acc_ref, m_ref, l_ref, kne_ref, vne_ref):
    st = pl.program_id(1)
    first, last = st == 0, st == pl.num_programs(1) - 1
    low, hmask = _head_masks()
    col = lax.broadcasted_iota(jnp.int32, (1, TK), 1)
    chunk = lambda c: slice(c * LANES, (c + 1) * LANES)

    def stacked_bias(band, col_idx=None, lo=0, hi=TK):
        if col_idx is not None:
            band = band + jnp.where((col_idx < lo) | (col_idx >= hi), NEG_INF, 0.0)
        return jnp.concatenate([band, band], axis=0)

    bias16 = stacked_bias(band_ref[0], col, jnp.where(first, BAND_HALF, 0),
                          jnp.where(last, TK - BAND_HALF, TK))

    def pipelined(tiles, score_fn, finish_fn):
        s_next = score_fn(tiles[0])
        for i, t in enumerate(tiles):
            s = s_next
            if i + 1 < len(tiles):
                s_next = score_fn(tiles[i + 1])
            finish_fn(t, s)

    def body16(g, carry):
        def score(r):
            kw = jnp.concatenate([kp_ref[0, r], kc_ref[0, r], kx_ref[0, r]], axis=0)
            return _scores(q_ref[0, r].astype(BF16), kw, bias16, hmask)

        def finish(r, s):
            vw = jnp.concatenate([vp_ref[0, r], vc_ref[0, r], vx_ref[0, r]], axis=0)
            for c, (m, l, acc) in enumerate(_softmax_pv(s, vw, low, hmask)):
                acc_ref[r, :, chunk(c)] = acc
                l_ref[c, r] = l
                for e in range(2):
                    m_ref[c, e, r] = jnp.broadcast_to(m[e * TL:(e + 1) * TL], (TL, LANES))

        pipelined([g * TILE_GROUP + i for i in range(TILE_GROUP)], score, finish)
        return carry

    lax.fori_loop(0, PLANES // TILE_GROUP, body16, 0)

    def mid_window(cur, prev, nxt, plane, lb):
        lo = lb * MID_ROWS - MID_LEAD
        if lo < 0:
            return jnp.concatenate([prev[0, plane, TL // 2 + lo:TL // 2],
                                    cur[0, plane, 0:lo + MID_KEYS]], axis=0)
        if lo + MID_KEYS > TL:
            return jnp.concatenate([cur[0, plane, lo:TL],
                                    nxt[0, plane, 0:lo + MID_KEYS - TL]], axis=0)
        return cur[0, plane, lo:lo + MID_KEYS]

    def body4(g, carry):
        block_rows = lambda lb: slice(lb * MID_ROWS, (lb + 1) * MID_ROWS)
        plane_set = lambda r4: [r4 + MID_DIL * j for j in range(MID_DIL)]

        def score(tile):
            r4, lb = tile
            planes = plane_set(r4)
            band = band_ref[1]
            if lb == 0:
                bias = stacked_bias(band, col % MID_KEYS, jnp.where(first, MID_LEAD, 0), MID_KEYS)
            elif lb == TL // MID_ROWS - 1:
                bias = stacked_bias(band, col % MID_KEYS, 0,
                                    jnp.where(last, MID_KEYS - MID_LEAD, MID_KEYS))
            else:
                bias = stacked_bias(band)
            q = jnp.concatenate([q_ref[0, p, block_rows(lb)] for p in planes],
                                axis=0).astype(BF16)
            kw = jnp.concatenate([mid_window(kc_ref, kp_ref, kx_ref, p, lb) for p in planes],
                                 axis=0)
            return _scores(q, kw, bias, hmask)

        def finish(tile, s):
            r4, lb = tile
            planes = plane_set(r4)
            rows = block_rows(lb)
            stacked = lambda ref, c: jnp.concatenate(
                [ref[c, e, p, rows] for e in range(2) for p in planes], axis=0)
            vw = jnp.concatenate([mid_window(vc_ref, vp_ref, vx_ref, p, lb) for p in planes],
                                 axis=0)
            m_old = [stacked(m_ref, c) for c in range(HW // LANES)]
            for c, new in enumerate(_softmax_pv(s, vw, low, hmask, m_old)):
                acc_old = jnp.concatenate([acc_ref[p, rows, chunk(c)] for p in planes], axis=0)
                l_old = jnp.concatenate([l_ref[c, p, rows] for p in planes], axis=0)
                m_m, l_m, acc_m = _merge(new, m_old[c], l_old, acc_old, low)
                for j, p in enumerate(planes):
                    piece = slice(j * MID_ROWS, (j + 1) * MID_ROWS)
                    acc_ref[p, rows, chunk(c)] = acc_m[piece]
                    l_ref[c, p, rows] = l_m[piece]
                    for e in range(2):
                        head_piece = slice(e * TL + j * MID_ROWS, e * TL + (j + 1) * MID_ROWS)
                        m_ref[c, e, p, rows] = m_m[head_piece]

        per_body = TILE_GROUP // (TL // MID_ROWS)
        pipelined([(g * per_body + i, lb) for i in range(per_body)
                   for lb in range(TL // MID_ROWS)], score, finish)
        return carry

    lax.fori_loop(0, MID_DIL * (TL // MID_ROWS) // TILE_GROUP, body4, 0)

    n_tiles = SUPER // TL
    for ext_ref, prev, cur, nxt in ((kne_ref, knp_ref, knc_ref, knx_ref),
                                    (vne_ref, vnp_ref, vnc_ref, vnx_ref)):
        ext_ref[0:BAND_HALF] = prev[0]
        ext_ref[BAND_HALF:BAND_HALF + SUPER] = cur[0]
        ext_ref[BAND_HALF + SUPER:] = nxt[0]
    band1 = stacked_bias(band_ref[2])

    def body1(g, carry):
        tiles = [g * TILE_GROUP + i for i in range(TILE_GROUP)]
        tile_rows = lambda t: pl.ds(pl.multiple_of(t * ONE_ROWS, ONE_ROWS), ONE_ROWS)
        tile_keys = lambda t: pl.ds(pl.multiple_of(t * TL, TL), TK)

        def tile_scores(t):
            lo = jnp.where(first & (t == 0), BAND_HALF, 0)
            hi = jnp.where(last & (t == n_tiles - 1), TK - BAND_HALF, TK)
            bias = band1 + jnp.where((col < lo) | (col >= hi), NEG_INF, 0.0)
            q = q_ref[0, :, tile_rows(t), :].reshape(TL, HW).astype(BF16)
            return _scores(q, kne_ref[tile_keys(t), :], bias, hmask)

        def emit(t, merged):
            tok = jnp.dot(unperm_ref[...], merged, preferred_element_type=F32)
            o_ref[0, pl.ds(pl.multiple_of(t * TL, TL), TL), :] = tok.astype(BF16)

        pending = []

        def finish(t, s):
            rows = tile_rows(t)
            stacked = lambda ref, c: jnp.concatenate(
                [ref[c, e, :, rows, :].reshape(TL, LANES) for e in range(2)], axis=0)
            m_old = [stacked(m_ref, c) for c in range(HW // LANES)]
            outs = []
            for c, new in enumerate(_softmax_pv(s, vne_ref[tile_keys(t), :], low, hmask, m_old)):
                acc_old = acc_ref[:, rows, chunk(c)].reshape(TL, LANES)
                l_old = l_ref[c, :, rows, :].reshape(TL, LANES)
                _, l_m, acc_m = _merge(new, m_old[c], l_old, acc_old, low)
                outs.append((acc_m / l_m).astype(BF16))
            if pending:
                emit(*pending.pop())
            pending.append((t, jnp.concatenate(outs, axis=1)))

        pipelined(tiles, tile_scores, finish)
        emit(*pending.pop())
        return carry

    lax.fori_loop(0, n_tiles // TILE_GROUP, body1, 0)


def _dil_attn(q16, k16, v16, kn, vn, band, unperm):
    B, S, _ = kn.shape
    n_half = S // PLANES // BAND_HALF
    n_tok_half = S // BAND_HALF
    per = TL // BAND_HALF
    per_tok = SUPER // BAND_HALF
    pcur = pl.BlockSpec((1, PLANES, TL, HW), lambda b, s, hh: (b, 0, s, hh))
    pprev = pl.BlockSpec((1, PLANES, BAND_HALF, HW),
                         lambda b, s, hh: (b, 0, jnp.maximum(s * per - 1, 0), hh))
    pnext = pl.BlockSpec((1, PLANES, BAND_HALF, HW),
                         lambda b, s, hh: (b, 0, jnp.minimum((s + 1) * per, n_half - 1), hh))
    tcur = pl.BlockSpec((1, SUPER, HW), lambda b, s, hh: (b, s, hh))
    tprev = pl.BlockSpec((1, BAND_HALF, HW),
                         lambda b, s, hh: (b, jnp.maximum(s * per_tok - 1, 0), hh))
    tnext = pl.BlockSpec((1, BAND_HALF, HW),
                         lambda b, s, hh: (b, jnp.minimum((s + 1) * per_tok, n_tok_half - 1), hh))
    return pl.pallas_call(
        _dil_attn_kernel,
        grid=(B, S // SUPER, ATTN_W // HW),
        in_specs=[pcur, pcur, pprev, pnext, pcur, pprev, pnext,
                  tcur, tprev, tnext, tcur, tprev, tnext,
                  _const_spec((3, TL, TK)), _const_spec((TL, TL))],
        out_specs=tcur,
        out_shape=jax.ShapeDtypeStruct((B, S, ATTN_W), BF16),
        scratch_shapes=[pltpu.VMEM((PLANES, TL, HW), F32),
                        pltpu.VMEM((HW // LANES, 2, PLANES, TL, LANES), F32),
                        pltpu.VMEM((HW // LANES, PLANES, TL, LANES), F32),
                        pltpu.VMEM((SUPER + 2 * BAND_HALF, HW), BF16),
                        pltpu.VMEM((SUPER + 2 * BAND_HALF, HW), BF16)],
        compiler_params=_cparams(3),
        name="dil_attn",
    )(q16, k16, k16, k16, v16, v16, v16, kn, kn, kn, vn, vn, vn, band, unperm)


def _halo_specs(width, n_tiles, rows, tile=TM):
    per = tile // rows
    last = n_tiles * per - 1
    prev = pl.BlockSpec((1, rows, width), lambda b, t: (b, jnp.maximum(t * per - 1, 0), 0))
    nxt = pl.BlockSpec((1, rows, width), lambda b, t: (b, jnp.minimum((t + 1) * per, last), 0))
    return prev, nxt


def _conv_rolls(buf):
    rows = buf.shape[0]
    return [buf if shift == 0 else pltpu.roll(buf, rows - shift, 0) for shift in range(SUBLANES)]


def _conv_rows(rolled, lo_row, n_rows, w_ref, b_ref, g_ref, beta_ref):
    acc = jnp.zeros((n_rows, CONV_W), F32) + b_ref[...]
    base = HALO - CONV_K // 2
    for tap in range(CONV_K):
        shift = (base + tap) % SUBLANES
        lo = lo_row + base + tap - shift
        acc = acc + w_ref[tap:tap + 1, :] * rolled[shift][lo:lo + n_rows]
    mu = jnp.mean(acc, axis=-1, keepdims=True)
    d = acc - mu
    var = jnp.mean(d * d, axis=-1, keepdims=True)
    z = d * lax.rsqrt(var + NORM_EPS) * g_ref[...] + beta_ref[...]
    return z * _sigmoid(z)


N_CAST = 11
UP_CAST = 2 * D_FF // N_CAST
DOWN_CAST = D_FF // N_CAST
OUT_CAST = LANES


def _out_ffn_kernel(a_ref, ap_ref, ax_ref, c_ref, cp_ref, cx_ref, m_ref, mp_ref, mx_ref,
                    x_ref, xp_ref, xx_ref, wo32_ref, g_ref, wu32_ref, dw_ref, db_ref, wd32_ref,
                    o_ref, gate_ref, wo_ref, wu_ref, wd_ref, *, n_tiles):
    step = pl.program_id(0)

    for c in range(N_CAST):
        @pl.when(step == c)
        def _(c=c):
            for half in range(UP_CAST // FF_CH):
                src = c * UP_CAST + half * FF_CH
                j, is_up = (src // FF_CH, 0) if src < D_FF else ((src - D_FF) // FF_CH, 1)
                dst = (2 * j + is_up) * FF_CH
                wu_ref[:, dst:dst + FF_CH] = wu32_ref[:, half * FF_CH:(half + 1) * FF_CH].astype(BF16)
            wd_ref[c * DOWN_CAST:(c + 1) * DOWN_CAST, :] = wd32_ref[...].astype(BF16)
            if (c + 1) * OUT_CAST <= D_MODEL:
                wo_ref[c * OUT_CAST:(c + 1) * OUT_CAST, :] = wo32_ref[...].astype(BF16)

    @pl.when(step >= N_CAST)
    def _():
        _out_ffn_tile(a_ref, ap_ref, ax_ref, c_ref, cp_ref, cx_ref, m_ref, mp_ref, mx_ref,
                      x_ref, xp_ref, xx_ref, wo_ref, g_ref, wu_ref, dw_ref, db_ref, wd_ref,
                      o_ref, gate_ref, lax.rem(step - N_CAST, n_tiles), n_tiles)


def _out_ffn_tile(a_ref, ap_ref, ax_ref, c_ref, cp_ref, cx_ref, m_ref, mp_ref, mx_ref,
                  x_ref, xp_ref, xx_ref, wo_ref, g_ref, wu_ref, dw_ref, db_ref, wd_ref,
                  o_ref, gate_ref, t, nt):
    has_next, has_prev = t < nt - 1, t > 0
    rows = TF + HALO

    def ext_rows(cur, nxt, prev):
        halo = jnp.concatenate([nxt[0, :SUBLANES].astype(F32), prev[0, SUBLANES:].astype(F32)],
                               axis=0)
        return jnp.concatenate([cur[0], halo.astype(cur.dtype)], axis=0)

    rid = lax.broadcasted_iota(jnp.int32, (rows, 1), 0)
    in_seq = ((rid < TF) | ((rid < TF + SUBLANES) & has_next)
              | ((rid >= TF + SUBLANES) & has_prev))
    mixed = jnp.concatenate([ext_rows(a_ref, ax_ref, ap_ref), ext_rows(c_ref, cx_ref, cp_ref),
                             ext_rows(m_ref, mx_ref, mp_ref)], axis=-1)
    mixed = jnp.where(in_seq, mixed, jnp.zeros_like(mixed))
    h = (jnp.where(in_seq, ext_rows(x_ref, xx_ref, xp_ref), 0.0)
         + jnp.dot(mixed, wo_ref[...], preferred_element_type=F32))
    ext = _rms_rows(h, g_ref[...]).astype(BF16)

    def conv3(f, lo):
        w = dw_ref[:, lo:lo + FF_CH]
        y = (w[0:1] * pltpu.roll(f, 1, 0) + w[1:2] * f + w[2:3] * pltpu.roll(f, rows - 1, 0))
        return y[:TF] + db_ref[:, lo:lo + FF_CH]

    n_chunks = D_FF // FF_CH
    out = h[:TF]
    for j in range(n_chunks):
        lo_g, lo_u = j * FF_CH, D_FF + j * FF_CH
        f = jnp.dot(ext, wu_ref[:, 2 * lo_g:2 * lo_g + 2 * FF_CH], preferred_element_type=F32)
        if j == n_chunks - 1:
            out = out + jnp.dot(gate_ref[:, :lo_g], wd_ref[:lo_g, :], preferred_element_type=F32)
        fg, fu = conv3(f[:, :FF_CH], lo_g), conv3(f[:, FF_CH:], lo_u)
        gate_ref[:, lo_g:lo_g + FF_CH] = (fg * _sigmoid(fg) * fu).astype(BF16)
    o_ref[0] = out + jnp.dot(gate_ref[:, D_FF - FF_CH:], wd_ref[D_FF - FF_CH:, :],
                             preferred_element_type=F32)


def _out_ffn(attn, cb, mo, x, w_out, g, w_up, dw_w, dw_b, w_down):
    B, S, _ = x.shape
    nt = S // TF
    per = TF // HALO

    def tile_of(step):
        i = jnp.maximum(step - N_CAST, 0)
        return i // nt, i % nt

    def with_halos(width):
        def cur(step):
            b, t = tile_of(step)
            return b, t, 0

        def prev(step):
            b, t = tile_of(step)
            return b, jnp.maximum(t * per - 1, 0), 0

        def nxt(step):
            b, t = tile_of(step)
            return b, jnp.minimum((t + 1) * per, nt * per - 1), 0

        return [pl.BlockSpec((1, TF, width), cur), pl.BlockSpec((1, HALO, width), prev),
                pl.BlockSpec((1, HALO, width), nxt)]

    cast_chunk = lambda limit: (lambda step: jnp.minimum(step, limit - 1))
    out_i, up_i, down_i = cast_chunk(D_MODEL // OUT_CAST), cast_chunk(N_CAST), cast_chunk(N_CAST)
    return pl.pallas_call(
        functools.partial(_out_ffn_kernel, n_tiles=nt),
        grid=(N_CAST + B * nt,),
        in_specs=with_halos(ATTN_W) + with_halos(CONV_W) + with_halos(MEM_W)
        + with_halos(D_MODEL)
        + [pl.BlockSpec((OUT_CAST, D_MODEL), lambda s: (out_i(s), 0)), _const_spec((1, D_MODEL)),
           pl.BlockSpec((D_MODEL, UP_CAST), lambda s: (0, up_i(s))),
           _const_spec((FFN_CONV_K, 2 * D_FF)), _const_spec((1, 2 * D_FF)),
           pl.BlockSpec((DOWN_CAST, D_MODEL), lambda s: (down_i(s), 0))],
        out_specs=pl.BlockSpec((1, TF, D_MODEL), lambda s: (*tile_of(s), 0)),
        out_shape=jax.ShapeDtypeStruct((B, S, D_MODEL), F32),
        scratch_shapes=[pltpu.VMEM((TF, D_FF), BF16), pltpu.VMEM((D_MODEL, D_MODEL), BF16),
                        pltpu.VMEM((D_MODEL, 2 * D_FF), BF16), pltpu.VMEM((D_FF, D_MODEL), BF16)],
        compiler_params=pltpu.CompilerParams(dimension_semantics=("arbitrary",),
                                             vmem_limit_bytes=VMEM_LIMIT),
        name="out_ffn",
    )(attn, attn, attn, cb, cb, cb, mo, mo, mo, x, x, x, w_out, g, w_up, dw_w, dw_b, w_down)


def _rope_tables():
    inv_freq = ROPE_THETA ** (-jnp.arange(0, ROT_DIM, 2, dtype=F32) / ROT_DIM)
    invf = jnp.broadcast_to(inv_freq[:, None], (SUBLANES, LANES))
    half = ROT_DIM // 2
    lane = np.arange(LANES) % HEAD_DIM
    lanes = np.zeros((SUBLANES, LANES), np.float32)
    lanes[0] = lane >= ROT_DIM
    lanes[1] = lane < half
    expand = np.zeros((LANES, 2 * LANES), np.float32)
    for l in range(LANES):
        if lane[l] < ROT_DIM:
            j = lane[l] % half
            expand[[j, half + j], l] = 1.0
            expand[[2 * half + j, 3 * half + j], LANES + l] = -1.0 if lane[l] < half else 1.0
    return invf, jnp.asarray(lanes), jnp.asarray(expand, BF16)


def _group_sum_matrix():
    idx = np.arange(MXU_DIM) // HEAD_DIM
    return jnp.asarray(idx[:, None] == idx[None, :], BF16)


def _plane_perm(n):
    out = np.arange(n)
    src = (out % (n // PLANES)) * PLANES + out // (n // PLANES)
    return np.asarray(src[:, None] == np.arange(n)[None, :], np.float32)


def kernel(x, mem, positions, mix_norm_g, mem_norm_g, w_in, w_mem_kv, q_norm_g, k_norm_g, mq_norm_g, mk_norm_g, conv_dw_w, conv_dw_b, conv_ln_g, conv_ln_b, w_out, ffn_norm_g, w_up, ffn_dw_w, ffn_dw_b, w_down):
    B, S, _ = x.shape
    depth = w_in.shape[0]
    pos_row = positions.reshape(B, 1, S)
    invf, lane_tab, expand = _rope_tables()
    gsum = _group_sum_matrix()
    perm = jnp.asarray(_plane_perm(SUB_ROWS), BF16)
    unperm = jnp.asarray(_plane_perm(TL).T, BF16)
    band = _band_tables()
    row = lambda a: a.reshape(1, -1)
    h = x
    for l in range(depth):
        kmt, vm = _mem_kv(mem, row(mem_norm_g[l]), w_mem_kv[l],
                          row(jnp.tile(mk_norm_g[l], MEM_HEADS)), gsum)
        q16, k16, v16, kn, vn, cb, mo = _in_proj(
            h, pos_row, row(mix_norm_g[l]), w_in[l],
            row(jnp.tile(q_norm_g[l], ATTN_HEADS)), row(jnp.tile(k_norm_g[l], ATTN_HEADS)),
            row(jnp.tile(mq_norm_g[l], MEM_HEADS)), gsum, invf, lane_tab, expand, perm,
            conv_dw_w[l], row(conv_dw_b[l]), row(conv_ln_g[l]), row(conv_ln_b[l]), kmt, vm)
        attn = _dil_attn(q16, k16, v16, kn, vn, band, unperm)
        h = _out_ffn(attn, cb, mo, h, w_out[l], row(ffn_norm_g[l]), w_up[l], ffn_dw_w[l],
                     row(ffn_dw_b[l]), w_down[l])
    return h
```

```python
import functools

import numpy as np
import jax
import jax.numpy as jnp
from jax import lax
from jax.experimental import pallas as pl
from jax.experimental.pallas import tpu as pltpu

F32 = jnp.float32
BF16 = jnp.bfloat16

D_MODEL = 1024
HEAD_DIM = 64
ATTN_HEADS = 8
ATTN_W = ATTN_HEADS * HEAD_DIM
CONV_W = 256
MEM_HEADS = 4
MEM_W = MEM_HEADS * HEAD_DIM
N_MEM = 256
PLANES = 16
MID_DIL = 4
BAND_HALF = 64
ROPE_THETA = 500000.0
ROT_DIM = HEAD_DIM // 4
CONV_K = 31
FFN_CONV_K = 3
D_FF = 2816
NORM_EPS = 1e-6
NEG_INF = -1e30
SM_SCALE = HEAD_DIM ** -0.5

LANES = 128
SUBLANES = 8
MXU_DIM = 256
BF16_ROWS = 16
VMEM_LIMIT = 56 * 1024 * 1024

TM = 512
CONV_ROWS = 64
TF = 512
TL = 128
TK = TL + 2 * BAND_HALF
SUPER = PLANES * TL
FF_CH = MXU_DIM
HALO = BF16_ROWS


def _cparams(n_axes):
    return pltpu.CompilerParams(dimension_semantics=("parallel",) * n_axes,
                                vmem_limit_bytes=VMEM_LIMIT)


def _const_spec(shape):
    return pl.BlockSpec(shape, lambda *_: (0,) * len(shape), pipeline_mode=pl.Buffered(1))


def _rms_rows(x, g):
    r = lax.rsqrt(jnp.mean(x * x, axis=-1, keepdims=True) + NORM_EPS)
    return x * r * g


def _head_sumsq(t, gsum):
    return [jnp.dot((tc * tc).astype(BF16), gsum, preferred_element_type=F32)
            for tc in (t[:, c * MXU_DIM:(c + 1) * MXU_DIM] for c in range(t.shape[1] // MXU_DIM))]


def _head_scale(t, sumsq, gain):
    outs = []
    for c, ssum in enumerate(sumsq):
        sl = slice(c * MXU_DIM, (c + 1) * MXU_DIM)
        outs.append(t[:, sl] * lax.rsqrt(ssum * (1.0 / HEAD_DIM) + NORM_EPS) * gain[:, sl])
    return outs


def _head_norm(t, gain, gsum):
    return _head_scale(t, _head_sumsq(t, gsum), gain)


def _sigmoid(x):
    return 1.0 / (1.0 + jnp.exp(-x))


IN_COLS = 3 * ATTN_W + 2 * CONV_W + MEM_W
IN_CAST = MXU_DIM
N_CAST_IN = IN_COLS // IN_CAST


def _in_proj_kernel(x_ref, xp_ref, xx_ref, pos_ref, g_ref, w32_ref, gq_ref, gk_ref, gm_ref,
                    gsum_ref, invf_ref, lane_ref, expand_ref, perm_ref,
                    cw_ref, cbias_ref, cg_ref, cbeta_ref, kmt_ref, vm_ref,
                    q16_ref, k16_ref, v16_ref, kn_ref, vn_ref, cb_ref, mo_ref,
                    w_ref, *, n_tiles):
    step = pl.program_id(0)

    for c in range(N_CAST_IN):
        @pl.when(step == c)
        def _(c=c):
            w_ref[:, c * IN_CAST:(c + 1) * IN_CAST] = w32_ref[...].astype(BF16)

    @pl.when(step >= N_CAST_IN)
    def _():
        _in_proj_tile(x_ref, xp_ref, xx_ref, pos_ref, g_ref, w_ref, gq_ref, gk_ref, gm_ref,
                      gsum_ref, invf_ref, lane_ref, expand_ref, perm_ref,
                      cw_ref, cbias_ref, cg_ref, cbeta_ref, kmt_ref, vm_ref,
                      q16_ref, k16_ref, v16_ref, kn_ref, vn_ref, cb_ref, mo_ref,
                      lax.rem(step - N_CAST_IN, n_tiles), n_tiles)


def _in_proj_tile(x_ref, xp_ref, xx_ref, pos_ref, g_ref, w_ref, gq_ref, gk_ref, gm_ref,
                  gsum_ref, invf_ref, lane_ref, expand_ref, perm_ref,
                  cw_ref, cbias_ref, cg_ref, cbeta_ref, kmt_ref, vm_ref,
                  q16_ref, k16_ref, v16_ref, kn_ref, vn_ref, cb_ref, mo_ref,
                  t, nt):
    has_prev, has_next = t > 0, t < nt - 1
    hn = _rms_rows(x_ref[0], g_ref[...]).astype(BF16)

    hn_halo = _rms_rows(jnp.concatenate([xp_ref[0], xx_ref[0]], axis=0), g_ref[...]).astype(BF16)
    hn_ext = jnp.concatenate([hn_halo[:HALO], hn, hn_halo[HALO:]], axis=0)
    cols = lambda lo, n: w_ref[:, lo:lo + n]
    q = jnp.dot(hn, cols(0, ATTN_W), preferred_element_type=F32)
    k = jnp.dot(hn, cols(ATTN_W, ATTN_W), preferred_element_type=F32)
    v = jnp.dot(hn, cols(2 * ATTN_W, ATTN_W), preferred_element_type=F32)
    glu = jnp.dot(hn_ext, cols(3 * ATTN_W, 2 * CONV_W), preferred_element_type=F32)
    qm = jnp.dot(hn, cols(3 * ATTN_W + 2 * CONV_W, MEM_W), preferred_element_type=F32)

    ang = invf_ref[:, 0:1] * pos_ref[0].astype(F32)

    def hi_lo(v):
        hi = v.astype(BF16).astype(F32)
        return [hi, (v - hi).astype(BF16).astype(F32)]

    tab = jnp.concatenate(hi_lo(jnp.cos(ang)) + hi_lo(jnp.sin(ang))
                          + [jnp.zeros((LANES - 4 * SUBLANES, TM), F32)], axis=0)
    cs = jnp.dot(tab.T.astype(BF16), expand_ref[...], preferred_element_type=F32)
    cosv = cs[:, :LANES] + lane_ref[0:1, :]
    sinv = cs[:, LANES:]
    first_half = lane_ref[1:2, :] > 0.5

    def rotary(chunks, scale):
        out = []
        for t in chunks:
            for s in range(MXU_DIM // LANES):
                xc = t[:, s * LANES:(s + 1) * LANES]
                partner = jnp.where(first_half, pltpu.roll(xc, LANES - ROT_DIM // 2, 1),
                                    pltpu.roll(xc, ROT_DIM // 2, 1))
                out.append((xc * cosv + partner * sinv) * scale)
        return out

    gsum = gsum_ref[...]
    q_ss, k_ss, qm_ss = _head_sumsq(q, gsum), _head_sumsq(k, gsum), _head_sumsq(qm, gsum)
    q_chunks = rotary(_head_scale(q, q_ss, gq_ref[...]), SM_SCALE)
    k_chunks = rotary(_head_scale(k, k_ss, gk_ref[...]), 1.0)
    qb = jnp.concatenate(q_chunks, axis=1).astype(BF16)
    kb = jnp.concatenate(k_chunks, axis=1).astype(BF16)
    vb = v.astype(BF16)
    kn_ref[0] = kb
    vn_ref[0] = vb
    qkv = jnp.dot(perm_ref[...], jnp.concatenate([qb, kb, vb], axis=1),
                  preferred_element_type=F32)
    rows = TM // PLANES
    for r in range(PLANES):
        blk = qkv[r * rows:(r + 1) * rows]
        q16_ref[0, r] = blk[:, 0:ATTN_W]
        k16_ref[0, r] = blk[:, ATTN_W:2 * ATTN_W].astype(BF16)
        v16_ref[0, r] = blk[:, 2 * ATTN_W:].astype(BF16)

    (qmn,) = _head_scale(qm, qm_ss, gm_ref[...])
    mo_ref[0] = _mem_attn((qmn * SM_SCALE).astype(BF16), kmt_ref, vm_ref)

    rid = lax.broadcasted_iota(jnp.int32, (TM + 2 * HALO, 1), 0)
    in_seq = ((rid >= HALO) | has_prev) & ((rid < TM + HALO) | has_next)
    cbuf = jnp.where(in_seq, glu[:, :CONV_W] * _sigmoid(glu[:, CONV_W:]), 0.0)
    rolled = _conv_rolls(cbuf)
    for lo in range(0, TM, CONV_ROWS):
        cb_ref[0, lo:lo + CONV_ROWS] = _conv_rows(rolled, lo, CONV_ROWS, cw_ref, cbias_ref,
                                                  cg_ref, cbeta_ref).astype(BF16)


def _in_proj(x, pos_row, g, w_in, gq, gk, gm, gsum, invf, lane_tab, expand, perm,
             conv_w, conv_b, conv_g, conv_beta, kmt, vm):
    B, S, _ = x.shape
    nt = S // TM
    per = TM // HALO

    def tile_of(step):
        i = jnp.maximum(step - N_CAST_IN, 0)
        return i // nt, i % nt

    def at_tile(fn):
        return lambda step: fn(*tile_of(step))

    tok = lambda w: pl.BlockSpec((1, TM, w), at_tile(lambda b, t: (b, t, 0)))
    xprev = pl.BlockSpec((1, HALO, D_MODEL),
                         at_tile(lambda b, t: (b, jnp.maximum(t * per - 1, 0), 0)))
    xnext = pl.BlockSpec((1, HALO, D_MODEL),
                         at_tile(lambda b, t: (b, jnp.minimum((t + 1) * per, nt * per - 1), 0)))
    plane = pl.BlockSpec((1, PLANES, TM // PLANES, ATTN_W), at_tile(lambda b, t: (b, 0, t, 0)))
    plane_shape = (B, PLANES, S // PLANES, ATTN_W)
    per_batch = lambda r, w: pl.BlockSpec((1, r, w), at_tile(lambda b, t: (b, 0, 0)))
    return pl.pallas_call(
        functools.partial(_in_proj_kernel, n_tiles=nt),
        grid=(N_CAST_IN + B * nt,),
        in_specs=[tok(D_MODEL), xprev, xnext,
                  pl.BlockSpec((1, 1, TM), at_tile(lambda b, t: (b, 0, t))),
                  _const_spec((1, D_MODEL)),
                  pl.BlockSpec((D_MODEL, IN_CAST),
                               lambda step: (0, jnp.minimum(step, N_CAST_IN - 1))),
                  _const_spec((1, ATTN_W)), _const_spec((1, ATTN_W)), _const_spec((1, MEM_W)),
                  _const_spec((MXU_DIM, MXU_DIM)), _const_spec((SUBLANES, LANES)),
                  _const_spec((SUBLANES, LANES)), _const_spec((LANES, 2 * LANES)),
                  _const_spec((TM, TM)),
                  _const_spec((CONV_K, CONV_W)), _const_spec((1, CONV_W)),
                  _const_spec((1, CONV_W)), _const_spec((1, CONV_W)),
                  per_batch(MEM_W, N_MEM), per_batch(N_MEM, MEM_W)],
        out_specs=[plane, plane, plane, tok(ATTN_W), tok(ATTN_W), tok(CONV_W), tok(MEM_W)],
        out_shape=[jax.ShapeDtypeStruct(plane_shape, F32),
                   jax.ShapeDtypeStruct(plane_shape, BF16),
                   jax.ShapeDtypeStruct(plane_shape, BF16),
                   jax.ShapeDtypeStruct((B, S, ATTN_W), BF16),
                   jax.ShapeDtypeStruct((B, S, ATTN_W), BF16),
                   jax.ShapeDtypeStruct((B, S, CONV_W), BF16),
                   jax.ShapeDtypeStruct((B, S, MEM_W), BF16)],
        scratch_shapes=[pltpu.VMEM((D_MODEL, IN_COLS), BF16)],
        compiler_params=pltpu.CompilerParams(dimension_semantics=("arbitrary",),
                                             vmem_limit_bytes=VMEM_LIMIT),
        name="in_proj",
    )(x, x, x, pos_row, g, w_in, gq, gk, gm, gsum, invf, lane_tab, expand, perm,
      conv_w, conv_b, conv_g, conv_beta, kmt, vm)


def _mem_kv_kernel(mem_ref, g_ref, w_ref, gk_ref, gsum_ref, kmt_ref, vm_ref):
    mn = _rms_rows(mem_ref[0], g_ref[...]).astype(BF16)
    kv = jnp.dot(mn, w_ref[...].astype(BF16), preferred_element_type=F32)
    (km,) = _head_norm(kv[:, :MEM_W], gk_ref[...], gsum_ref[...])
    kmt_ref[0] = km.T.astype(BF16)
    vm_ref[0] = kv[:, MEM_W:].astype(BF16)


def _mem_kv(mem, g, w, gk, gsum):
    B = mem.shape[0]
    return pl.pallas_call(
        _mem_kv_kernel,
        grid=(B,),
        in_specs=[pl.BlockSpec((1, N_MEM, D_MODEL), lambda b: (b, 0, 0)),
                  _const_spec((1, D_MODEL)), _const_spec((D_MODEL, 2 * MEM_W)),
                  _const_spec((1, MEM_W)), _const_spec((MXU_DIM, MXU_DIM))],
        out_specs=[pl.BlockSpec((1, MEM_W, N_MEM), lambda b: (b, 0, 0)),
                   pl.BlockSpec((1, N_MEM, MEM_W), lambda b: (b, 0, 0))],
        out_shape=[jax.ShapeDtypeStruct((B, MEM_W, N_MEM), BF16),
                   jax.ShapeDtypeStruct((B, N_MEM, MEM_W), BF16)],
        compiler_params=_cparams(1),
        name="mem_kv",
    )(mem, g, w, gk, gsum)


def _head_masks():
    lane = lax.broadcasted_iota(jnp.int32, (1, LANES), 1)
    low = lane < HEAD_DIM
    return low, (jnp.where(low, 1.0, 0.0).astype(BF16), jnp.where(low, 0.0, 1.0).astype(BF16))


def _mem_attn(qm, kmt_ref, vm_ref):
    low, hmask = _head_masks()
    out = []
    for c in range(MEM_W // LANES):
        sl = slice(c * LANES, (c + 1) * LANES)
        qc = qm[:, sl]
        halves = []
        for e in range(2):
            s = jnp.dot(qc * hmask[e], kmt_ref[0, sl, :], preferred_element_type=F32)
            m = jnp.max(s, axis=-1, keepdims=True)
            p = jnp.exp(s - m)
            l = jnp.sum(p, axis=-1, keepdims=True)
            o = jnp.dot(p.astype(BF16), vm_ref[0, :, sl], preferred_element_type=F32)
            halves.append(o / l)
        out.append(jnp.where(low, halves[0], halves[1]).astype(BF16))
    return jnp.concatenate(out, axis=1)


HW = ATTN_W // 2
MID_ROWS = TL // MID_DIL
MID_KEYS = TK // MID_DIL
MID_LEAD = (MID_KEYS - MID_ROWS) // 2
ONE_ROWS = TL // PLANES
TILE_GROUP = 16


def _band_tables():
    rho = np.arange(TL)[:, None]
    kap = np.arange(TK)[None, :]
    d16 = kap - BAND_HALF - rho
    j, lq = rho // MID_ROWS, rho % MID_ROWS
    jk, lk = kap // MID_KEYS, kap % MID_KEYS
    d4 = MID_DIL * (lk - MID_LEAD - lq) + (jk - j)
    r, l1 = rho // ONE_ROWS, rho % ONE_ROWS
    d1 = kap - BAND_HALF - PLANES * l1 - r
    tabs = [np.where(np.abs(d) <= BAND_HALF, 0.0, NEG_INF) for d in (d16, d4, d1)]
    return jnp.asarray(np.stack(tabs), F32)


ALL_CHUNKS = tuple(range(HW // LANES))


def _scores(q, kw, bias2, hmask, chunks=ALL_CHUNKS):
    out = []
    for c in chunks:
        sl = slice(c * LANES, (c + 1) * LANES)
        qc = q[:, sl]
        qs = jnp.concatenate([qc * hmask[0], qc * hmask[1]], axis=0)
        out.append(lax.dot_general(qs, kw[:, sl], (((1,), (1,)), ((), ())),
                                   preferred_element_type=F32) + bias2)
    return out


def _softmax_pv(scores, vw, low, hmask, m_old=None, chunks=ALL_CHUNKS):
    ones = jnp.ones((TK, LANES), BF16)
    res = []
    for c, s in zip(chunks, scores):
        m = jnp.max(s, axis=-1, keepdims=True)
        if m_old is None:
            shift = m
        else:
            m = jnp.maximum(m, m_old[c])
            shift = jnp.concatenate([m] * (TK // LANES), axis=1)
        p = jnp.exp((s - shift).astype(BF16))
        v_aug = jnp.concatenate([vw[:, c * LANES:(c + 1) * LANES], ones], axis=1)
        r = jnp.dot(p, v_aug, preferred_element_type=F32)
        res.append((m, jnp.where(low, r[:TL, LANES:], r[TL:, LANES:]),
                    jnp.where(low, r[:TL, :LANES], r[TL:, :LANES])))
    return res


def _attend(q, kw, vw, bias2, low, hmask, m_old=None):
    res = []
    for c in ALL_CHUNKS:
        res += _softmax_pv(_scores(q, kw, bias2, hmask, (c,)), vw, low, hmask, m_old, (c,))
    return res


def _merge(new, m_old, l_old, acc_old, low):
    m, l_n, acc_n = new
    b = jnp.exp(m_old - m)
    b = jnp.where(low, b[:TL], b[TL:])
    return m, l_n + b * l_old, acc_n + b * acc_old


def _dil_attn_kernel(q_ref, kc_ref, kp_ref, kx_ref, vc_ref, vp_ref, vx_ref,
                     knc_ref, knp_ref, knx_ref, vnc_ref, vnp_ref, vnx_ref, band_ref, unperm_ref,
                     o_ref, acc_ref, m_ref, l_ref, kne_ref, vne_ref):
    st = pl.program_id(1)
    first, last = st == 0, st == pl.num_programs(1) - 1
    low, hmask = _head_masks()
    col = lax.broadcasted_iota(jnp.int32, (1, TK), 1)
    chunk = lambda c: slice(c * LANES, (c + 1) * LANES)

    def stacked_bias(band, col_idx=None, lo=0, hi=TK):
        if col_idx is not None:
            band = band + jnp.where((col_idx < lo) | (col_idx >= hi), NEG_INF, 0.0)
        return jnp.concatenate([band, band], axis=0)

    bias16 = stacked_bias(band_ref[0], col, jnp.where(first, BAND_HALF, 0),
                          jnp.where(last, TK - BAND_HALF, TK))

    def pipelined(tiles, score_fn, finish_fn):
        s_next = score_fn(tiles[0])
        for i, t in enumerate(tiles):
            s = s_next
            if i + 1 < len(tiles):
                s_next = score_fn(tiles[i + 1])
            finish_fn(t, s)

    def body16(g, carry):
        def score(r):
            kw = jnp.concatenate([kp_ref[0, r], kc_ref[0, r], kx_ref[0, r]], axis=0)
            return _scores(q_ref[0, r].astype(BF16), kw, bias16, hmask)

        def finish(r, s):
            vw = jnp.concatenate([vp_ref[0, r], vc_ref[0, r], vx_ref[0, r]], axis=0)
            for c, (m, l, acc) in enumerate(_softmax_pv(s, vw, low, hmask)):
                acc_ref[r, :, chunk(c)] = acc
                l_ref[c, r] = l
                for e in range(2):
                    m_ref[c, e, r] = jnp.broadcast_to(m[e * TL:(e + 1) * TL], (TL, LANES))

        pipelined([g * TILE_GROUP + i for i in range(TILE_GROUP)], score, finish)
        return carry

    lax.fori_loop(0, PLANES // TILE_GROUP, body16, 0)

    def mid_window(cur, prev, nxt, plane, lb):
        lo = lb * MID_ROWS - MID_LEAD
        if lo < 0:
            return jnp.concatenate([prev[0, plane, TL // 2 + lo:TL // 2],
                                    cur[0, plane, 0:lo + MID_KEYS]], axis=0)
        if lo + MID_KEYS > TL:
            return jnp.concatenate([cur[0, plane, lo:TL],
                                    nxt[0, plane, 0:lo + MID_KEYS - TL]], axis=0)
        return cur[0, plane, lo:lo + MID_KEYS]

    def body4(g, carry):
        block_rows = lambda lb: slice(lb * MID_ROWS, (lb + 1) * MID_ROWS)
        plane_set = lambda r4: [r4 + MID_DIL * j for j in range(MID_DIL)]

        def score(tile):
            r4, lb = tile
            planes = plane_set(r4)
            band = band_ref[1]
            if lb == 0:
                bias = stacked_bias(band, col % MID_KEYS, jnp.where(first, MID_LEAD, 0), MID_KEYS)
            elif lb == TL // MID_ROWS - 1:
                bias = stacked_bias(band, col % MID_KEYS, 0,
                                    jnp.where(last, MID_KEYS - MID_LEAD, MID_KEYS))
            else:
                bias = stacked_bias(band)
            q = jnp.concatenate([q_ref[0, p, block_rows(lb)] for p in planes],
                                axis=0).astype(BF16)
            kw = jnp.concatenate([mid_window(kc_ref, kp_ref, kx_ref, p, lb) for p in planes],
                                 axis=0)
            return _scores(q, kw, bias, hmask)

        def finish(tile, s):
            r4, lb = tile
            planes = plane_set(r4)
            rows = block_rows(lb)
            stacked = lambda ref, c: jnp.concatenate(
                [ref[c, e, p, rows] for e in range(2) for p in planes], axis=0)
            vw = jnp.concatenate([mid_window(vc_ref, vp_ref, vx_ref, p, lb) for p in planes],
                                 axis=0)
            m_old = [stacked(m_ref, c) for c in range(HW // LANES)]
            for c, new in enumerate(_softmax_pv(s, vw, low, hmask, m_old)):
                acc_old = jnp.concatenate([acc_ref[p, rows, chunk(c)] for p in planes], axis=0)
                l_old = jnp.concatenate([l_ref[c, p, rows] for p in planes], axis=0)
                m_m, l_m, acc_m = _merge(new, m_old[c], l_old, acc_old, low)
                for j, p in enumerate(planes):
                    piece = slice(j * MID_ROWS, (j + 1) * MID_ROWS)
                    acc_ref[p, rows, chunk(c)] = acc_m[piece]
                    l_ref[c, p, rows] = l_m[piece]
                    for e in range(2):
                        head_piece = slice(e * TL + j * MID_ROWS, e * TL + (j + 1) * MID_ROWS)
                        m_ref[c, e, p, rows] = m_m[head_piece]

        per_body = TILE_GROUP // (TL // MID_ROWS)
        pipelined([(g * per_body + i, lb) for i in range(per_body)
                   for lb in range(TL // MID_ROWS)], score, finish)
        return carry

    lax.fori_loop(0, MID_DIL * (TL // MID_ROWS) // TILE_GROUP, body4, 0)

    n_tiles = SUPER // TL
    for ext_ref, prev, cur, nxt in ((kne_ref, knp_ref, knc_ref, knx_ref),
                                    (vne_ref, vnp_ref, vnc_ref, vnx_ref)):
        ext_ref[0:BAND_HALF] = prev[0]
        ext_ref[BAND_HALF:BAND_HALF + SUPER] = cur[0]
        ext_ref[BAND_HALF + SUPER:] = nxt[0]
    band1 = stacked_bias(band_ref[2])

    def body1(g, carry):
        tiles = [g * TILE_GROUP + i for i in range(TILE_GROUP)]
        tile_rows = lambda t: pl.ds(pl.multiple_of(t * ONE_ROWS, ONE_ROWS), ONE_ROWS)
        tile_keys = lambda t: pl.ds(pl.multiple_of(t * TL, TL), TK)

        def tile_scores(t):
            lo = jnp.where(first & (t == 0), BAND_HALF, 0)
            hi = jnp.where(last & (t == n_tiles - 1), TK - BAND_HALF, TK)
            bias = band1 + jnp.where((col < lo) | (col >= hi), NEG_INF, 0.0)
            q = q_ref[0, :, tile_rows(t), :].reshape(TL, HW).astype(BF16)
            return _scores(q, kne_ref[tile_keys(t), :], bias, hmask)

        def emit(t, merged):
            tok = jnp.dot(unperm_ref[...], merged, preferred_element_type=F32)
            o_ref[0, pl.ds(pl.multiple_of(t * TL, TL), TL), :] = tok.astype(BF16)

        pending = []

        def finish(t, s):
            rows = tile_rows(t)
            stacked = lambda ref, c: jnp.concatenate(
                [ref[c, e, :, rows, :].reshape(TL, LANES) for e in range(2)], axis=0)
            m_old = [stacked(m_ref, c) for c in range(HW // LANES)]
            outs = []
            for c, new in enumerate(_softmax_pv(s, vne_ref[tile_keys(t), :], low, hmask, m_old)):
                acc_old = acc_ref[:, rows, chunk(c)].reshape(TL, LANES)
                l_old = l_ref[c, :, rows, :].reshape(TL, LANES)
                _, l_m, acc_m = _merge(new, m_old[c], l_old, acc_old, low)
                outs.append((acc_m / l_m).astype(BF16))
            if pending:
                emit(*pending.pop())
            pending.append((t, jnp.concatenate(outs, axis=1)))

        pipelined(tiles, tile_scores, finish)
        emit(*pending.pop())
        return carry

    lax.fori_loop(0, n_tiles // TILE_GROUP, body1, 0)


def _dil_attn(q16, k16, v16, kn, vn, band, unperm):
    B, S, _ = kn.shape
    n_half = S // PLANES // BAND_HALF
    n_tok_half = S // BAND_HALF
    per = TL // BAND_HALF
    per_tok = SUPER // BAND_HALF
    pcur = pl.BlockSpec((1, PLANES, TL, HW), lambda b, s, hh: (b, 0, s, hh))
    pprev = pl.BlockSpec((1, PLANES, BAND_HALF, HW),
                         lambda b, s, hh: (b, 0, jnp.maximum(s * per - 1, 0), hh))
    pnext = pl.BlockSpec((1, PLANES, BAND_HALF, HW),
                         lambda b, s, hh: (b, 0, jnp.minimum((s + 1) * per, n_half - 1), hh))
    tcur = pl.BlockSpec((1, SUPER, HW), lambda b, s, hh: (b, s, hh))
    tprev = pl.BlockSpec((1, BAND_HALF, HW),
                         lambda b, s, hh: (b, jnp.maximum(s * per_tok - 1, 0), hh))
    tnext = pl.BlockSpec((1, BAND_HALF, HW),
                         lambda b, s, hh: (b, jnp.minimum((s + 1) * per_tok, n_tok_half - 1), hh))
    return pl.pallas_call(
        _dil_attn_kernel,
        grid=(B, S // SUPER, ATTN_W // HW),
        in_specs=[pcur, pcur, pprev, pnext, pcur, pprev, pnext,
                  tcur, tprev, tnext, tcur, tprev, tnext,
                  _const_spec((3, TL, TK)), _const_spec((TL, TL))],
        out_specs=tcur,
        out_shape=jax.ShapeDtypeStruct((B, S, ATTN_W), BF16),
        scratch_shapes=[pltpu.VMEM((PLANES, TL, HW), F32),
                        pltpu.VMEM((HW // LANES, 2, PLANES, TL, LANES), F32),
                        pltpu.VMEM((HW // LANES, PLANES, TL, LANES), F32),
                        pltpu.VMEM((SUPER + 2 * BAND_HALF, HW), BF16),
                        pltpu.VMEM((SUPER + 2 * BAND_HALF, HW), BF16)],
        compiler_params=_cparams(3),
        name="dil_attn",
    )(q16, k16, k16, k16, v16, v16, v16, kn, kn, kn, vn, vn, vn, band, unperm)


def _halo_specs(width, n_tiles, rows, tile=TM):
    per = tile // rows
    last = n_tiles * per - 1
    prev = pl.BlockSpec((1, rows, width), lambda b, t: (b, jnp.maximum(t * per - 1, 0), 0))
    nxt = pl.BlockSpec((1, rows, width), lambda b, t: (b, jnp.minimum((t + 1) * per, last), 0))
    return prev, nxt


def _conv_rolls(buf):
    rows = buf.shape[0]
    return [buf if shift == 0 else pltpu.roll(buf, rows - shift, 0) for shift in range(SUBLANES)]


def _conv_rows(rolled, lo_row, n_rows, w_ref, b_ref, g_ref, beta_ref):
    acc = jnp.zeros((n_rows, CONV_W), F32) + b_ref[...]
    base = HALO - CONV_K // 2
    for tap in range(CONV_K):
        shift = (base + tap) % SUBLANES
        lo = lo_row + base + tap - shift
        acc = acc + w_ref[tap:tap + 1, :] * rolled[shift][lo:lo + n_rows]
    mu = jnp.mean(acc, axis=-1, keepdims=True)
    d = acc - mu
    var = jnp.mean(d * d, axis=-1, keepdims=True)
    z = d * lax.rsqrt(var + NORM_EPS) * g_ref[...] + beta_ref[...]
    return z * _sigmoid(z)


N_CAST = 11
UP_CAST = 2 * D_FF // N_CAST
DOWN_CAST = D_FF // N_CAST
OUT_CAST = LANES


def _out_ffn_kernel(a_ref, ap_ref, ax_ref, c_ref, cp_ref, cx_ref, m_ref, mp_ref, mx_ref,
                    x_ref, xp_ref, xx_ref, wo32_ref, g_ref, wu32_ref, dw_ref, db_ref, wd32_ref,
                    o_ref, gate_ref, wo_ref, wu_ref, wd_ref, *, n_tiles):
    step = pl.program_id(0)

    for c in range(N_CAST):
        @pl.when(step == c)
        def _(c=c):
            for half in range(UP_CAST // FF_CH):
                src = c * UP_CAST + half * FF_CH
                j, is_up = (src // FF_CH, 0) if src < D_FF else ((src - D_FF) // FF_CH, 1)
                dst = (2 * j + is_up) * FF_CH
                wu_ref[:, dst:dst + FF_CH] = wu32_ref[:, half * FF_CH:(half + 1) * FF_CH].astype(BF16)
            wd_ref[c * DOWN_CAST:(c + 1) * DOWN_CAST, :] = wd32_ref[...].astype(BF16)
            if (c + 1) * OUT_CAST <= D_MODEL:
                wo_ref[c * OUT_CAST:(c + 1) * OUT_CAST, :] = wo32_ref[...].astype(BF16)

    @pl.when(step >= N_CAST)
    def _():
        _out_ffn_tile(a_ref, ap_ref, ax_ref, c_ref, cp_ref, cx_ref, m_ref, mp_ref, mx_ref,
                      x_ref, xp_ref, xx_ref, wo_ref, g_ref, wu_ref, dw_ref, db_ref, wd_ref,
                      o_ref, gate_ref, lax.rem(step - N_CAST, n_tiles), n_tiles)


def _out_ffn_tile(a_ref, ap_ref, ax_ref, c_ref, cp_ref, cx_ref, m_ref, mp_ref, mx_ref,
                  x_ref, xp_ref, xx_ref, wo_ref, g_ref, wu_ref, dw_ref, db_ref, wd_ref,
                  o_ref, gate_ref, t, nt):
    has_next, has_prev = t < nt - 1, t > 0
    rows = TF + HALO

    def ext_rows(cur, nxt, prev):
        halo = jnp.concatenate([nxt[0, :SUBLANES].astype(F32), prev[0, SUBLANES:].astype(F32)],
                               axis=0)
        return jnp.concatenate([cur[0], halo.astype(cur.dtype)], axis=0)

    rid = lax.broadcasted_iota(jnp.int32, (rows, 1), 0)
    in_seq = ((rid < TF) | ((rid < TF + SUBLANES) & has_next)
              | ((rid >= TF + SUBLANES) & has_prev))
    mixed = jnp.concatenate([ext_rows(a_ref, ax_ref, ap_ref), ext_rows(c_ref, cx_ref, cp_ref),
                             ext_rows(m_ref, mx_ref, mp_ref)], axis=-1)
    mixed = jnp.where(in_seq, mixed, jnp.zeros_like(mixed))
    h = (jnp.where(in_seq, ext_rows(x_ref, xx_ref, xp_ref), 0.0)
         + jnp.dot(mixed, wo_ref[...], preferred_element_type=F32))
    ext = _rms_rows(h, g_ref[...]).astype(BF16)

    def conv3(f, lo):
        w = dw_ref[:, lo:lo + FF_CH]
        y = (w[0:1] * pltpu.roll(f, 1, 0) + w[1:2] * f + w[2:3] * pltpu.roll(f, rows - 1, 0))
        return y[:TF] + db_ref[:, lo:lo + FF_CH]

    n_chunks = D_FF // FF_CH
    out = h[:TF]
    for j in range(n_chunks):
        lo_g, lo_u = j * FF_CH, D_FF + j * FF_CH
        f = jnp.dot(ext, wu_ref[:, 2 * lo_g:2 * lo_g + 2 * FF_CH], preferred_element_type=F32)
        if j == n_chunks - 1:
            out = out + jnp.dot(gate_ref[:, :lo_g], wd_ref[:lo_g, :], preferred_element_type=F32)
        fg, fu = conv3(f[:, :FF_CH], lo_g), conv3(f[:, FF_CH:], lo_u)
        gate_ref[:, lo_g:lo_g + FF_CH] = (fg * _sigmoid(fg) * fu).astype(BF16)
    o_ref[0] = out + jnp.dot(gate_ref[:, D_FF - FF_CH:], wd_ref[D_FF - FF_CH:, :],
                             preferred_element_type=F32)


def _out_ffn(attn, cb, mo, x, w_out, g, w_up, dw_w, dw_b, w_down):
    B, S, _ = x.shape
    nt = S // TF
    per = TF // HALO

    def tile_of(step):
        i = jnp.maximum(step - N_CAST, 0)
        return i // nt, i % nt

    def with_halos(width):
        def cur(step):
            b, t = tile_of(step)
            return b, t, 0

        def prev(step):
            b, t = tile_of(step)
            return b, jnp.maximum(t * per - 1, 0), 0

        def nxt(step):
            b, t = tile_of(step)
            return b, jnp.minimum((t + 1) * per, nt * per - 1), 0

        return [pl.BlockSpec((1, TF, width), cur), pl.BlockSpec((1, HALO, width), prev),
                pl.BlockSpec((1, HALO, width), nxt)]

    cast_chunk = lambda limit: (lambda step: jnp.minimum(step, limit - 1))
    out_i, up_i, down_i = cast_chunk(D_MODEL // OUT_CAST), cast_chunk(N_CAST), cast_chunk(N_CAST)
    return pl.pallas_call(
        functools.partial(_out_ffn_kernel, n_tiles=nt),
        grid=(N_CAST + B * nt,),
        in_specs=with_halos(ATTN_W) + with_halos(CONV_W) + with_halos(MEM_W)
        + with_halos(D_MODEL)
        + [pl.BlockSpec((OUT_CAST, D_MODEL), lambda s: (out_i(s), 0)), _const_spec((1, D_MODEL)),
           pl.BlockSpec((D_MODEL, UP_CAST), lambda s: (0, up_i(s))),
           _const_spec((FFN_CONV_K, 2 * D_FF)), _const_spec((1, 2 * D_FF)),
           pl.BlockSpec((DOWN_CAST, D_MODEL), lambda s: (down_i(s), 0))],
        out_specs=pl.BlockSpec((1, TF, D_MODEL), lambda s: (*tile_of(s), 0)),
        out_shape=jax.ShapeDtypeStruct((B, S, D_MODEL), F32),
        scratch_shapes=[pltpu.VMEM((TF, D_FF), BF16), pltpu.VMEM((D_MODEL, D_MODEL), BF16),
                        pltpu.VMEM((D_MODEL, 2 * D_FF), BF16), pltpu.VMEM((D_FF, D_MODEL), BF16)],
        compiler_params=pltpu.CompilerParams(dimension_semantics=("arbitrary",),
                                             vmem_limit_bytes=VMEM_LIMIT),
        name="out_ffn",
    )(attn, attn, attn, cb, cb, cb, mo, mo, mo, x, x, x, w_out, g, w_up, dw_w, dw_b, w_down)


def _rope_tables():
    inv_freq = ROPE_THETA ** (-jnp.arange(0, ROT_DIM, 2, dtype=F32) / ROT_DIM)
    invf = jnp.broadcast_to(inv_freq[:, None], (SUBLANES, LANES))
    half = ROT_DIM // 2
    lane = np.arange(LANES) % HEAD_DIM
    lanes = np.zeros((SUBLANES, LANES), np.float32)
    lanes[0] = lane >= ROT_DIM
    lanes[1] = lane < half
    expand = np.zeros((LANES, 2 * LANES), np.float32)
    for l in range(LANES):
        if lane[l] < ROT_DIM:
            j = lane[l] % half
            expand[[j, half + j], l] = 1.0
            expand[[2 * half + j, 3 * half + j], LANES + l] = -1.0 if lane[l] < half else 1.0
    return invf, jnp.asarray(lanes), jnp.asarray(expand, BF16)


def _group_sum_matrix():
    idx = np.arange(MXU_DIM) // HEAD_DIM
    return jnp.asarray(idx[:, None] == idx[None, :], BF16)


def _plane_perm(n):
    out = np.arange(n)
    src = (out % (n // PLANES)) * PLANES + out // (n // PLANES)
    return np.asarray(src[:, None] == np.arange(n)[None, :], np.float32)


def kernel(x, mem, positions, mix_norm_g, mem_norm_g, w_in, w_mem_kv, q_norm_g, k_norm_g, mq_norm_g, mk_norm_g, conv_dw_w, conv_dw_b, conv_ln_g, conv_ln_b, w_out, ffn_norm_g, w_up, ffn_dw_w, ffn_dw_b, w_down):
    B, S, _ = x.shape
    depth = w_in.shape[0]
    pos_row = positions.reshape(B, 1, S)
    invf, lane_tab, expand = _rope_tables()
    gsum = _group_sum_matrix()
    perm = jnp.asarray(_plane_perm(TM), BF16)
    unperm = jnp.asarray(_plane_perm(TL).T, BF16)
    band = _band_tables()
    row = lambda a: a.reshape(1, -1)
    h = x
    for l in range(depth):
        kmt, vm = _mem_kv(mem, row(mem_norm_g[l]), w_mem_kv[l],
                          row(jnp.tile(mk_norm_g[l], MEM_HEADS)), gsum)
        q16, k16, v16, kn, vn, cb, mo = _in_proj(
            h, pos_row, row(mix_norm_g[l]), w_in[l],
            row(jnp.tile(q_norm_g[l], ATTN_HEADS)), row(jnp.tile(k_norm_g[l], ATTN_HEADS)),
            row(jnp.tile(mq_norm_g[l], MEM_HEADS)), gsum, invf, lane_tab, expand, perm,
            conv_dw_w[l], row(conv_dw_b[l]), row(conv_ln_g[l]), row(conv_ln_b[l]), kmt, vm)
        attn = _dil_attn(q16, k16, v16, kn, vn, band, unperm)
        h = _out_ffn(attn, cb, mo, h, w_out[l], row(ffn_norm_g[l]), w_up[l], ffn_dw_w[l],
                     row(ffn_dw_b[l]), w_down[l])
    return h
```

```python
import functools

import numpy as np
import jax
import jax.numpy as jnp
from jax import lax
from jax.experimental import pallas as pl
from jax.experimental.pallas import tpu as pltpu

F32 = jnp.float32
BF16 = jnp.bfloat16

D_MODEL = 1024
HEAD_DIM = 64
ATTN_HEADS = 8
ATTN_W = ATTN_HEADS * HEAD_DIM
CONV_W = 256
MEM_HEADS = 4
MEM_W = MEM_HEADS * HEAD_DIM
N_MEM = 256
PLANES = 16
MID_DIL = 4
BAND_HALF = 64
ROPE_THETA = 500000.0
ROT_DIM = HEAD_DIM // 4
CONV_K = 31
FFN_CONV_K = 3
D_FF = 2816
NORM_EPS = 1e-6
NEG_INF = -1e30
SM_SCALE = HEAD_DIM ** -0.5

LANES = 128
SUBLANES = 8
MXU_DIM = 256
BF16_ROWS = 16
VMEM_LIMIT = 56 * 1024 * 1024

TM = 512
CONV_ROWS = 64
TF = 512
TL = 128
TK = TL + 2 * BAND_HALF
SUPER = PLANES * TL
FF_CH = MXU_DIM
HALO = BF16_ROWS


def _cparams(n_axes):
    return pltpu.CompilerParams(dimension_semantics=("parallel",) * n_axes,
                                vmem_limit_bytes=VMEM_LIMIT)


def _const_spec(shape):
    return pl.BlockSpec(shape, lambda *_: (0,) * len(shape), pipeline_mode=pl.Buffered(1))


def _rms_rows(x, g):
    r = lax.rsqrt(jnp.mean(x * x, axis=-1, keepdims=True) + NORM_EPS)
    return x * r * g


def _head_sumsq(t, gsum):
    return [jnp.dot((tc * tc).astype(BF16), gsum, preferred_element_type=F32)
            for tc in (t[:, c * MXU_DIM:(c + 1) * MXU_DIM] for c in range(t.shape[1] // MXU_DIM))]


def _head_scale(t, sumsq, gain):
    outs = []
    for c, ssum in enumerate(sumsq):
        sl = slice(c * MXU_DIM, (c + 1) * MXU_DIM)
        outs.append(t[:, sl] * lax.rsqrt(ssum * (1.0 / HEAD_DIM) + NORM_EPS) * gain[:, sl])
    return outs


def _head_norm(t, gain, gsum):
    return _head_scale(t, _head_sumsq(t, gsum), gain)


def _sigmoid(x):
    return 1.0 / (1.0 + jnp.exp(-x))


IN_COLS = 3 * ATTN_W + 2 * CONV_W + MEM_W
IN_CAST = MXU_DIM
N_CAST_IN = IN_COLS // IN_CAST


def _in_proj_kernel(x_ref, xp_ref, xx_ref, pos_ref, g_ref, w32_ref, gq_ref, gk_ref, gm_ref,
                    gsum_ref, invf_ref, lane_ref, expand_ref, perm_ref,
                    cw_ref, cbias_ref, cg_ref, cbeta_ref, kmt_ref, vm_ref,
                    q16_ref, k16_ref, v16_ref, kn_ref, vn_ref, cb_ref, mo_ref,
                    w_ref, *, n_tiles):
    step = pl.program_id(0)

    for c in range(N_CAST_IN):
        @pl.when(step == c)
        def _(c=c):
            w_ref[:, c * IN_CAST:(c + 1) * IN_CAST] = w32_ref[...].astype(BF16)

    @pl.when(step >= N_CAST_IN)
    def _():
        _in_proj_tile(x_ref, xp_ref, xx_ref, pos_ref, g_ref, w_ref, gq_ref, gk_ref, gm_ref,
                      gsum_ref, invf_ref, lane_ref, expand_ref, perm_ref,
                      cw_ref, cbias_ref, cg_ref, cbeta_ref, kmt_ref, vm_ref,
                      q16_ref, k16_ref, v16_ref, kn_ref, vn_ref, cb_ref, mo_ref,
                      lax.rem(step - N_CAST_IN, n_tiles), n_tiles)


def _in_proj_tile(x_ref, xp_ref, xx_ref, pos_ref, g_ref, w_ref, gq_ref, gk_ref, gm_ref,
                  gsum_ref, invf_ref, lane_ref, expand_ref, perm_ref,
                  cw_ref, cbias_ref, cg_ref, cbeta_ref, kmt_ref, vm_ref,
                  q16_ref, k16_ref, v16_ref, kn_ref, vn_ref, cb_ref, mo_ref,
                  t, nt):
    has_prev, has_next = t > 0, t < nt - 1
    hn = _rms_rows(x_ref[0], g_ref[...]).astype(BF16)

    hn_halo = _rms_rows(jnp.concatenate([xp_ref[0], xx_ref[0]], axis=0), g_ref[...]).astype(BF16)
    hn_ext = jnp.concatenate([hn_halo[:HALO], hn, hn_halo[HALO:]], axis=0)
    cols = lambda lo, n: w_ref[:, lo:lo + n]
    q = jnp.dot(hn, cols(0, ATTN_W), preferred_element_type=F32)
    k = jnp.dot(hn, cols(ATTN_W, ATTN_W), preferred_element_type=F32)
    v = jnp.dot(hn, cols(2 * ATTN_W, ATTN_W), preferred_element_type=F32)
    glu = jnp.dot(hn_ext, cols(3 * ATTN_W, 2 * CONV_W), preferred_element_type=F32)
    qm = jnp.dot(hn, cols(3 * ATTN_W + 2 * CONV_W, MEM_W), preferred_element_type=F32)

    ang = invf_ref[:, 0:1] * pos_ref[0].astype(F32)

    def hi_lo(v):
        hi = v.astype(BF16).astype(F32)
        return [hi, (v - hi).astype(BF16).astype(F32)]

    tab = jnp.concatenate(hi_lo(jnp.cos(ang)) + hi_lo(jnp.sin(ang))
                          + [jnp.zeros((LANES - 4 * SUBLANES, TM), F32)], axis=0)
    cs = jnp.dot(tab.T.astype(BF16), expand_ref[...], preferred_element_type=F32)
    cosv = cs[:, :LANES] + lane_ref[0:1, :]
    sinv = cs[:, LANES:]
    first_half = lane_ref[1:2, :] > 0.5

    def rotary(chunks, scale):
        out = []
        for t in chunks:
            for s in range(MXU_DIM // LANES):
                xc = t[:, s * LANES:(s + 1) * LANES]
                partner = jnp.where(first_half, pltpu.roll(xc, LANES - ROT_DIM // 2, 1),
                                    pltpu.roll(xc, ROT_DIM // 2, 1))
                out.append((xc * cosv + partner * sinv) * scale)
        return out

    gsum = gsum_ref[...]
    q_ss, k_ss, qm_ss = _head_sumsq(q, gsum), _head_sumsq(k, gsum), _head_sumsq(qm, gsum)
    q_chunks = rotary(_head_scale(q, q_ss, gq_ref[...]), SM_SCALE)
    k_chunks = rotary(_head_scale(k, k_ss, gk_ref[...]), 1.0)
    qb = jnp.concatenate(q_chunks, axis=1).astype(BF16)
    kb = jnp.concatenate(k_chunks, axis=1).astype(BF16)
    vb = v.astype(BF16)
    kn_ref[0] = kb
    vn_ref[0] = vb
    qkv = jnp.dot(perm_ref[...], jnp.concatenate([qb, kb, vb], axis=1),
                  preferred_element_type=F32)
    rows = TM // PLANES
    for r in range(PLANES):
        blk = qkv[r * rows:(r + 1) * rows]
        q16_ref[0, r] = blk[:, 0:ATTN_W]
        k16_ref[0, r] = blk[:, ATTN_W:2 * ATTN_W].astype(BF16)
        v16_ref[0, r] = blk[:, 2 * ATTN_W:].astype(BF16)

    (qmn,) = _head_scale(qm, qm_ss, gm_ref[...])
    mo_ref[0] = _mem_attn((qmn * SM_SCALE).astype(BF16), kmt_ref, vm_ref)

    rid = lax.broadcasted_iota(jnp.int32, (TM + 2 * HALO, 1), 0)
    in_seq = ((rid >= HALO) | has_prev) & ((rid < TM + HALO) | has_next)
    cbuf = jnp.where(in_seq, glu[:, :CONV_W] * _sigmoid(glu[:, CONV_W:]), 0.0)
    rolled = _conv_rolls(cbuf)
    for lo in range(0, TM, CONV_ROWS):
        cb_ref[0, lo:lo + CONV_ROWS] = _conv_rows(rolled, lo, CONV_ROWS, cw_ref, cbias_ref,
                                                  cg_ref, cbeta_ref).astype(BF16)


def _in_proj(x, pos_row, g, w_in, gq, gk, gm, gsum, invf, lane_tab, expand, perm,
             conv_w, conv_b, conv_g, conv_beta, kmt, vm):
    B, S, _ = x.shape
    nt = S // TM
    per = TM // HALO

    def tile_of(step):
        i = jnp.maximum(step - N_CAST_IN, 0)
        return i // nt, i % nt

    def at_tile(fn):
        return lambda step: fn(*tile_of(step))

    tok = lambda w: pl.BlockSpec((1, TM, w), at_tile(lambda b, t: (b, t, 0)))
    xprev = pl.BlockSpec((1, HALO, D_MODEL),
                         at_tile(lambda b, t: (b, jnp.maximum(t * per - 1, 0), 0)))
    xnext = pl.BlockSpec((1, HALO, D_MODEL),
                         at_tile(lambda b, t: (b, jnp.minimum((t + 1) * per, nt * per - 1), 0)))
    plane = pl.BlockSpec((1, PLANES, TM // PLANES, ATTN_W), at_tile(lambda b, t: (b, 0, t, 0)))
    plane_shape = (B, PLANES, S // PLANES, ATTN_W)
    per_batch = lambda r, w: pl.BlockSpec((1, r, w), at_tile(lambda b, t: (b, 0, 0)))
    return pl.pallas_call(
        functools.partial(_in_proj_kernel, n_tiles=nt),
        grid=(N_CAST_IN + B * nt,),
        in_specs=[tok(D_MODEL), xprev, xnext,
                  pl.BlockSpec((1, 1, TM), at_tile(lambda b, t: (b, 0, t))),
                  _const_spec((1, D_MODEL)),
                  pl.BlockSpec((D_MODEL, IN_CAST),
                               lambda step: (0, jnp.minimum(step, N_CAST_IN - 1))),
                  _const_spec((1, ATTN_W)), _const_spec((1, ATTN_W)), _const_spec((1, MEM_W)),
                  _const_spec((MXU_DIM, MXU_DIM)), _const_spec((SUBLANES, LANES)),
                  _const_spec((SUBLANES, LANES)), _const_spec((LANES, 2 * LANES)),
                  _const_spec((TM, TM)),
                  _const_spec((CONV_K, CONV_W)), _const_spec((1, CONV_W)),
                  _const_spec((1, CONV_W)), _const_spec((1, CONV_W)),
                  per_batch(MEM_W, N_MEM), per_batch(N_MEM, MEM_W)],
        out_specs=[plane, plane, plane, tok(ATTN_W), tok(ATTN_W), tok(CONV_W), tok(MEM_W)],
        out_shape=[jax.ShapeDtypeStruct(plane_shape, F32),
                   jax.ShapeDtypeStruct(plane_shape, BF16),
                   jax.ShapeDtypeStruct(plane_shape, BF16),
                   jax.ShapeDtypeStruct((B, S, ATTN_W), BF16),
                   jax.ShapeDtypeStruct((B, S, ATTN_W), BF16),
                   jax.ShapeDtypeStruct((B, S, CONV_W), BF16),
                   jax.ShapeDtypeStruct((B, S, MEM_W), BF16)],
        scratch_shapes=[pltpu.VMEM((D_MODEL, IN_COLS), BF16)],
        compiler_params=pltpu.CompilerParams(dimension_semantics=("arbitrary",),
                                             vmem_limit_bytes=VMEM_LIMIT),
        name="in_proj",
    )(x, x, x, pos_row, g, w_in, gq, gk, gm, gsum, invf, lane_tab, expand, perm,
      conv_w, conv_b, conv_g, conv_beta, kmt, vm)


def _mem_kv_kernel(mem_ref, g_ref, w_ref, gk_ref, gsum_ref, kmt_ref, vm_ref):
    mn = _rms_rows(mem_ref[0], g_ref[...]).astype(BF16)
    kv = jnp.dot(mn, w_ref[...].astype(BF16), preferred_element_type=F32)
    (km,) = _head_norm(kv[:, :MEM_W], gk_ref[...], gsum_ref[...])
    kmt_ref[0] = km.T.astype(BF16)
    vm_ref[0] = kv[:, MEM_W:].astype(BF16)


def _mem_kv(mem, g, w, gk, gsum):
    B = mem.shape[0]
    return pl.pallas_call(
        _mem_kv_kernel,
        grid=(B,),
        in_specs=[pl.BlockSpec((1, N_MEM, D_MODEL), lambda b: (b, 0, 0)),
                  _const_spec((1, D_MODEL)), _const_spec((D_MODEL, 2 * MEM_W)),
                  _const_spec((1, MEM_W)), _const_spec((MXU_DIM, MXU_DIM))],
        out_specs=[pl.BlockSpec((1, MEM_W, N_MEM), lambda b: (b, 0, 0)),
                   pl.BlockSpec((1, N_MEM, MEM_W), lambda b: (b, 0, 0))],
        out_shape=[jax.ShapeDtypeStruct((B, MEM_W, N_MEM), BF16),
                   jax.ShapeDtypeStruct((B, N_MEM, MEM_W), BF16)],
        compiler_params=_cparams(1),
        name="mem_kv",
    )(mem, g, w, gk, gsum)


def _head_masks():
    lane = lax.broadcasted_iota(jnp.int32, (1, LANES), 1)
    low = lane < HEAD_DIM
    return low, (jnp.where(low, 1.0, 0.0).astype(BF16), jnp.where(low, 0.0, 1.0).astype(BF16))


def _mem_attn(qm, kmt_ref, vm_ref):
    low, hmask = _head_masks()
    ones = jnp.ones((N_MEM, LANES), BF16)
    out = []
    for c in range(MEM_W // LANES):
        sl = slice(c * LANES, (c + 1) * LANES)
        qc = qm[:, sl]
        v_aug = jnp.concatenate([vm_ref[0, :, sl], ones], axis=1)
        halves = []
        for e in range(2):
            s = jnp.dot(qc * hmask[e], kmt_ref[0, sl, :], preferred_element_type=F32)
            m = jnp.max(s, axis=-1, keepdims=True)
            p = jnp.exp((s - m).astype(BF16))
            r = jnp.dot(p, v_aug, preferred_element_type=F32)
            halves.append(r[:, :LANES] / r[:, LANES:])
        out.append(jnp.where(low, halves[0], halves[1]).astype(BF16))
    return jnp.concatenate(out, axis=1)


HW = ATTN_W // 2
MID_ROWS = TL // MID_DIL
MID_KEYS = TK // MID_DIL
MID_LEAD = (MID_KEYS - MID_ROWS) // 2
ONE_ROWS = TL // PLANES
TILE_GROUP = 16


def _band_tables():
    rho = np.arange(TL)[:, None]
    kap = np.arange(TK)[None, :]
    d16 = kap - BAND_HALF - rho
    j, lq = rho // MID_ROWS, rho % MID_ROWS
    jk, lk = kap // MID_KEYS, kap % MID_KEYS
    d4 = MID_DIL * (lk - MID_LEAD - lq) + (jk - j)
    r, l1 = rho // ONE_ROWS, rho % ONE_ROWS
    d1 = kap - BAND_HALF - PLANES * l1 - r
    tabs = [np.where(np.abs(d) <= BAND_HALF, 0.0, NEG_INF) for d in (d16, d4, d1)]
    return jnp.asarray(np.stack(tabs), F32)


ALL_CHUNKS = tuple(range(HW // LANES))


def _scores(q, kw, bias2, hmask, chunks=ALL_CHUNKS):
    out = []
    for c in chunks:
        sl = slice(c * LANES, (c + 1) * LANES)
        qc = q[:, sl]
        qs = jnp.concatenate([qc * hmask[0], qc * hmask[1]], axis=0)
        out.append(lax.dot_general(qs, kw[:, sl], (((1,), (1,)), ((), ())),
                                   preferred_element_type=F32) + bias2)
    return out


def _softmax_pv(scores, vw, low, hmask, m_old=None, chunks=ALL_CHUNKS):
    ones = jnp.ones((TK, LANES), BF16)
    res = []
    for c, s in zip(chunks, scores):
        m = jnp.max(s, axis=-1, keepdims=True)
        if m_old is None:
            shift = m
        else:
            m = jnp.maximum(m, m_old[c])
            shift = jnp.concatenate([m] * (TK // LANES), axis=1)
        p = jnp.exp((s - shift).astype(BF16))
        v_aug = jnp.concatenate([vw[:, c * LANES:(c + 1) * LANES], ones], axis=1)
        r = jnp.dot(p, v_aug, preferred_element_type=F32)
        res.append((m, jnp.where(low, r[:TL, LANES:], r[TL:, LANES:]),
                    jnp.where(low, r[:TL, :LANES], r[TL:, :LANES])))
    return res


def _attend(q, kw, vw, bias2, low, hmask, m_old=None):
    res = []
    for c in ALL_CHUNKS:
        res += _softmax_pv(_scores(q, kw, bias2, hmask, (c,)), vw, low, hmask, m_old, (c,))
    return res


def _merge(new, m_old, l_old, acc_old, low):
    m, l_n, acc_n = new
    b = jnp.exp(m_old - m)
    b = jnp.where(low, b[:TL], b[TL:])
    return m, l_n + b * l_old, acc_n + b * acc_old


def _dil_attn_kernel(q_ref, kc_ref, kp_ref, kx_ref, vc_ref, vp_ref, vx_ref,
                     knc_ref, knp_ref, knx_ref, vnc_ref, vnp_ref, vnx_ref, band_ref, unperm_ref,
                     o_ref, acc_ref, m_ref, l_ref, kne_ref, vne_ref):
    st = pl.program_id(1)
    first, last = st == 0, st == pl.num_programs(1) - 1
    low, hmask = _head_masks()
    col = lax.broadcasted_iota(jnp.int32, (1, TK), 1)
    chunk = lambda c: slice(c * LANES, (c + 1) * LANES)

    def stacked_bias(band, col_idx=None, lo=0, hi=TK):
        if col_idx is not None:
            band = band + jnp.where((col_idx < lo) | (col_idx >= hi), NEG_INF, 0.0)
        return jnp.concatenate([band, band], axis=0)

    bias16 = stacked_bias(band_ref[0], col, jnp.where(first, BAND_HALF, 0),
                          jnp.where(last, TK - BAND_HALF, TK))

    def pipelined(tiles, score_fn, finish_fn):
        s_next = score_fn(tiles[0])
        for i, t in enumerate(tiles):
            s = s_next
            if i + 1 < len(tiles):
                s_next = score_fn(tiles[i + 1])
            finish_fn(t, s)

    def body16(g, carry):
        def score(r):
            kw = jnp.concatenate([kp_ref[0, r], kc_ref[0, r], kx_ref[0, r]], axis=0)
            return _scores(q_ref[0, r].astype(BF16), kw, bias16, hmask)

        def finish(r, s):
            vw = jnp.concatenate([vp_ref[0, r], vc_ref[0, r], vx_ref[0, r]], axis=0)
            for c, (m, l, acc) in enumerate(_softmax_pv(s, vw, low, hmask)):
                acc_ref[r, :, chunk(c)] = acc
                l_ref[c, r] = l
                for e in range(2):
                    m_ref[c, e, r] = jnp.broadcast_to(m[e * TL:(e + 1) * TL], (TL, LANES))

        pipelined([g * TILE_GROUP + i for i in range(TILE_GROUP)], score, finish)
        return carry

    lax.fori_loop(0, PLANES // TILE_GROUP, body16, 0)

    def mid_window(cur, prev, nxt, plane, lb):
        lo = lb * MID_ROWS - MID_LEAD
        if lo < 0:
            return jnp.concatenate([prev[0, plane, TL // 2 + lo:TL // 2],
                                    cur[0, plane, 0:lo + MID_KEYS]], axis=0)
        if lo + MID_KEYS > TL:
            return jnp.concatenate([cur[0, plane, lo:TL],
                                    nxt[0, plane, 0:lo + MID_KEYS - TL]], axis=0)
        return cur[0, plane, lo:lo + MID_KEYS]

    def body4(g, carry):
        block_rows = lambda lb: slice(lb * MID_ROWS, (lb + 1) * MID_ROWS)
        plane_set = lambda r4: [r4 + MID_DIL * j for j in range(MID_DIL)]

        def score(tile):
            r4, lb = tile
            planes = plane_set(r4)
            band = band_ref[1]
            if lb == 0:
                bias = stacked_bias(band, col % MID_KEYS, jnp.where(first, MID_LEAD, 0), MID_KEYS)
            elif lb == TL // MID_ROWS - 1:
                bias = stacked_bias(band, col % MID_KEYS, 0,
                                    jnp.where(last, MID_KEYS - MID_LEAD, MID_KEYS))
            else:
                bias = stacked_bias(band)
            q = jnp.concatenate([q_ref[0, p, block_rows(lb)] for p in planes],
                                axis=0).astype(BF16)
            kw = jnp.concatenate([mid_window(kc_ref, kp_ref, kx_ref, p, lb) for p in planes],
                                 axis=0)
            return _scores(q, kw, bias, hmask)

        def finish(tile, s):
            r4, lb = tile
            planes = plane_set(r4)
            rows = block_rows(lb)
            stacked = lambda ref, c: jnp.concatenate(
                [ref[c, e, p, rows] for e in range(2) for p in planes], axis=0)
            vw = jnp.concatenate([mid_window(vc_ref, vp_ref, vx_ref, p, lb) for p in planes],
                                 axis=0)
            m_old = [stacked(m_ref, c) for c in range(HW // LANES)]
            for c, new in enumerate(_softmax_pv(s, vw, low, hmask, m_old)):
                acc_old = jnp.concatenate([acc_ref[p, rows, chunk(c)] for p in planes], axis=0)
                l_old = jnp.concatenate([l_ref[c, p, rows] for p in planes], axis=0)
                m_m, l_m, acc_m = _merge(new, m_old[c], l_old, acc_old, low)
                for j, p in enumerate(planes):
                    piece = slice(j * MID_ROWS, (j + 1) * MID_ROWS)
                    acc_ref[p, rows, chunk(c)] = acc_m[piece]
                    l_ref[c, p, rows] = l_m[piece]
                    for e in range(2):
                        head_piece = slice(e * TL + j * MID_ROWS, e * TL + (j + 1) * MID_ROWS)
                        m_ref[c, e, p, rows] = m_m[head_piece]

        per_body = TILE_GROUP // (TL // MID_ROWS)
        pipelined([(g * per_body + i, lb) for i in range(per_body)
                   for lb in range(TL // MID_ROWS)], score, finish)
        return carry

    lax.fori_loop(0, MID_DIL * (TL // MID_ROWS) // TILE_GROUP, body4, 0)

    n_tiles = SUPER // TL
    for ext_ref, prev, cur, nxt in ((kne_ref, knp_ref, knc_ref, knx_ref),
                                    (vne_ref, vnp_ref, vnc_ref, vnx_ref)):
        ext_ref[0:BAND_HALF] = prev[0]
        ext_ref[BAND_HALF:BAND_HALF + SUPER] = cur[0]
        ext_ref[BAND_HALF + SUPER:] = nxt[0]
    band1 = stacked_bias(band_ref[2])

    def body1(g, carry):
        tiles = [g * TILE_GROUP + i for i in range(TILE_GROUP)]
        tile_rows = lambda t: pl.ds(pl.multiple_of(t * ONE_ROWS, ONE_ROWS), ONE_ROWS)
        tile_keys = lambda t: pl.ds(pl.multiple_of(t * TL, TL), TK)

        def tile_scores(t):
            lo = jnp.where(first & (t == 0), BAND_HALF, 0)
            hi = jnp.where(last & (t == n_tiles - 1), TK - BAND_HALF, TK)
            bias = band1 + jnp.where((col < lo) | (col >= hi), NEG_INF, 0.0)
            q = q_ref[0, :, tile_rows(t), :].reshape(TL, HW).astype(BF16)
            return _scores(q, kne_ref[tile_keys(t), :], bias, hmask)

        def emit(t, merged):
            tok = jnp.dot(unperm_ref[...], merged, preferred_element_type=F32)
            o_ref[0, pl.ds(pl.multiple_of(t * TL, TL), TL), :] = tok.astype(BF16)

        pending = []

        def finish(t, s):
            rows = tile_rows(t)
            stacked = lambda ref, c: jnp.concatenate(
                [ref[c, e, :, rows, :].reshape(TL, LANES) for e in range(2)], axis=0)
            m_old = [stacked(m_ref, c) for c in range(HW // LANES)]
            outs = []
            for c, new in enumerate(_softmax_pv(s, vne_ref[tile_keys(t), :], low, hmask, m_old)):
                acc_old = acc_ref[:, rows, chunk(c)].reshape(TL, LANES)
                l_old = l_ref[c, :, rows, :].reshape(TL, LANES)
                _, l_m, acc_m = _merge(new, m_old[c], l_old, acc_old, low)
                outs.append((acc_m / l_m).astype(BF16))
            if pending:
                emit(*pending.pop())
            pending.append((t, jnp.concatenate(outs, axis=1)))

        pipelined(tiles, tile_scores, finish)
        emit(*pending.pop())
        return carry

    lax.fori_loop(0, n_tiles // TILE_GROUP, body1, 0)


def _dil_attn(q16, k16, v16, kn, vn, band, unperm):
    B, S, _ = kn.shape
    n_half = S // PLANES // BAND_HALF
    n_tok_half = S // BAND_HALF
    per = TL // BAND_HALF
    per_tok = SUPER // BAND_HALF
    pcur = pl.BlockSpec((1, PLANES, TL, HW), lambda b, s, hh: (b, 0, s, hh))
    pprev = pl.BlockSpec((1, PLANES, BAND_HALF, HW),
                         lambda b, s, hh: (b, 0, jnp.maximum(s * per - 1, 0), hh))
    pnext = pl.BlockSpec((1, PLANES, BAND_HALF, HW),
                         lambda b, s, hh: (b, 0, jnp.minimum((s + 1) * per, n_half - 1), hh))
    tcur = pl.BlockSpec((1, SUPER, HW), lambda b, s, hh: (b, s, hh))
    tprev = pl.BlockSpec((1, BAND_HALF, HW),
                         lambda b, s, hh: (b, jnp.maximum(s * per_tok - 1, 0), hh))
    tnext = pl.BlockSpec((1, BAND_HALF, HW),
                         lambda b, s, hh: (b, jnp.minimum((s + 1) * per_tok, n_tok_half - 1), hh))
    return pl.pallas_call(
        _dil_attn_kernel,
        grid=(B, S // SUPER, ATTN_W // HW),
        in_specs=[pcur, pcur, pprev, pnext, pcur, pprev, pnext,
                  tcur, tprev, tnext, tcur, tprev, tnext,
                  _const_spec((3, TL, TK)), _const_spec((TL, TL))],
        out_specs=tcur,
        out_shape=jax.ShapeDtypeStruct((B, S, ATTN_W), BF16),
        scratch_shapes=[pltpu.VMEM((PLANES, TL, HW), F32),
                        pltpu.VMEM((HW // LANES, 2, PLANES, TL, LANES), F32),
                        pltpu.VMEM((HW // LANES, PLANES, TL, LANES), F32),
                        pltpu.VMEM((SUPER + 2 * BAND_HALF, HW), BF16),
                        pltpu.VMEM((SUPER + 2 * BAND_HALF, HW), BF16)],
        compiler_params=_cparams(3),
        name="dil_attn",
    )(q16, k16, k16, k16, v16, v16, v16, kn, kn, kn, vn, vn, vn, band, unperm)


def _halo_specs(width, n_tiles, rows, tile=TM):
    per = tile // rows
    last = n_tiles * per - 1
    prev = pl.BlockSpec((1, rows, width), lambda b, t: (b, jnp.maximum(t * per - 1, 0), 0))
    nxt = pl.BlockSpec((1, rows, width), lambda b, t: (b, jnp.minimum((t + 1) * per, last), 0))
    return prev, nxt


def _conv_rolls(buf):
    rows = buf.shape[0]
    return [buf if shift == 0 else pltpu.roll(buf, rows - shift, 0) for shift in range(SUBLANES)]


def _conv_rows(rolled, lo_row, n_rows, w_ref, b_ref, g_ref, beta_ref):
    acc = jnp.zeros((n_rows, CONV_W), F32) + b_ref[...]
    base = HALO - CONV_K // 2
    for tap in range(CONV_K):
        shift = (base + tap) % SUBLANES
        lo = lo_row + base + tap - shift
        acc = acc + w_ref[tap:tap + 1, :] * rolled[shift][lo:lo + n_rows]
    mu = jnp.mean(acc, axis=-1, keepdims=True)
    d = acc - mu
    var = jnp.mean(d * d, axis=-1, keepdims=True)
    z = d * lax.rsqrt(var + NORM_EPS) * g_ref[...] + beta_ref[...]
    return z * _sigmoid(z)


N_CAST = 11
UP_CAST = 2 * D_FF // N_CAST
DOWN_CAST = D_FF // N_CAST
OUT_CAST = LANES


def _out_ffn_kernel(a_ref, ap_ref, ax_ref, c_ref, cp_ref, cx_ref, m_ref, mp_ref, mx_ref,
                    x_ref, xp_ref, xx_ref, wo32_ref, g_ref, wu32_ref, dw_ref, db_ref, wd32_ref,
                    o_ref, gate_ref, wo_ref, wu_ref, wd_ref, *, n_tiles):
    step = pl.program_id(0)

    for c in range(N_CAST):
        @pl.when(step == c)
        def _(c=c):
            for half in range(UP_CAST // FF_CH):
                src = c * UP_CAST + half * FF_CH
                j, is_up = (src // FF_CH, 0) if src < D_FF else ((src - D_FF) // FF_CH, 1)
                dst = (2 * j + is_up) * FF_CH
                wu_ref[:, dst:dst + FF_CH] = wu32_ref[:, half * FF_CH:(half + 1) * FF_CH].astype(BF16)
            wd_ref[c * DOWN_CAST:(c + 1) * DOWN_CAST, :] = wd32_ref[...].astype(BF16)
            if (c + 1) * OUT_CAST <= D_MODEL:
                wo_ref[c * OUT_CAST:(c + 1) * OUT_CAST, :] = wo32_ref[...].astype(BF16)

    @pl.when(step >= N_CAST)
    def _():
        _out_ffn_tile(a_ref, ap_ref, ax_ref, c_ref, cp_ref, cx_ref, m_ref, mp_ref, mx_ref,
                      x_ref, xp_ref, xx_ref, wo_ref, g_ref, wu_ref, dw_ref, db_ref, wd_ref,
                      o_ref, gate_ref, lax.rem(step - N_CAST, n_tiles), n_tiles)


def _out_ffn_tile(a_ref, ap_ref, ax_ref, c_ref, cp_ref, cx_ref, m_ref, mp_ref, mx_ref,
                  x_ref, xp_ref, xx_ref, wo_ref, g_ref, wu_ref, dw_ref, db_ref, wd_ref,
                  o_ref, gate_ref, t, nt):
    has_next, has_prev = t < nt - 1, t > 0
    rows = TF + HALO

    def ext_rows(cur, nxt, prev):
        halo = jnp.concatenate([nxt[0, :SUBLANES].astype(F32), prev[0, SUBLANES:].astype(F32)],
                               axis=0)
        return jnp.concatenate([cur[0], halo.astype(cur.dtype)], axis=0)

    rid = lax.broadcasted_iota(jnp.int32, (rows, 1), 0)
    in_seq = ((rid < TF) | ((rid < TF + SUBLANES) & has_next)
              | ((rid >= TF + SUBLANES) & has_prev))
    mixed = jnp.concatenate([ext_rows(a_ref, ax_ref, ap_ref), ext_rows(c_ref, cx_ref, cp_ref),
                             ext_rows(m_ref, mx_ref, mp_ref)], axis=-1)
    mixed = jnp.where(in_seq, mixed, jnp.zeros_like(mixed))
    h = (jnp.where(in_seq, ext_rows(x_ref, xx_ref, xp_ref), 0.0)
         + jnp.dot(mixed, wo_ref[...], preferred_element_type=F32))
    ext = _rms_rows(h, g_ref[...]).astype(BF16)

    def conv3(f, lo):
        w = dw_ref[:, lo:lo + FF_CH]
        y = (w[0:1] * pltpu.roll(f, 1, 0) + w[1:2] * f + w[2:3] * pltpu.roll(f, rows - 1, 0))
        return y[:TF] + db_ref[:, lo:lo + FF_CH]

    n_chunks = D_FF // FF_CH
    out = h[:TF]
    for j in range(n_chunks):
        lo_g, lo_u = j * FF_CH, D_FF + j * FF_CH
        f = jnp.dot(ext, wu_ref[:, 2 * lo_g:2 * lo_g + 2 * FF_CH], preferred_element_type=F32)
        if j == n_chunks - 1:
            out = out + jnp.dot(gate_ref[:, :lo_g], wd_ref[:lo_g, :], preferred_element_type=F32)
        fg, fu = conv3(f[:, :FF_CH], lo_g), conv3(f[:, FF_CH:], lo_u)
        gate_ref[:, lo_g:lo_g + FF_CH] = (fg * _sigmoid(fg) * fu).astype(BF16)
    o_ref[0] = out + jnp.dot(gate_ref[:, D_FF - FF_CH:], wd_ref[D_FF - FF_CH:, :],
                             preferred_element_type=F32)


def _out_ffn(attn, cb, mo, x, w_out, g, w_up, dw_w, dw_b, w_down):
    B, S, _ = x.shape
    nt = S // TF
    per = TF // HALO

    def tile_of(step):
        i = jnp.maximum(step - N_CAST, 0)
        return i // nt, i % nt

    def with_halos(width):
        def cur(step):
            b, t = tile_of(step)
            return b, t, 0

        def prev(step):
            b, t = tile_of(step)
            return b, jnp.maximum(t * per - 1, 0), 0

        def nxt(step):
            b, t = tile_of(step)
            return b, jnp.minimum((t + 1) * per, nt * per - 1), 0

        return [pl.BlockSpec((1, TF, width), cur), pl.BlockSpec((1, HALO, width), prev),
                pl.BlockSpec((1, HALO, width), nxt)]

    cast_chunk = lambda limit: (lambda step: jnp.minimum(step, limit - 1))
    out_i, up_i, down_i = cast_chunk(D_MODEL // OUT_CAST), cast_chunk(N_CAST), cast_chunk(N_CAST)
    return pl.pallas_call(
        functools.partial(_out_ffn_kernel, n_tiles=nt),
        grid=(N_CAST + B * nt,),
        in_specs=with_halos(ATTN_W) + with_halos(CONV_W) + with_halos(MEM_W)
        + with_halos(D_MODEL)
        + [pl.BlockSpec((OUT_CAST, D_MODEL), lambda s: (out_i(s), 0)), _const_spec((1, D_MODEL)),
           pl.BlockSpec((D_MODEL, UP_CAST), lambda s: (0, up_i(s))),
           _const_spec((FFN_CONV_K, 2 * D_FF)), _const_spec((1, 2 * D_FF)),
           pl.BlockSpec((DOWN_CAST, D_MODEL), lambda s: (down_i(s), 0))],
        out_specs=pl.BlockSpec((1, TF, D_MODEL), lambda s: (*tile_of(s), 0)),
        out_shape=jax.ShapeDtypeStruct((B, S, D_MODEL), F32),
        scratch_shapes=[pltpu.VMEM((TF, D_FF), BF16), pltpu.VMEM((D_MODEL, D_MODEL), BF16),
                        pltpu.VMEM((D_MODEL, 2 * D_FF), BF16), pltpu.VMEM((D_FF, D_MODEL), BF16)],
        compiler_params=pltpu.CompilerParams(dimension_semantics=("arbitrary",),
                                             vmem_limit_bytes=VMEM_LIMIT),
        name="out_ffn",
    )(attn, attn, attn, cb, cb, cb, mo, mo, mo, x, x, x, w_out, g, w_up, dw_w, dw_b, w_down)


def _rope_tables():
    inv_freq = ROPE_THETA ** (-jnp.arange(0, ROT_DIM, 2, dtype=F32) / ROT_DIM)
    invf = jnp.broadcast_to(inv_freq[:, None], (SUBLANES, LANES))
    half = ROT_DIM // 2
    lane = np.arange(LANES) % HEAD_DIM
    lanes = np.zeros((SUBLANES, LANES), np.float32)
    lanes[0] = lane >= ROT_DIM
    lanes[1] = lane < half
    expand = np.zeros((LANES, 2 * LANES), np.float32)
    for l in range(LANES):
        if lane[l] < ROT_DIM:
            j = lane[l] % half
            expand[[j, half + j], l] = 1.0
            expand[[2 * half + j, 3 * half + j], LANES + l] = -1.0 if lane[l] < half else 1.0
    return invf, jnp.asarray(lanes), jnp.asarray(expand, BF16)


def _group_sum_matrix():
    idx = np.arange(MXU_DIM) // HEAD_DIM
    return jnp.asarray(idx[:, None] == idx[None, :], BF16)


def _plane_perm(n):
    out = np.arange(n)
    src = (out % (n // PLANES)) * PLANES + out // (n // PLANES)
    return np.asarray(src[:, None] == np.arange(n)[None, :], np.float32)


def kernel(x, mem, positions, mix_norm_g, mem_norm_g, w_in, w_mem_kv, q_norm_g, k_norm_g, mq_norm_g, mk_norm_g, conv_dw_w, conv_dw_b, conv_ln_g, conv_ln_b, w_out, ffn_norm_g, w_up, ffn_dw_w, ffn_dw_b, w_down):
    B, S, _ = x.shape
    depth = w_in.shape[0]
    pos_row = positions.reshape(B, 1, S)
    invf, lane_tab, expand = _rope_tables()
    gsum = _group_sum_matrix()
    perm = jnp.asarray(_plane_perm(TM), BF16)
    unperm = jnp.asarray(_plane_perm(TL).T, BF16)
    band = _band_tables()
    row = lambda a: a.reshape(1, -1)
    h = x
    for l in range(depth):
        kmt, vm = _mem_kv(mem, row(mem_norm_g[l]), w_mem_kv[l],
                          row(jnp.tile(mk_norm_g[l], MEM_HEADS)), gsum)
        q16, k16, v16, kn, vn, cb, mo = _in_proj(
            h, pos_row, row(mix_norm_g[l]), w_in[l],
            row(jnp.tile(q_norm_g[l], ATTN_HEADS)), row(jnp.tile(k_norm_g[l], ATTN_HEADS)),
            row(jnp.tile(mq_norm_g[l], MEM_HEADS)), gsum, invf, lane_tab, expand, perm,
            conv_dw_w[l], row(conv_dw_b[l]), row(conv_ln_g[l]), row(conv_ln_b[l]), kmt, vm)
        attn = _dil_attn(q16, k16, v16, kn, vn, band, unperm)
        h = _out_ffn(attn, cb, mo, h, w_out[l], row(ffn_norm_g[l]), w_up[l], ffn_dw_w[l],
                     row(ffn_dw_b[l]), w_down[l])
    return h
```

```python
import functools

import numpy as np
import jax
import jax.numpy as jnp
from jax import lax
from jax.experimental import pallas as pl
from jax.experimental.pallas import tpu as pltpu

F32 = jnp.float32
BF16 = jnp.bfloat16

D_MODEL = 1024
HEAD_DIM = 64
ATTN_HEADS = 8
ATTN_W = ATTN_HEADS * HEAD_DIM
CONV_W = 256
MEM_HEADS = 4
MEM_W = MEM_HEADS * HEAD_DIM
N_MEM = 256
PLANES = 16
MID_DIL = 4
BAND_HALF = 64
ROPE_THETA = 500000.0
ROT_DIM = HEAD_DIM // 4
CONV_K = 31
FFN_CONV_K = 3
D_FF = 2816
NORM_EPS = 1e-6
NEG_INF = -1e30
SM_SCALE = HEAD_DIM ** -0.5

LANES = 128
SUBLANES = 8
MXU_DIM = 256
BF16_ROWS = 16
VMEM_LIMIT = 56 * 1024 * 1024

TM = 512
CONV_ROWS = 64
TF = 512
TL = 128
TK = TL + 2 * BAND_HALF
SUPER = PLANES * TL
FF_CH = MXU_DIM
HALO = BF16_ROWS


def _cparams(n_axes):
    return pltpu.CompilerParams(dimension_semantics=("parallel",) * n_axes,
                                vmem_limit_bytes=VMEM_LIMIT)


def _const_spec(shape):
    return pl.BlockSpec(shape, lambda *_: (0,) * len(shape), pipeline_mode=pl.Buffered(1))


def _rms_rows(x, g):
    r = lax.rsqrt(jnp.mean(x * x, axis=-1, keepdims=True) + NORM_EPS)
    return x * r * g


def _head_sumsq(t, gsum):
    return [jnp.dot((tc * tc).astype(BF16), gsum, preferred_element_type=F32)
            for tc in (t[:, c * MXU_DIM:(c + 1) * MXU_DIM] for c in range(t.shape[1] // MXU_DIM))]


def _head_scale(t, sumsq, gain):
    outs = []
    for c, ssum in enumerate(sumsq):
        sl = slice(c * MXU_DIM, (c + 1) * MXU_DIM)
        outs.append(t[:, sl] * lax.rsqrt(ssum * (1.0 / HEAD_DIM) + NORM_EPS) * gain[:, sl])
    return outs


def _head_norm(t, gain, gsum):
    return _head_scale(t, _head_sumsq(t, gsum), gain)


def _sigmoid(x):
    return 1.0 / (1.0 + jnp.exp(-x))


IN_COLS = 3 * ATTN_W + 2 * CONV_W + MEM_W
IN_CAST = MXU_DIM
N_CAST_IN = IN_COLS // IN_CAST


def _in_proj_kernel(x_ref, xp_ref, xx_ref, pos_ref, g_ref, w32_ref, gq_ref, gk_ref, gm_ref,
                    gsum_ref, invf_ref, lane_ref, expand_ref, perm_ref,
                    cw_ref, cbias_ref, cg_ref, cbeta_ref, kmt_ref, vm_ref,
                    q16_ref, k16_ref, v16_ref, kn_ref, vn_ref, cb_ref, mo_ref,
                    w_ref, *, n_tiles):
    step = pl.program_id(0)

    for c in range(N_CAST_IN):
        @pl.when(step == c)
        def _(c=c):
            w_ref[:, c * IN_CAST:(c + 1) * IN_CAST] = w32_ref[...].astype(BF16)

    @pl.when(step >= N_CAST_IN)
    def _():
        _in_proj_tile(x_ref, xp_ref, xx_ref, pos_ref, g_ref, w_ref, gq_ref, gk_ref, gm_ref,
                      gsum_ref, invf_ref, lane_ref, expand_ref, perm_ref,
                      cw_ref, cbias_ref, cg_ref, cbeta_ref, kmt_ref, vm_ref,
                      q16_ref, k16_ref, v16_ref, kn_ref, vn_ref, cb_ref, mo_ref,
                      lax.rem(step - N_CAST_IN, n_tiles), n_tiles)


def _in_proj_tile(x_ref, xp_ref, xx_ref, pos_ref, g_ref, w_ref, gq_ref, gk_ref, gm_ref,
                  gsum_ref, invf_ref, lane_ref, expand_ref, perm_ref,
                  cw_ref, cbias_ref, cg_ref, cbeta_ref, kmt_ref, vm_ref,
                  q16_ref, k16_ref, v16_ref, kn_ref, vn_ref, cb_ref, mo_ref,
                  t, nt):
    has_prev, has_next = t > 0, t < nt - 1
    hn = _rms_rows(x_ref[0], g_ref[...]).astype(BF16)

    hn_halo = _rms_rows(jnp.concatenate([xp_ref[0], xx_ref[0]], axis=0), g_ref[...]).astype(BF16)
    hn_ext = jnp.concatenate([hn_halo[:HALO], hn, hn_halo[HALO:]], axis=0)
    cols = lambda lo, n: w_ref[:, lo:lo + n]
    q = jnp.dot(hn, cols(0, ATTN_W), preferred_element_type=F32)
    k = jnp.dot(hn, cols(ATTN_W, ATTN_W), preferred_element_type=F32)
    v = jnp.dot(hn, cols(2 * ATTN_W, ATTN_W), preferred_element_type=F32)
    glu = jnp.dot(hn_ext, cols(3 * ATTN_W, 2 * CONV_W), preferred_element_type=F32)
    qm = jnp.dot(hn, cols(3 * ATTN_W + 2 * CONV_W, MEM_W), preferred_element_type=F32)

    ang = invf_ref[:, 0:1] * pos_ref[0].astype(F32)

    def hi_lo(v):
        hi = v.astype(BF16).astype(F32)
        return [hi, (v - hi).astype(BF16).astype(F32)]

    tab = jnp.concatenate(hi_lo(jnp.cos(ang)) + hi_lo(jnp.sin(ang))
                          + [jnp.zeros((LANES - 4 * SUBLANES, TM), F32)], axis=0)
    cs = jnp.dot(tab.T.astype(BF16), expand_ref[...], preferred_element_type=F32)
    cosv = cs[:, :LANES] + lane_ref[0:1, :]
    sinv = cs[:, LANES:]
    first_half = lane_ref[1:2, :] > 0.5

    def rotary(chunks, scale):
        out = []
        for t in chunks:
            for s in range(MXU_DIM // LANES):
                xc = t[:, s * LANES:(s + 1) * LANES]
                partner = jnp.where(first_half, pltpu.roll(xc, LANES - ROT_DIM // 2, 1),
                                    pltpu.roll(xc, ROT_DIM // 2, 1))
                out.append((xc * cosv + partner * sinv) * scale)
        return out

    gsum = gsum_ref[...]
    q_ss, k_ss, qm_ss = _head_sumsq(q, gsum), _head_sumsq(k, gsum), _head_sumsq(qm, gsum)
    q_chunks = rotary(_head_scale(q, q_ss, gq_ref[...]), SM_SCALE)
    k_chunks = rotary(_head_scale(k, k_ss, gk_ref[...]), 1.0)
    qb = jnp.concatenate(q_chunks, axis=1).astype(BF16)
    kb = jnp.concatenate(k_chunks, axis=1).astype(BF16)
    vb = v.astype(BF16)
    kn_ref[0] = kb
    vn_ref[0] = vb
    qkv = jnp.dot(perm_ref[...], jnp.concatenate([qb, kb, vb], axis=1),
                  preferred_element_type=F32)
    rows = TM // PLANES
    for r in range(PLANES):
        blk = qkv[r * rows:(r + 1) * rows]
        q16_ref[0, r] = blk[:, 0:ATTN_W]
        k16_ref[0, r] = blk[:, ATTN_W:2 * ATTN_W].astype(BF16)
        v16_ref[0, r] = blk[:, 2 * ATTN_W:].astype(BF16)

    (qmn,) = _head_scale(qm, qm_ss, gm_ref[...])
    mo_ref[0] = _mem_attn((qmn * SM_SCALE).astype(BF16), kmt_ref, vm_ref)

    rid = lax.broadcasted_iota(jnp.int32, (TM + 2 * HALO, 1), 0)
    in_seq = ((rid >= HALO) | has_prev) & ((rid < TM + HALO) | has_next)
    cbuf = jnp.where(in_seq, glu[:, :CONV_W] * _sigmoid(glu[:, CONV_W:]), 0.0)
    rolled = _conv_rolls(cbuf)
    for lo in range(0, TM, CONV_ROWS):
        cb_ref[0, lo:lo + CONV_ROWS] = _conv_rows(rolled, lo, CONV_ROWS, cw_ref, cbias_ref,
                                                  cg_ref, cbeta_ref).astype(BF16)


def _in_proj(x, pos_row, g, w_in, gq, gk, gm, gsum, invf, lane_tab, expand, perm,
             conv_w, conv_b, conv_g, conv_beta, kmt, vm):
    B, S, _ = x.shape
    nt = S // TM
    per = TM // HALO

    def tile_of(step):
        i = jnp.maximum(step - N_CAST_IN, 0)
        return i // nt, i % nt

    def at_tile(fn):
        return lambda step: fn(*tile_of(step))

    tok = lambda w: pl.BlockSpec((1, TM, w), at_tile(lambda b, t: (b, t, 0)))
    xprev = pl.BlockSpec((1, HALO, D_MODEL),
                         at_tile(lambda b, t: (b, jnp.maximum(t * per - 1, 0), 0)))
    xnext = pl.BlockSpec((1, HALO, D_MODEL),
                         at_tile(lambda b, t: (b, jnp.minimum((t + 1) * per, nt * per - 1), 0)))
    plane = pl.BlockSpec((1, PLANES, TM // PLANES, ATTN_W), at_tile(lambda b, t: (b, 0, t, 0)))
    plane_shape = (B, PLANES, S // PLANES, ATTN_W)
    per_batch = lambda r, w: pl.BlockSpec((1, r, w), at_tile(lambda b, t: (b, 0, 0)))
    return pl.pallas_call(
        functools.partial(_in_proj_kernel, n_tiles=nt),
        grid=(N_CAST_IN + B * nt,),
        in_specs=[tok(D_MODEL), xprev, xnext,
                  pl.BlockSpec((1, 1, TM), at_tile(lambda b, t: (b, 0, t))),
                  _const_spec((1, D_MODEL)),
                  pl.BlockSpec((D_MODEL, IN_CAST),
                               lambda step: (0, jnp.minimum(step, N_CAST_IN - 1))),
                  _const_spec((1, ATTN_W)), _const_spec((1, ATTN_W)), _const_spec((1, MEM_W)),
                  _const_spec((MXU_DIM, MXU_DIM)), _const_spec((SUBLANES, LANES)),
                  _const_spec((SUBLANES, LANES)), _const_spec((LANES, 2 * LANES)),
                  _const_spec((TM, TM)),
                  _const_spec((CONV_K, CONV_W)), _const_spec((1, CONV_W)),
                  _const_spec((1, CONV_W)), _const_spec((1, CONV_W)),
                  per_batch(MEM_W, N_MEM), per_batch(N_MEM, MEM_W)],
        out_specs=[plane, plane, plane, tok(ATTN_W), tok(ATTN_W), tok(CONV_W), tok(MEM_W)],
        out_shape=[jax.ShapeDtypeStruct(plane_shape, F32),
                   jax.ShapeDtypeStruct(plane_shape, BF16),
                   jax.ShapeDtypeStruct(plane_shape, BF16),
                   jax.ShapeDtypeStruct((B, S, ATTN_W), BF16),
                   jax.ShapeDtypeStruct((B, S, ATTN_W), BF16),
                   jax.ShapeDtypeStruct((B, S, CONV_W), BF16),
                   jax.ShapeDtypeStruct((B, S, MEM_W), BF16)],
        scratch_shapes=[pltpu.VMEM((D_MODEL, IN_COLS), BF16)],
        compiler_params=pltpu.CompilerParams(dimension_semantics=("arbitrary",),
                                             vmem_limit_bytes=VMEM_LIMIT),
        name="in_proj",
    )(x, x, x, pos_row, g, w_in, gq, gk, gm, gsum, invf, lane_tab, expand, perm,
      conv_w, conv_b, conv_g, conv_beta, kmt, vm)


def _mem_kv_kernel(mem_ref, g_ref, w_ref, gk_ref, gsum_ref, kmt_ref, vm_ref):
    mn = _rms_rows(mem_ref[0], g_ref[...]).astype(BF16)
    kv = jnp.dot(mn, w_ref[...].astype(BF16), preferred_element_type=F32)
    (km,) = _head_norm(kv[:, :MEM_W], gk_ref[...], gsum_ref[...])
    kmt_ref[0] = km.T.astype(BF16)
    vm_ref[0] = kv[:, MEM_W:].astype(BF16)


def _mem_kv(mem, g, w, gk, gsum):
    B = mem.shape[0]
    return pl.pallas_call(
        _mem_kv_kernel,
        grid=(B,),
        in_specs=[pl.BlockSpec((1, N_MEM, D_MODEL), lambda b: (b, 0, 0)),
                  _const_spec((1, D_MODEL)), _const_spec((D_MODEL, 2 * MEM_W)),
                  _const_spec((1, MEM_W)), _const_spec((MXU_DIM, MXU_DIM))],
        out_specs=[pl.BlockSpec((1, MEM_W, N_MEM), lambda b: (b, 0, 0)),
                   pl.BlockSpec((1, N_MEM, MEM_W), lambda b: (b, 0, 0))],
        out_shape=[jax.ShapeDtypeStruct((B, MEM_W, N_MEM), BF16),
                   jax.ShapeDtypeStruct((B, N_MEM, MEM_W), BF16)],
        compiler_params=_cparams(1),
        name="mem_kv",
    )(mem, g, w, gk, gsum)


def _head_masks():
    lane = lax.broadcasted_iota(jnp.int32, (1, LANES), 1)
    low = lane < HEAD_DIM
    return low, (jnp.where(low, 1.0, 0.0).astype(BF16), jnp.where(low, 0.0, 1.0).astype(BF16))


def _mem_attn(qm, kmt_ref, vm_ref):
    low, hmask = _head_masks()
    ones = jnp.ones((N_MEM, LANES), BF16)
    out = []
    for c in range(MEM_W // LANES):
        sl = slice(c * LANES, (c + 1) * LANES)
        qc = qm[:, sl]
        v_aug = jnp.concatenate([vm_ref[0, :, sl], ones], axis=1)
        halves = []
        for e in range(2):
            s = jnp.dot(qc * hmask[e], kmt_ref[0, sl, :], preferred_element_type=F32)
            m = jnp.max(s, axis=-1, keepdims=True)
            p = jnp.exp((s - m).astype(BF16))
            r = jnp.dot(p, v_aug, preferred_element_type=F32)
            halves.append(r[:, :LANES] / r[:, LANES:])
        out.append(jnp.where(low, halves[0], halves[1]).astype(BF16))
    return jnp.concatenate(out, axis=1)


HW = ATTN_W // 2
MID_ROWS = TL // MID_DIL
MID_KEYS = TK // MID_DIL
MID_LEAD = (MID_KEYS - MID_ROWS) // 2
ONE_ROWS = TL // PLANES
TILE_GROUP = 16


def _band_tables():
    rho = np.arange(TL)[:, None]
    kap = np.arange(TK)[None, :]
    d16 = kap - BAND_HALF - rho
    j, lq = rho // MID_ROWS, rho % MID_ROWS
    jk, lk = kap // MID_KEYS, kap % MID_KEYS
    d4 = MID_DIL * (lk - MID_LEAD - lq) + (jk - j)
    r, l1 = rho // ONE_ROWS, rho % ONE_ROWS
    d1 = kap - BAND_HALF - PLANES * l1 - r
    tabs = [np.where(np.abs(d) <= BAND_HALF, 0.0, NEG_INF) for d in (d16, d4, d1)]
    return jnp.asarray(np.stack(tabs), F32)


def _scores(q, kw, bias2, hmask):
    out = []
    for c in range(HW // LANES):
        sl = slice(c * LANES, (c + 1) * LANES)
        qc = q[:, sl]
        qs = jnp.concatenate([qc * hmask[0], qc * hmask[1]], axis=0)
        out.append(lax.dot_general(qs, kw[:, sl], (((1,), (1,)), ((), ())),
                                   preferred_element_type=F32) + bias2)
    return out


def _softmax_pv(scores, vw, low, m_old=None):
    ones = jnp.ones((TK, LANES), BF16)
    res = []
    for c, s in enumerate(scores):
        m = jnp.max(s, axis=-1, keepdims=True)
        if m_old is None:
            shift = m
        else:
            m = jnp.maximum(m, m_old[c])
            shift = jnp.concatenate([m] * (TK // LANES), axis=1)
        p = jnp.exp((s - shift).astype(BF16))
        v_aug = jnp.concatenate([vw[:, c * LANES:(c + 1) * LANES], ones], axis=1)
        r = jnp.dot(p, v_aug, preferred_element_type=F32)
        res.append((m, jnp.where(low, r[:TL, LANES:], r[TL:, LANES:]),
                    jnp.where(low, r[:TL, :LANES], r[TL:, :LANES])))
    return res


def _merge(new, m_old, l_old, acc_old, low):
    m, l_n, acc_n = new
    b = jnp.exp(m_old - m)
    b = jnp.where(low, b[:TL], b[TL:])
    return m, l_n + b * l_old, acc_n + b * acc_old


def _dil_attn_kernel(q_ref, kc_ref, kp_ref, kx_ref, vc_ref, vp_ref, vx_ref,
                     knc_ref, knp_ref, knx_ref, vnc_ref, vnp_ref, vnx_ref, band_ref, unperm_ref,
                     o_ref, acc_ref, m_ref, l_ref, kne_ref, vne_ref):
    st = pl.program_id(1)
    first, last = st == 0, st == pl.num_programs(1) - 1
    low, hmask = _head_masks()
    col = lax.broadcasted_iota(jnp.int32, (1, TK), 1)
    chunk = lambda c: slice(c * LANES, (c + 1) * LANES)

    def stacked_bias(band, col_idx=None, lo=0, hi=TK):
        if col_idx is not None:
            band = band + jnp.where((col_idx < lo) | (col_idx >= hi), NEG_INF, 0.0)
        return jnp.concatenate([band, band], axis=0)

    bias16 = stacked_bias(band_ref[0], col, jnp.where(first, BAND_HALF, 0),
                          jnp.where(last, TK - BAND_HALF, TK))

    def pipelined(tiles, score_fn, finish_fn):
        s_next = score_fn(tiles[0])
        for i, t in enumerate(tiles):
            s = s_next
            if i + 1 < len(tiles):
                s_next = score_fn(tiles[i + 1])
            finish_fn(t, s)

    def body16(g, carry):
        def score(r):
            kw = jnp.concatenate([kp_ref[0, r], kc_ref[0, r], kx_ref[0, r]], axis=0)
            return _scores(q_ref[0, r].astype(BF16), kw, bias16, hmask)

        def finish(r, s):
            vw = jnp.concatenate([vp_ref[0, r], vc_ref[0, r], vx_ref[0, r]], axis=0)
            for c, (m, l, acc) in enumerate(_softmax_pv(s, vw, low)):
                acc_ref[r, :, chunk(c)] = acc
                l_ref[c, r] = l
                for e in range(2):
                    m_ref[c, e, r] = jnp.broadcast_to(m[e * TL:(e + 1) * TL], (TL, LANES))

        pipelined([g * TILE_GROUP + i for i in range(TILE_GROUP)], score, finish)
        return carry

    lax.fori_loop(0, PLANES // TILE_GROUP, body16, 0)

    def mid_window(cur, prev, nxt, plane, lb):
        lo = lb * MID_ROWS - MID_LEAD
        if lo < 0:
            return jnp.concatenate([prev[0, plane, TL // 2 + lo:TL // 2],
                                    cur[0, plane, 0:lo + MID_KEYS]], axis=0)
        if lo + MID_KEYS > TL:
            return jnp.concatenate([cur[0, plane, lo:TL],
                                    nxt[0, plane, 0:lo + MID_KEYS - TL]], axis=0)
        return cur[0, plane, lo:lo + MID_KEYS]

    def body4(g, carry):
        block_rows = lambda lb: slice(lb * MID_ROWS, (lb + 1) * MID_ROWS)
        plane_set = lambda r4: [r4 + MID_DIL * j for j in range(MID_DIL)]

        def score(tile):
            r4, lb = tile
            planes = plane_set(r4)
            band = band_ref[1]
            if lb == 0:
                bias = stacked_bias(band, col % MID_KEYS, jnp.where(first, MID_LEAD, 0), MID_KEYS)
            elif lb == TL // MID_ROWS - 1:
                bias = stacked_bias(band, col % MID_KEYS, 0,
                                    jnp.where(last, MID_KEYS - MID_LEAD, MID_KEYS))
            else:
                bias = stacked_bias(band)
            q = jnp.concatenate([q_ref[0, p, block_rows(lb)] for p in planes],
                                axis=0).astype(BF16)
            kw = jnp.concatenate([mid_window(kc_ref, kp_ref, kx_ref, p, lb) for p in planes],
                                 axis=0)
            return _scores(q, kw, bias, hmask)

        def finish(tile, s):
            r4, lb = tile
            planes = plane_set(r4)
            rows = block_rows(lb)
            stacked = lambda ref, c: jnp.concatenate(
                [ref[c, e, p, rows] for e in range(2) for p in planes], axis=0)
            vw = jnp.concatenate([mid_window(vc_ref, vp_ref, vx_ref, p, lb) for p in planes],
                                 axis=0)
            m_old = [stacked(m_ref, c) for c in range(HW // LANES)]
            for c, new in enumerate(_softmax_pv(s, vw, low, m_old)):
                acc_old = jnp.concatenate([acc_ref[p, rows, chunk(c)] for p in planes], axis=0)
                l_old = jnp.concatenate([l_ref[c, p, rows] for p in planes], axis=0)
                m_m, l_m, acc_m = _merge(new, m_old[c], l_old, acc_old, low)
                for j, p in enumerate(planes):
                    piece = slice(j * MID_ROWS, (j + 1) * MID_ROWS)
                    acc_ref[p, rows, chunk(c)] = acc_m[piece]
                    l_ref[c, p, rows] = l_m[piece]
                    for e in range(2):
                        head_piece = slice(e * TL + j * MID_ROWS, e * TL + (j + 1) * MID_ROWS)
                        m_ref[c, e, p, rows] = m_m[head_piece]

        per_body = TILE_GROUP // (TL // MID_ROWS)
        pipelined([(g * per_body + i, lb) for i in range(per_body)
                   for lb in range(TL // MID_ROWS)], score, finish)
        return carry

    lax.fori_loop(0, MID_DIL * (TL // MID_ROWS) // TILE_GROUP, body4, 0)

    n_tiles = SUPER // TL
    for ext_ref, prev, cur, nxt in ((kne_ref, knp_ref, knc_ref, knx_ref),
                                    (vne_ref, vnp_ref, vnc_ref, vnx_ref)):
        ext_ref[0:BAND_HALF] = prev[0]
        ext_ref[BAND_HALF:BAND_HALF + SUPER] = cur[0]
        ext_ref[BAND_HALF + SUPER:] = nxt[0]
    band1 = stacked_bias(band_ref[2])

    def body1(g, carry):
        tiles = [g * TILE_GROUP + i for i in range(TILE_GROUP)]
        tile_rows = lambda t: pl.ds(pl.multiple_of(t * ONE_ROWS, ONE_ROWS), ONE_ROWS)
        tile_keys = lambda t: pl.ds(pl.multiple_of(t * TL, TL), TK)

        def tile_scores(t):
            lo = jnp.where(first & (t == 0), BAND_HALF, 0)
            hi = jnp.where(last & (t == n_tiles - 1), TK - BAND_HALF, TK)
            bias = band1 + jnp.where((col < lo) | (col >= hi), NEG_INF, 0.0)
            q = q_ref[0, :, tile_rows(t), :].reshape(TL, HW).astype(BF16)
            return _scores(q, kne_ref[tile_keys(t), :], bias, hmask)

        def emit(t, merged):
            tok = jnp.dot(unperm_ref[...], merged, preferred_element_type=F32)
            o_ref[0, pl.ds(pl.multiple_of(t * TL, TL), TL), :] = tok.astype(BF16)

        pending = []

        def finish(t, s):
            rows = tile_rows(t)
            stacked = lambda ref, c: jnp.concatenate(
                [ref[c, e, :, rows, :].reshape(TL, LANES) for e in range(2)], axis=0)
            m_old = [stacked(m_ref, c) for c in range(HW // LANES)]
            outs = []
            for c, new in enumerate(_softmax_pv(s, vne_ref[tile_keys(t), :], low, m_old)):
                acc_old = acc_ref[:, rows, chunk(c)].reshape(TL, LANES)
                l_old = l_ref[c, :, rows, :].reshape(TL, LANES)
                _, l_m, acc_m = _merge(new, m_old[c], l_old, acc_old, low)
                outs.append((acc_m / l_m).astype(BF16))
            if pending:
                emit(*pending.pop())
            pending.append((t, jnp.concatenate(outs, axis=1)))

        pipelined(tiles, tile_scores, finish)
        emit(*pending.pop())
        return carry

    lax.fori_loop(0, n_tiles // TILE_GROUP, body1, 0)


def _dil_attn(q16, k16, v16, kn, vn, band, unperm):
    B, S, _ = kn.shape
    n_half = S // PLANES // BAND_HALF
    n_tok_half = S // BAND_HALF
    per = TL // BAND_HALF
    per_tok = SUPER // BAND_HALF
    pcur = pl.BlockSpec((1, PLANES, TL, HW), lambda b, s, hh: (b, 0, s, hh))
    pprev = pl.BlockSpec((1, PLANES, BAND_HALF, HW),
                         lambda b, s, hh: (b, 0, jnp.maximum(s * per - 1, 0), hh))
    pnext = pl.BlockSpec((1, PLANES, BAND_HALF, HW),
                         lambda b, s, hh: (b, 0, jnp.minimum((s + 1) * per, n_half - 1), hh))
    tcur = pl.BlockSpec((1, SUPER, HW), lambda b, s, hh: (b, s, hh))
    tprev = pl.BlockSpec((1, BAND_HALF, HW),
                         lambda b, s, hh: (b, jnp.maximum(s * per_tok - 1, 0), hh))
    tnext = pl.BlockSpec((1, BAND_HALF, HW),
                         lambda b, s, hh: (b, jnp.minimum((s + 1) * per_tok, n_tok_half - 1), hh))
    return pl.pallas_call(
        _dil_attn_kernel,
        grid=(B, S // SUPER, ATTN_W // HW),
        in_specs=[pcur, pcur, pprev, pnext, pcur, pprev, pnext,
                  tcur, tprev, tnext, tcur, tprev, tnext,
                  _const_spec((3, TL, TK)), _const_spec((TL, TL))],
        out_specs=tcur,
        out_shape=jax.ShapeDtypeStruct((B, S, ATTN_W), BF16),
        scratch_shapes=[pltpu.VMEM((PLANES, TL, HW), F32),
                        pltpu.VMEM((HW // LANES, 2, PLANES, TL, LANES), F32),
                        pltpu.VMEM((HW // LANES, PLANES, TL, LANES), F32),
                        pltpu.VMEM((SUPER + 2 * BAND_HALF, HW), BF16),
                        pltpu.VMEM((SUPER + 2 * BAND_HALF, HW), BF16)],
        compiler_params=_cparams(3),
        name="dil_attn",
    )(q16, k16, k16, k16, v16, v16, v16, kn, kn, kn, vn, vn, vn, band, unperm)


def _conv_rolls(buf):
    rows = buf.shape[0]
    return [buf if shift == 0 else pltpu.roll(buf, rows - shift, 0) for shift in range(SUBLANES)]


def _conv_rows(rolled, lo_row, n_rows, w_ref, b_ref, g_ref, beta_ref):
    acc = jnp.zeros((n_rows, CONV_W), F32) + b_ref[...]
    base = HALO - CONV_K // 2
    for tap in range(CONV_K):
        shift = (base + tap) % SUBLANES
        lo = lo_row + base + tap - shift
        acc = acc + w_ref[tap:tap + 1, :] * rolled[shift][lo:lo + n_rows]
    mu = jnp.mean(acc, axis=-1, keepdims=True)
    d = acc - mu
    var = jnp.mean(d * d, axis=-1, keepdims=True)
    z = d * lax.rsqrt(var + NORM_EPS) * g_ref[...] + beta_ref[...]
    return z * _sigmoid(z)


N_CAST = 11
UP_CAST = 2 * D_FF // N_CAST
DOWN_CAST = D_FF // N_CAST
OUT_CAST = LANES


def _out_ffn_kernel(a_ref, ap_ref, ax_ref, c_ref, cp_ref, cx_ref, m_ref, mp_ref, mx_ref,
                    x_ref, xp_ref, xx_ref, wo32_ref, g_ref, wu32_ref, dw_ref, db_ref, wd32_ref,
                    o_ref, gate_ref, wo_ref, wu_ref, wd_ref, *, n_tiles):
    step = pl.program_id(0)

    for c in range(N_CAST):
        @pl.when(step == c)
        def _(c=c):
            for half in range(UP_CAST // FF_CH):
                src = c * UP_CAST + half * FF_CH
                j, is_up = (src // FF_CH, 0) if src < D_FF else ((src - D_FF) // FF_CH, 1)
                dst = (2 * j + is_up) * FF_CH
                wu_ref[:, dst:dst + FF_CH] = wu32_ref[:, half * FF_CH:(half + 1) * FF_CH].astype(BF16)
            wd_ref[c * DOWN_CAST:(c + 1) * DOWN_CAST, :] = wd32_ref[...].astype(BF16)
            if (c + 1) * OUT_CAST <= D_MODEL:
                wo_ref[c * OUT_CAST:(c + 1) * OUT_CAST, :] = wo32_ref[...].astype(BF16)

    @pl.when(step >= N_CAST)
    def _():
        _out_ffn_tile(a_ref, ap_ref, ax_ref, c_ref, cp_ref, cx_ref, m_ref, mp_ref, mx_ref,
                      x_ref, xp_ref, xx_ref, wo_ref, g_ref, wu_ref, dw_ref, db_ref, wd_ref,
                      o_ref, gate_ref, lax.rem(step - N_CAST, n_tiles), n_tiles)


def _out_ffn_tile(a_ref, ap_ref, ax_ref, c_ref, cp_ref, cx_ref, m_ref, mp_ref, mx_ref,
                  x_ref, xp_ref, xx_ref, wo_ref, g_ref, wu_ref, dw_ref, db_ref, wd_ref,
                  o_ref, gate_ref, t, nt):
    has_next, has_prev = t < nt - 1, t > 0
    rows = TF + HALO

    def ext_rows(cur, nxt, prev):
        halo = jnp.concatenate([nxt[0, :SUBLANES].astype(F32), prev[0, SUBLANES:].astype(F32)],
                               axis=0)
        return jnp.concatenate([cur[0], halo.astype(cur.dtype)], axis=0)

    rid = lax.broadcasted_iota(jnp.int32, (rows, 1), 0)
    in_seq = ((rid < TF) | ((rid < TF + SUBLANES) & has_next)
              | ((rid >= TF + SUBLANES) & has_prev))
    mixed = jnp.concatenate([ext_rows(a_ref, ax_ref, ap_ref), ext_rows(c_ref, cx_ref, cp_ref),
                             ext_rows(m_ref, mx_ref, mp_ref)], axis=-1)
    mixed = jnp.where(in_seq, mixed, jnp.zeros_like(mixed))
    h = (jnp.where(in_seq, ext_rows(x_ref, xx_ref, xp_ref), 0.0)
         + jnp.dot(mixed, wo_ref[...], preferred_element_type=F32))
    ext = _rms_rows(h, g_ref[...]).astype(BF16)

    def conv3(f, lo):
        w = dw_ref[:, lo:lo + FF_CH]
        y = (w[0:1] * pltpu.roll(f, 1, 0) + w[1:2] * f + w[2:3] * pltpu.roll(f, rows - 1, 0))
        return y[:TF] + db_ref[:, lo:lo + FF_CH]

    n_chunks = D_FF // FF_CH
    out = h[:TF]
    for j in range(n_chunks):
        lo_g, lo_u = j * FF_CH, D_FF + j * FF_CH
        f = jnp.dot(ext, wu_ref[:, 2 * lo_g:2 * lo_g + 2 * FF_CH], preferred_element_type=F32)
        if j == n_chunks - 1:
            out = out + jnp.dot(gate_ref[:, :lo_g], wd_ref[:lo_g, :], preferred_element_type=F32)
        fg, fu = conv3(f[:, :FF_CH], lo_g), conv3(f[:, FF_CH:], lo_u)
        gate_ref[:, lo_g:lo_g + FF_CH] = (fg * _sigmoid(fg) * fu).astype(BF16)
    o_ref[0] = out + jnp.dot(gate_ref[:, D_FF - FF_CH:], wd_ref[D_FF - FF_CH:, :],
                             preferred_element_type=F32)


def _out_ffn(attn, cb, mo, x, w_out, g, w_up, dw_w, dw_b, w_down):
    B, S, _ = x.shape
    nt = S // TF
    per = TF // HALO

    def tile_of(step):
        i = jnp.maximum(step - N_CAST, 0)
        return i // nt, i % nt

    def with_halos(width):
        def cur(step):
            b, t = tile_of(step)
            return b, t, 0

        def prev(step):
            b, t = tile_of(step)
            return b, jnp.maximum(t * per - 1, 0), 0

        def nxt(step):
            b, t = tile_of(step)
            return b, jnp.minimum((t + 1) * per, nt * per - 1), 0

        return [pl.BlockSpec((1, TF, width), cur), pl.BlockSpec((1, HALO, width), prev),
                pl.BlockSpec((1, HALO, width), nxt)]

    cast_chunk = lambda limit: (lambda step: jnp.minimum(step, limit - 1))
    out_i, up_i, down_i = cast_chunk(D_MODEL // OUT_CAST), cast_chunk(N_CAST), cast_chunk(N_CAST)
    return pl.pallas_call(
        functools.partial(_out_ffn_kernel, n_tiles=nt),
        grid=(N_CAST + B * nt,),
        in_specs=with_halos(ATTN_W) + with_halos(CONV_W) + with_halos(MEM_W)
        + with_halos(D_MODEL)
        + [pl.BlockSpec((OUT_CAST, D_MODEL), lambda s: (out_i(s), 0)), _const_spec((1, D_MODEL)),
           pl.BlockSpec((D_MODEL, UP_CAST), lambda s: (0, up_i(s))),
           _const_spec((FFN_CONV_K, 2 * D_FF)), _const_spec((1, 2 * D_FF)),
           pl.BlockSpec((DOWN_CAST, D_MODEL), lambda s: (down_i(s), 0))],
        out_specs=pl.BlockSpec((1, TF, D_MODEL), lambda s: (*tile_of(s), 0)),
        out_shape=jax.ShapeDtypeStruct((B, S, D_MODEL), F32),
        scratch_shapes=[pltpu.VMEM((TF, D_FF), BF16), pltpu.VMEM((D_MODEL, D_MODEL), BF16),
                        pltpu.VMEM((D_MODEL, 2 * D_FF), BF16), pltpu.VMEM((D_FF, D_MODEL), BF16)],
        compiler_params=pltpu.CompilerParams(dimension_semantics=("arbitrary",),
                                             vmem_limit_bytes=VMEM_LIMIT),
        name="out_ffn",
    )(attn, attn, attn, cb, cb, cb, mo, mo, mo, x, x, x, w_out, g, w_up, dw_w, dw_b, w_down)


def _rope_tables():
    inv_freq = ROPE_THETA ** (-jnp.arange(0, ROT_DIM, 2, dtype=F32) / ROT_DIM)
    invf = jnp.broadcast_to(inv_freq[:, None], (SUBLANES, LANES))
    half = ROT_DIM // 2
    lane = np.arange(LANES) % HEAD_DIM
    lanes = np.zeros((SUBLANES, LANES), np.float32)
    lanes[0] = lane >= ROT_DIM
    lanes[1] = lane < half
    expand = np.zeros((LANES, 2 * LANES), np.float32)
    for l in range(LANES):
        if lane[l] < ROT_DIM:
            j = lane[l] % half
            expand[[j, half + j], l] = 1.0
            expand[[2 * half + j, 3 * half + j], LANES + l] = -1.0 if lane[l] < half else 1.0
    return invf, jnp.asarray(lanes), jnp.asarray(expand, BF16)


def _group_sum_matrix():
    idx = np.arange(MXU_DIM) // HEAD_DIM
    return jnp.asarray(idx[:, None] == idx[None, :], BF16)


def _plane_perm(n):
    out = np.arange(n)
    src = (out % (n // PLANES)) * PLANES + out // (n // PLANES)
    return np.asarray(src[:, None] == np.arange(n)[None, :], np.float32)


def kernel(x, mem, positions, mix_norm_g, mem_norm_g, w_in, w_mem_kv, q_norm_g, k_norm_g, mq_norm_g, mk_norm_g, conv_dw_w, conv_dw_b, conv_ln_g, conv_ln_b, w_out, ffn_norm_g, w_up, ffn_dw_w, ffn_dw_b, w_down):
    B, S, _ = x.shape
    depth = w_in.shape[0]
    pos_row = positions.reshape(B, 1, S)
    invf, lane_tab, expand = _rope_tables()
    gsum = _group_sum_matrix()
    perm = jnp.asarray(_plane_perm(TM), BF16)
    unperm = jnp.asarray(_plane_perm(TL).T, BF16)
    band = _band_tables()
    row = lambda a: a.reshape(1, -1)
    h = x
    for l in range(depth):
        kmt, vm = _mem_kv(mem, row(mem_norm_g[l]), w_mem_kv[l],
                          row(jnp.tile(mk_norm_g[l], MEM_HEADS)), gsum)
        q16, k16, v16, kn, vn, cb, mo = _in_proj(
            h, pos_row, row(mix_norm_g[l]), w_in[l],
            row(jnp.tile(q_norm_g[l], ATTN_HEADS)), row(jnp.tile(k_norm_g[l], ATTN_HEADS)),
            row(jnp.tile(mq_norm_g[l], MEM_HEADS)), gsum, invf, lane_tab, expand, perm,
            conv_dw_w[l], row(conv_dw_b[l]), row(conv_ln_g[l]), row(conv_ln_b[l]), kmt, vm)
        attn = _dil_attn(q16, k16, v16, kn, vn, band, unperm)
        h = _out_ffn(attn, cb, mo, h, w_out[l], row(ffn_norm_g[l]), w_up[l], ffn_dw_w[l],
                     row(ffn_dw_b[l]), w_down[l])
    return h
```

```python
import functools

import numpy as np
import jax
import jax.numpy as jnp
from jax import lax
from jax.experimental import pallas as pl
from jax.experimental.pallas import tpu as pltpu

F32 = jnp.float32
BF16 = jnp.bfloat16

D_MODEL = 1024
HEAD_DIM = 64
ATTN_HEADS = 8
ATTN_W = ATTN_HEADS * HEAD_DIM
CONV_W = 256
MEM_HEADS = 4
MEM_W = MEM_HEADS * HEAD_DIM
N_MEM = 256
PLANES = 16
MID_DIL = 4
BAND_HALF = 64
ROPE_THETA = 500000.0
ROT_DIM = HEAD_DIM // 4
CONV_K = 31
FFN_CONV_K = 3
D_FF = 2816
NORM_EPS = 1e-6
NEG_INF = -1e30
SM_SCALE = HEAD_DIM ** -0.5

LANES = 128
SUBLANES = 8
MXU_DIM = 256
BF16_ROWS = 16
VMEM_LIMIT = 56 * 1024 * 1024

TM = 512
CONV_ROWS = 64
TF = 512
TL = 128
TK = TL + 2 * BAND_HALF
SUPER = PLANES * TL
FF_CH = MXU_DIM
HALO = BF16_ROWS


def _cparams(n_axes):
    return pltpu.CompilerParams(dimension_semantics=("parallel",) * n_axes,
                                vmem_limit_bytes=VMEM_LIMIT)


def _const_spec(shape):
    return pl.BlockSpec(shape, lambda *_: (0,) * len(shape), pipeline_mode=pl.Buffered(1))


def _rms_rows(x, g):
    r = lax.rsqrt(jnp.mean(x * x, axis=-1, keepdims=True) + NORM_EPS)
    return x * r * g


def _head_sumsq(t, gsum):
    return [jnp.dot((tc * tc).astype(BF16), gsum, preferred_element_type=F32)
            for tc in (t[:, c * MXU_DIM:(c + 1) * MXU_DIM] for c in range(t.shape[1] // MXU_DIM))]


def _head_scale(t, sumsq, gain):
    outs = []
    for c, ssum in enumerate(sumsq):
        sl = slice(c * MXU_DIM, (c + 1) * MXU_DIM)
        outs.append(t[:, sl] * lax.rsqrt(ssum * (1.0 / HEAD_DIM) + NORM_EPS) * gain[:, sl])
    return outs


def _head_norm(t, gain, gsum):
    return _head_scale(t, _head_sumsq(t, gsum), gain)


def _sigmoid(x):
    return 1.0 / (1.0 + jnp.exp(-x))


IN_COLS = 3 * ATTN_W + 2 * CONV_W + MEM_W
IN_CAST = MXU_DIM
N_CAST_IN = IN_COLS // IN_CAST


def _in_proj_kernel(x_ref, xp_ref, xx_ref, pos_ref, g_ref, w32_ref, gq_ref, gk_ref, gm_ref,
                    gsum_ref, invf_ref, lane_ref, expand_ref, perm_ref,
                    cw_ref, cbias_ref, cg_ref, cbeta_ref, kmt_ref, vm_ref,
                    q16_ref, k16_ref, v16_ref, kn_ref, vn_ref, cb_ref, mo_ref,
                    w_ref, *, n_tiles):
    step = pl.program_id(0)

    for c in range(N_CAST_IN):
        @pl.when(step == c)
        def _(c=c):
            w_ref[:, c * IN_CAST:(c + 1) * IN_CAST] = w32_ref[...].astype(BF16)

    @pl.when(step >= N_CAST_IN)
    def _():
        _in_proj_tile(x_ref, xp_ref, xx_ref, pos_ref, g_ref, w_ref, gq_ref, gk_ref, gm_ref,
                      gsum_ref, invf_ref, lane_ref, expand_ref, perm_ref,
                      cw_ref, cbias_ref, cg_ref, cbeta_ref, kmt_ref, vm_ref,
                      q16_ref, k16_ref, v16_ref, kn_ref, vn_ref, cb_ref, mo_ref,
                      lax.rem(step - N_CAST_IN, n_tiles), n_tiles)


def _in_proj_tile(x_ref, xp_ref, xx_ref, pos_ref, g_ref, w_ref, gq_ref, gk_ref, gm_ref,
                  gsum_ref, invf_ref, lane_ref, expand_ref, perm_ref,
                  cw_ref, cbias_ref, cg_ref, cbeta_ref, kmt_ref, vm_ref,
                  q16_ref, k16_ref, v16_ref, kn_ref, vn_ref, cb_ref, mo_ref,
                  t, nt):
    has_prev, has_next = t > 0, t < nt - 1
    hn = _rms_rows(x_ref[0], g_ref[...]).astype(BF16)

    hn_halo = _rms_rows(jnp.concatenate([xp_ref[0], xx_ref[0]], axis=0), g_ref[...]).astype(BF16)
    hn_ext = jnp.concatenate([hn_halo[:HALO], hn, hn_halo[HALO:]], axis=0)
    cols = lambda lo, n: w_ref[:, lo:lo + n]
    q = jnp.dot(hn, cols(0, ATTN_W), preferred_element_type=F32)
    k = jnp.dot(hn, cols(ATTN_W, ATTN_W), preferred_element_type=F32)
    v = jnp.dot(hn, cols(2 * ATTN_W, ATTN_W), preferred_element_type=F32)
    glu = jnp.dot(hn_ext, cols(3 * ATTN_W, 2 * CONV_W), preferred_element_type=F32)
    qm = jnp.dot(hn, cols(3 * ATTN_W + 2 * CONV_W, MEM_W), preferred_element_type=F32)

    ang = invf_ref[:, 0:1] * pos_ref[0].astype(F32)

    def hi_lo(v):
        hi = v.astype(BF16).astype(F32)
        return [hi, (v - hi).astype(BF16).astype(F32)]

    tab = jnp.concatenate(hi_lo(jnp.cos(ang)) + hi_lo(jnp.sin(ang))
                          + [jnp.zeros((LANES - 4 * SUBLANES, TM), F32)], axis=0)
    cs = jnp.dot(tab.T.astype(BF16), expand_ref[...], preferred_element_type=F32)
    cosv = cs[:, :LANES] + lane_ref[0:1, :]
    sinv = cs[:, LANES:]
    first_half = lane_ref[1:2, :] > 0.5

    def rotary(chunks, scale):
        out = []
        for t in chunks:
            for s in range(MXU_DIM // LANES):
                xc = t[:, s * LANES:(s + 1) * LANES]
                partner = jnp.where(first_half, pltpu.roll(xc, LANES - ROT_DIM // 2, 1),
                                    pltpu.roll(xc, ROT_DIM // 2, 1))
                out.append((xc * cosv + partner * sinv) * scale)
        return out

    gsum = gsum_ref[...]
    q_ss, k_ss, qm_ss = _head_sumsq(q, gsum), _head_sumsq(k, gsum), _head_sumsq(qm, gsum)
    q_chunks = rotary(_head_scale(q, q_ss, gq_ref[...]), SM_SCALE)
    k_chunks = rotary(_head_scale(k, k_ss, gk_ref[...]), 1.0)
    qb = jnp.concatenate(q_chunks, axis=1).astype(BF16)
    kb = jnp.concatenate(k_chunks, axis=1).astype(BF16)
    vb = v.astype(BF16)
    kn_ref[0] = kb
    vn_ref[0] = vb
    qkv = jnp.dot(perm_ref[...], jnp.concatenate([qb, kb, vb], axis=1),
                  preferred_element_type=F32)
    rows = TM // PLANES
    for r in range(PLANES):
        blk = qkv[r * rows:(r + 1) * rows]
        q16_ref[0, r] = blk[:, 0:ATTN_W]
        k16_ref[0, r] = blk[:, ATTN_W:2 * ATTN_W].astype(BF16)
        v16_ref[0, r] = blk[:, 2 * ATTN_W:].astype(BF16)

    (qmn,) = _head_scale(qm, qm_ss, gm_ref[...])
    mo_ref[0] = _mem_attn((qmn * SM_SCALE).astype(BF16), kmt_ref, vm_ref)

    rid = lax.broadcasted_iota(jnp.int32, (TM + 2 * HALO, 1), 0)
    in_seq = ((rid >= HALO) | has_prev) & ((rid < TM + HALO) | has_next)
    cbuf = jnp.where(in_seq, glu[:, :CONV_W] * _sigmoid(glu[:, CONV_W:]), 0.0)
    rolled = _conv_rolls(cbuf)
    for lo in range(0, TM, CONV_ROWS):
        cb_ref[0, lo:lo + CONV_ROWS] = _conv_rows(rolled, lo, CONV_ROWS, cw_ref, cbias_ref,
                                                  cg_ref, cbeta_ref).astype(BF16)


def _in_proj(x, pos_row, g, w_in, gq, gk, gm, gsum, invf, lane_tab, expand, perm,
             conv_w, conv_b, conv_g, conv_beta, kmt, vm):
    B, S, _ = x.shape
    nt = S // TM
    per = TM // HALO

    def tile_of(step):
        i = jnp.maximum(step - N_CAST_IN, 0)
        return i // nt, i % nt

    def at_tile(fn):
        return lambda step: fn(*tile_of(step))

    tok = lambda w: pl.BlockSpec((1, TM, w), at_tile(lambda b, t: (b, t, 0)))
    xprev = pl.BlockSpec((1, HALO, D_MODEL),
                         at_tile(lambda b, t: (b, jnp.maximum(t * per - 1, 0), 0)))
    xnext = pl.BlockSpec((1, HALO, D_MODEL),
                         at_tile(lambda b, t: (b, jnp.minimum((t + 1) * per, nt * per - 1), 0)))
    plane = pl.BlockSpec((1, PLANES, TM // PLANES, ATTN_W), at_tile(lambda b, t: (b, 0, t, 0)))
    plane_shape = (B, PLANES, S // PLANES, ATTN_W)
    per_batch = lambda r, w: pl.BlockSpec((1, r, w), at_tile(lambda b, t: (b, 0, 0)))
    return pl.pallas_call(
        functools.partial(_in_proj_kernel, n_tiles=nt),
        grid=(N_CAST_IN + B * nt,),
        in_specs=[tok(D_MODEL), xprev, xnext,
                  pl.BlockSpec((1, 1, TM), at_tile(lambda b, t: (b, 0, t))),
                  _const_spec((1, D_MODEL)),
                  pl.BlockSpec((D_MODEL, IN_CAST),
                               lambda step: (0, jnp.minimum(step, N_CAST_IN - 1))),
                  _const_spec((1, ATTN_W)), _const_spec((1, ATTN_W)), _const_spec((1, MEM_W)),
                  _const_spec((MXU_DIM, MXU_DIM)), _const_spec((SUBLANES, LANES)),
                  _const_spec((SUBLANES, LANES)), _const_spec((LANES, 2 * LANES)),
                  _const_spec((TM, TM)),
                  _const_spec((CONV_K, CONV_W)), _const_spec((1, CONV_W)),
                  _const_spec((1, CONV_W)), _const_spec((1, CONV_W)),
                  per_batch(MEM_W, N_MEM), per_batch(N_MEM, MEM_W)],
        out_specs=[plane, plane, plane, tok(ATTN_W), tok(ATTN_W), tok(CONV_W), tok(MEM_W)],
        out_shape=[jax.ShapeDtypeStruct(plane_shape, F32),
                   jax.ShapeDtypeStruct(plane_shape, BF16),
                   jax.ShapeDtypeStruct(plane_shape, BF16),
                   jax.ShapeDtypeStruct((B, S, ATTN_W), BF16),
                   jax.ShapeDtypeStruct((B, S, ATTN_W), BF16),
                   jax.ShapeDtypeStruct((B, S, CONV_W), BF16),
                   jax.ShapeDtypeStruct((B, S, MEM_W), BF16)],
        scratch_shapes=[pltpu.VMEM((D_MODEL, IN_COLS), BF16)],
        compiler_params=pltpu.CompilerParams(dimension_semantics=("arbitrary",),
                                             vmem_limit_bytes=VMEM_LIMIT),
        name="in_proj",
    )(x, x, x, pos_row, g, w_in, gq, gk, gm, gsum, invf, lane_tab, expand, perm,
      conv_w, conv_b, conv_g, conv_beta, kmt, vm)


def _mem_kv_kernel(mem_ref, g_ref, w_ref, gk_ref, gsum_ref, kmt_ref, vm_ref):
    mn = _rms_rows(mem_ref[0], g_ref[...]).astype(BF16)
    kv = jnp.dot(mn, w_ref[...].astype(BF16), preferred_element_type=F32)
    (km,) = _head_norm(kv[:, :MEM_W], gk_ref[...], gsum_ref[...])
    kmt_ref[0] = km.T.astype(BF16)
    vm_ref[0] = kv[:, MEM_W:].astype(BF16)


def _mem_kv(mem, g, w, gk, gsum):
    B = mem.shape[0]
    return pl.pallas_call(
        _mem_kv_kernel,
        grid=(B,),
        in_specs=[pl.BlockSpec((1, N_MEM, D_MODEL), lambda b: (b, 0, 0)),
                  _const_spec((1, D_MODEL)), _const_spec((D_MODEL, 2 * MEM_W)),
                  _const_spec((1, MEM_W)), _const_spec((MXU_DIM, MXU_DIM))],
        out_specs=[pl.BlockSpec((1, MEM_W, N_MEM), lambda b: (b, 0, 0)),
                   pl.BlockSpec((1, N_MEM, MEM_W), lambda b: (b, 0, 0))],
        out_shape=[jax.ShapeDtypeStruct((B, MEM_W, N_MEM), BF16),
                   jax.ShapeDtypeStruct((B, N_MEM, MEM_W), BF16)],
        compiler_params=_cparams(1),
        name="mem_kv",
    )(mem, g, w, gk, gsum)


def _head_masks():
    lane = lax.broadcasted_iota(jnp.int32, (1, LANES), 1)
    low = lane < HEAD_DIM
    return low, (jnp.where(low, 1.0, 0.0).astype(BF16), jnp.where(low, 0.0, 1.0).astype(BF16))


def _mem_attn(qm, kmt_ref, vm_ref):
    low, hmask = _head_masks()
    ones = jnp.ones((N_MEM, LANES), BF16)
    out = []
    for c in range(MEM_W // LANES):
        sl = slice(c * LANES, (c + 1) * LANES)
        qc = qm[:, sl]
        v_aug = jnp.concatenate([vm_ref[0, :, sl], ones], axis=1)
        halves = []
        for e in range(2):
            s = jnp.dot(qc * hmask[e], kmt_ref[0, sl, :], preferred_element_type=F32)
            m = jnp.max(s, axis=-1, keepdims=True)
            p = jnp.exp((s - m).astype(BF16))
            r = jnp.dot(p, v_aug, preferred_element_type=F32)
            halves.append(r[:, :LANES] / r[:, LANES:])
        out.append(jnp.where(low, halves[0], halves[1]).astype(BF16))
    return jnp.concatenate(out, axis=1)


HW = ATTN_W // 2
MID_ROWS = TL // MID_DIL
MID_KEYS = TK // MID_DIL
MID_LEAD = (MID_KEYS - MID_ROWS) // 2
ONE_ROWS = TL // PLANES
TILE_GROUP = PLANES


def _band_tables():
    rho = np.arange(TL)[:, None]
    kap = np.arange(TK)[None, :]
    d16 = kap - BAND_HALF - rho
    j, lq = rho // MID_ROWS, rho % MID_ROWS
    jk, lk = kap // MID_KEYS, kap % MID_KEYS
    d4 = MID_DIL * (lk - MID_LEAD - lq) + (jk - j)
    r, l1 = rho // ONE_ROWS, rho % ONE_ROWS
    d1 = kap - BAND_HALF - PLANES * l1 - r
    tabs = [np.where(np.abs(d) <= BAND_HALF, 0.0, NEG_INF) for d in (d16, d4, d1)]
    return jnp.asarray(np.stack(tabs), F32)


def _scores(q, kw, bias2, hmask):
    out = []
    for c in range(HW // LANES):
        sl = slice(c * LANES, (c + 1) * LANES)
        qc = q[:, sl]
        qs = jnp.concatenate([qc * hmask[0], qc * hmask[1]], axis=0)
        out.append(lax.dot_general(qs, kw[:, sl], (((1,), (1,)), ((), ())),
                                   preferred_element_type=F32) + bias2)
    return out


def _softmax_pv(scores, vw, low, m_old=None):
    ones = jnp.ones((TK, LANES), BF16)
    res = []
    for c, s in enumerate(scores):
        m = jnp.max(s, axis=-1, keepdims=True)
        if m_old is None:
            shift = m
        else:
            m = jnp.maximum(m, m_old[c])
            shift = jnp.concatenate([m] * (TK // LANES), axis=1)
        p = jnp.exp((s - shift).astype(BF16))
        v_aug = jnp.concatenate([vw[:, c * LANES:(c + 1) * LANES], ones], axis=1)
        r = jnp.dot(p, v_aug, preferred_element_type=F32)
        res.append((m, jnp.where(low, r[:TL, LANES:], r[TL:, LANES:]),
                    jnp.where(low, r[:TL, :LANES], r[TL:, :LANES])))
    return res


def _merge(new, m_old, l_old, acc_old, low):
    m, l_n, acc_n = new
    b = jnp.exp(m_old - m)
    b = jnp.where(low, b[:TL], b[TL:])
    return m, l_n + b * l_old, acc_n + b * acc_old


def _dil_attn_kernel(q_ref, kc_ref, kp_ref, kx_ref, vc_ref, vp_ref, vx_ref,
                     knc_ref, knp_ref, knx_ref, vnc_ref, vnp_ref, vnx_ref, band_ref, unperm_ref,
                     o_ref, acc_ref, m_ref, l_ref, kne_ref, vne_ref):
    st = pl.program_id(1)
    first, last = st == 0, st == pl.num_programs(1) - 1
    low, hmask = _head_masks()
    col = lax.broadcasted_iota(jnp.int32, (1, TK), 1)
    chunk = lambda c: slice(c * LANES, (c + 1) * LANES)

    def stacked_bias(band, col_idx=None, lo=0, hi=TK):
        if col_idx is not None:
            band = band + jnp.where((col_idx < lo) | (col_idx >= hi), NEG_INF, 0.0)
        return jnp.concatenate([band, band], axis=0)

    bias16 = stacked_bias(band_ref[0], col, jnp.where(first, BAND_HALF, 0),
                          jnp.where(last, TK - BAND_HALF, TK))

    def pipelined(tiles, score_fn, finish_fn):
        s_next = score_fn(tiles[0])
        for i, t in enumerate(tiles):
            s = s_next
            if i + 1 < len(tiles):
                s_next = score_fn(tiles[i + 1])
            finish_fn(t, s)

    def body16(g, carry):
        def score(r):
            kw = jnp.concatenate([kp_ref[0, r], kc_ref[0, r], kx_ref[0, r]], axis=0)
            return _scores(q_ref[0, r].astype(BF16), kw, bias16, hmask)

        def finish(r, s):
            vw = jnp.concatenate([vp_ref[0, r], vc_ref[0, r], vx_ref[0, r]], axis=0)
            for c, (m, l, acc) in enumerate(_softmax_pv(s, vw, low)):
                acc_ref[r, :, chunk(c)] = acc
                l_ref[c, r] = l
                for e in range(2):
                    m_ref[c, e, r] = jnp.broadcast_to(m[e * TL:(e + 1) * TL], (TL, LANES))

        pipelined([g * TILE_GROUP + i for i in range(TILE_GROUP)], score, finish)
        return carry

    lax.fori_loop(0, PLANES // TILE_GROUP, body16, 0)

    def mid_window(cur, prev, nxt, plane, lb):
        lo = lb * MID_ROWS - MID_LEAD
        if lo < 0:
            return jnp.concatenate([prev[0, plane, TL // 2 + lo:TL // 2],
                                    cur[0, plane, 0:lo + MID_KEYS]], axis=0)
        if lo + MID_KEYS > TL:
            return jnp.concatenate([cur[0, plane, lo:TL],
                                    nxt[0, plane, 0:lo + MID_KEYS - TL]], axis=0)
        return cur[0, plane, lo:lo + MID_KEYS]

    def body4(g, carry):
        block_rows = lambda lb: slice(lb * MID_ROWS, (lb + 1) * MID_ROWS)
        plane_set = lambda r4: [r4 + MID_DIL * j for j in range(MID_DIL)]

        def score(tile):
            r4, lb = tile
            planes = plane_set(r4)
            band = band_ref[1]
            if lb == 0:
                bias = stacked_bias(band, col % MID_KEYS, jnp.where(first, MID_LEAD, 0), MID_KEYS)
            elif lb == TL // MID_ROWS - 1:
                bias = stacked_bias(band, col % MID_KEYS, 0,
                                    jnp.where(last, MID_KEYS - MID_LEAD, MID_KEYS))
            else:
                bias = stacked_bias(band)
            q = jnp.concatenate([q_ref[0, p, block_rows(lb)] for p in planes],
                                axis=0).astype(BF16)
            kw = jnp.concatenate([mid_window(kc_ref, kp_ref, kx_ref, p, lb) for p in planes],
                                 axis=0)
            return _scores(q, kw, bias, hmask)

        def finish(tile, s):
            r4, lb = tile
            planes = plane_set(r4)
            rows = block_rows(lb)
            stacked = lambda ref, c: jnp.concatenate(
                [ref[c, e, p, rows] for e in range(2) for p in planes], axis=0)
            vw = jnp.concatenate([mid_window(vc_ref, vp_ref, vx_ref, p, lb) for p in planes],
                                 axis=0)
            m_old = [stacked(m_ref, c) for c in range(HW // LANES)]
            for c, new in enumerate(_softmax_pv(s, vw, low, m_old)):
                acc_old = jnp.concatenate([acc_ref[p, rows, chunk(c)] for p in planes], axis=0)
                l_old = jnp.concatenate([l_ref[c, p, rows] for p in planes], axis=0)
                m_m, l_m, acc_m = _merge(new, m_old[c], l_old, acc_old, low)
                for j, p in enumerate(planes):
                    piece = slice(j * MID_ROWS, (j + 1) * MID_ROWS)
                    acc_ref[p, rows, chunk(c)] = acc_m[piece]
                    l_ref[c, p, rows] = l_m[piece]
                    for e in range(2):
                        head_piece = slice(e * TL + j * MID_ROWS, e * TL + (j + 1) * MID_ROWS)
                        m_ref[c, e, p, rows] = m_m[head_piece]

        per_body = TILE_GROUP // (TL // MID_ROWS)
        pipelined([(g * per_body + i, lb) for i in range(per_body)
                   for lb in range(TL // MID_ROWS)], score, finish)
        return carry

    lax.fori_loop(0, MID_DIL * (TL // MID_ROWS) // TILE_GROUP, body4, 0)

    n_tiles = SUPER // TL
    for ext_ref, prev, cur, nxt in ((kne_ref, knp_ref, knc_ref, knx_ref),
                                    (vne_ref, vnp_ref, vnc_ref, vnx_ref)):
        ext_ref[0:BAND_HALF] = prev[0]
        ext_ref[BAND_HALF:BAND_HALF + SUPER] = cur[0]
        ext_ref[BAND_HALF + SUPER:] = nxt[0]
    band1 = stacked_bias(band_ref[2])

    def body1(g, carry):
        tiles = [g * TILE_GROUP + i for i in range(TILE_GROUP)]
        tile_rows = lambda t: pl.ds(pl.multiple_of(t * ONE_ROWS, ONE_ROWS), ONE_ROWS)
        tile_keys = lambda t: pl.ds(pl.multiple_of(t * TL, TL), TK)

        def tile_scores(t):
            lo = jnp.where(first & (t == 0), BAND_HALF, 0)
            hi = jnp.where(last & (t == n_tiles - 1), TK - BAND_HALF, TK)
            bias = band1 + jnp.where((col < lo) | (col >= hi), NEG_INF, 0.0)
            q = q_ref[0, :, tile_rows(t), :].reshape(TL, HW).astype(BF16)
            return _scores(q, kne_ref[tile_keys(t), :], bias, hmask)

        def emit(t, merged):
            tok = jnp.dot(unperm_ref[...], merged, preferred_element_type=F32)
            o_ref[0, pl.ds(pl.multiple_of(t * TL, TL), TL), :] = tok.astype(BF16)

        pending = []

        def finish(t, s):
            rows = tile_rows(t)
            stacked = lambda ref, c: jnp.concatenate(
                [ref[c, e, :, rows, :].reshape(TL, LANES) for e in range(2)], axis=0)
            m_old = [stacked(m_ref, c) for c in range(HW // LANES)]
            outs = []
            for c, new in enumerate(_softmax_pv(s, vne_ref[tile_keys(t), :], low, m_old)):
                acc_old = acc_ref[:, rows, chunk(c)].reshape(TL, LANES)
                l_old = l_ref[c, :, rows, :].reshape(TL, LANES)
                _, l_m, acc_m = _merge(new, m_old[c], l_old, acc_old, low)
                outs.append((acc_m / l_m).astype(BF16))
            if pending:
                emit(*pending.pop())
            pending.append((t, jnp.concatenate(outs, axis=1)))

        pipelined(tiles, tile_scores, finish)
        emit(*pending.pop())
        return carry

    lax.fori_loop(0, n_tiles // TILE_GROUP, body1, 0)


def _dil_attn(q16, k16, v16, kn, vn, band, unperm):
    B, S, _ = kn.shape
    n_half = S // PLANES // BAND_HALF
    n_tok_half = S // BAND_HALF
    per = TL // BAND_HALF
    per_tok = SUPER // BAND_HALF
    pcur = pl.BlockSpec((1, PLANES, TL, HW), lambda b, s, hh: (b, 0, s, hh))
    pprev = pl.BlockSpec((1, PLANES, BAND_HALF, HW),
                         lambda b, s, hh: (b, 0, jnp.maximum(s * per - 1, 0), hh))
    pnext = pl.BlockSpec((1, PLANES, BAND_HALF, HW),
                         lambda b, s, hh: (b, 0, jnp.minimum((s + 1) * per, n_half - 1), hh))
    tcur = pl.BlockSpec((1, SUPER, HW), lambda b, s, hh: (b, s, hh))
    tprev = pl.BlockSpec((1, BAND_HALF, HW),
                         lambda b, s, hh: (b, jnp.maximum(s * per_tok - 1, 0), hh))
    tnext = pl.BlockSpec((1, BAND_HALF, HW),
                         lambda b, s, hh: (b, jnp.minimum((s + 1) * per_tok, n_tok_half - 1), hh))
    return pl.pallas_call(
        _dil_attn_kernel,
        grid=(B, S // SUPER, ATTN_W // HW),
        in_specs=[pcur, pcur, pprev, pnext, pcur, pprev, pnext,
                  tcur, tprev, tnext, tcur, tprev, tnext,
                  _const_spec((3, TL, TK)), _const_spec((TL, TL))],
        out_specs=tcur,
        out_shape=jax.ShapeDtypeStruct((B, S, ATTN_W), BF16),
        scratch_shapes=[pltpu.VMEM((PLANES, TL, HW), F32),
                        pltpu.VMEM((HW // LANES, 2, PLANES, TL, LANES), F32),
                        pltpu.VMEM((HW // LANES, PLANES, TL, LANES), F32),
                        pltpu.VMEM((SUPER + 2 * BAND_HALF, HW), BF16),
                        pltpu.VMEM((SUPER + 2 * BAND_HALF, HW), BF16)],
        compiler_params=_cparams(3),
        name="dil_attn",
    )(q16, k16, k16, k16, v16, v16, v16, kn, kn, kn, vn, vn, vn, band, unperm)


def _conv_rolls(buf):
    rows = buf.shape[0]
    return [buf if shift == 0 else pltpu.roll(buf, rows - shift, 0) for shift in range(SUBLANES)]


def _conv_rows(rolled, lo_row, n_rows, w_ref, b_ref, g_ref, beta_ref):
    acc = jnp.zeros((n_rows, CONV_W), F32) + b_ref[...]
    base = HALO - CONV_K // 2
    for tap in range(CONV_K):
        shift = (base + tap) % SUBLANES
        lo = lo_row + base + tap - shift
        acc = acc + w_ref[tap:tap + 1, :] * rolled[shift][lo:lo + n_rows]
    mu = jnp.mean(acc, axis=-1, keepdims=True)
    d = acc - mu
    var = jnp.mean(d * d, axis=-1, keepdims=True)
    z = d * lax.rsqrt(var + NORM_EPS) * g_ref[...] + beta_ref[...]
    return z * _sigmoid(z)


N_CAST = D_FF // FF_CH
UP_CAST = 2 * D_FF // N_CAST
DOWN_CAST = D_FF // N_CAST
OUT_CAST = LANES
assert D_MODEL // OUT_CAST <= N_CAST and N_CAST * UP_CAST == 2 * D_FF and UP_CAST % FF_CH == 0


def _out_ffn_kernel(a_ref, ap_ref, ax_ref, c_ref, cp_ref, cx_ref, m_ref, mp_ref, mx_ref,
                    x_ref, xp_ref, xx_ref, wo32_ref, g_ref, wu32_ref, dw_ref, db_ref, wd32_ref,
                    o_ref, gate_ref, wo_ref, wu_ref, wd_ref, *, n_tiles):
    step = pl.program_id(0)

    for c in range(N_CAST):
        @pl.when(step == c)
        def _(c=c):
            for half in range(UP_CAST // FF_CH):
                src = c * UP_CAST + half * FF_CH
                j, is_up = (src // FF_CH, 0) if src < D_FF else ((src - D_FF) // FF_CH, 1)
                dst = (2 * j + is_up) * FF_CH
                wu_ref[:, dst:dst + FF_CH] = wu32_ref[:, half * FF_CH:(half + 1) * FF_CH].astype(BF16)
            wd_ref[c * DOWN_CAST:(c + 1) * DOWN_CAST, :] = wd32_ref[...].astype(BF16)
            if (c + 1) * OUT_CAST <= D_MODEL:
                wo_ref[c * OUT_CAST:(c + 1) * OUT_CAST, :] = wo32_ref[...].astype(BF16)

    @pl.when(step >= N_CAST)
    def _():
        _out_ffn_tile(a_ref, ap_ref, ax_ref, c_ref, cp_ref, cx_ref, m_ref, mp_ref, mx_ref,
                      x_ref, xp_ref, xx_ref, wo_ref, g_ref, wu_ref, dw_ref, db_ref, wd_ref,
                      o_ref, gate_ref, lax.rem(step - N_CAST, n_tiles), n_tiles)


def _out_ffn_tile(a_ref, ap_ref, ax_ref, c_ref, cp_ref, cx_ref, m_ref, mp_ref, mx_ref,
                  x_ref, xp_ref, xx_ref, wo_ref, g_ref, wu_ref, dw_ref, db_ref, wd_ref,
                  o_ref, gate_ref, t, nt):
    has_next, has_prev = t < nt - 1, t > 0
    rows = TF + HALO

    def ext_rows(cur, nxt, prev):
        halo = jnp.concatenate([nxt[0, :SUBLANES].astype(F32), prev[0, SUBLANES:].astype(F32)],
                               axis=0)
        return jnp.concatenate([cur[0], halo.astype(cur.dtype)], axis=0)

    rid = lax.broadcasted_iota(jnp.int32, (rows, 1), 0)
    in_seq = ((rid < TF) | ((rid < TF + SUBLANES) & has_next)
              | ((rid >= TF + SUBLANES) & has_prev))
    mixed = jnp.concatenate([ext_rows(a_ref, ax_ref, ap_ref), ext_rows(c_ref, cx_ref, cp_ref),
                             ext_rows(m_ref, mx_ref, mp_ref)], axis=-1)
    mixed = jnp.where(in_seq, mixed, jnp.zeros_like(mixed))
    h = (jnp.where(in_seq, ext_rows(x_ref, xx_ref, xp_ref), 0.0)
         + jnp.dot(mixed, wo_ref[...], preferred_element_type=F32))
    ext = _rms_rows(h, g_ref[...]).astype(BF16)

    def conv3(f, lo):
        w = dw_ref[:, lo:lo + FF_CH]
        y = (w[0:1] * pltpu.roll(f, 1, 0) + w[1:2] * f + w[2:3] * pltpu.roll(f, rows - 1, 0))
        return y[:TF] + db_ref[:, lo:lo + FF_CH]

    n_chunks = D_FF // FF_CH
    out = h[:TF]
    for j in range(n_chunks):
        lo_g, lo_u = j * FF_CH, D_FF + j * FF_CH
        f = jnp.dot(ext, wu_ref[:, 2 * lo_g:2 * lo_g + 2 * FF_CH], preferred_element_type=F32)
        if j == n_chunks - 1:
            out = out + jnp.dot(gate_ref[:, :lo_g], wd_ref[:lo_g, :], preferred_element_type=F32)
        fg, fu = conv3(f[:, :FF_CH], lo_g), conv3(f[:, FF_CH:], lo_u)
        gate_ref[:, lo_g:lo_g + FF_CH] = (fg * _sigmoid(fg) * fu).astype(BF16)
    o_ref[0] = out + jnp.dot(gate_ref[:, D_FF - FF_CH:], wd_ref[D_FF - FF_CH:, :],
                             preferred_element_type=F32)


def _out_ffn(attn, cb, mo, x, w_out, g, w_up, dw_w, dw_b, w_down):
    B, S, _ = x.shape
    nt = S // TF
    per = TF // HALO

    def tile_of(step):
        i = jnp.maximum(step - N_CAST, 0)
        return i // nt, i % nt

    def with_halos(width):
        def cur(step):
            b, t = tile_of(step)
            return b, t, 0

        def prev(step):
            b, t = tile_of(step)
            return b, jnp.maximum(t * per - 1, 0), 0

        def nxt(step):
            b, t = tile_of(step)
            return b, jnp.minimum((t + 1) * per, nt * per - 1), 0

        return [pl.BlockSpec((1, TF, width), cur), pl.BlockSpec((1, HALO, width), prev),
                pl.BlockSpec((1, HALO, width), nxt)]

    cast_chunk = lambda limit: (lambda step: jnp.minimum(step, limit - 1))
    out_i, up_i, down_i = cast_chunk(D_MODEL // OUT_CAST), cast_chunk(N_CAST), cast_chunk(N_CAST)
    return pl.pallas_call(
        functools.partial(_out_ffn_kernel, n_tiles=nt),
        grid=(N_CAST + B * nt,),
        in_specs=with_halos(ATTN_W) + with_halos(CONV_W) + with_halos(MEM_W)
        + with_halos(D_MODEL)
        + [pl.BlockSpec((OUT_CAST, D_MODEL), lambda s: (out_i(s), 0)), _const_spec((1, D_MODEL)),
           pl.BlockSpec((D_MODEL, UP_CAST), lambda s: (0, up_i(s))),
           _const_spec((FFN_CONV_K, 2 * D_FF)), _const_spec((1, 2 * D_FF)),
           pl.BlockSpec((DOWN_CAST, D_MODEL), lambda s: (down_i(s), 0))],
        out_specs=pl.BlockSpec((1, TF, D_MODEL), lambda s: (*tile_of(s), 0)),
        out_shape=jax.ShapeDtypeStruct((B, S, D_MODEL), F32),
        scratch_shapes=[pltpu.VMEM((TF, D_FF), BF16), pltpu.VMEM((D_MODEL, D_MODEL), BF16),
                        pltpu.VMEM((D_MODEL, 2 * D_FF), BF16), pltpu.VMEM((D_FF, D_MODEL), BF16)],
        compiler_params=pltpu.CompilerParams(dimension_semantics=("arbitrary",),
                                             vmem_limit_bytes=VMEM_LIMIT),
        name="out_ffn",
    )(attn, attn, attn, cb, cb, cb, mo, mo, mo, x, x, x, w_out, g, w_up, dw_w, dw_b, w_down)


def _rope_tables():
    inv_freq = ROPE_THETA ** (-jnp.arange(0, ROT_DIM, 2, dtype=F32) / ROT_DIM)
    invf = jnp.broadcast_to(inv_freq[:, None], (SUBLANES, LANES))
    half = ROT_DIM // 2
    lane = np.arange(LANES) % HEAD_DIM
    lanes = np.zeros((SUBLANES, LANES), np.float32)
    lanes[0] = lane >= ROT_DIM
    lanes[1] = lane < half
    expand = np.zeros((LANES, 2 * LANES), np.float32)
    for l in range(LANES):
        if lane[l] < ROT_DIM:
            j = lane[l] % half
            expand[[j, half + j], l] = 1.0
            expand[[2 * half + j, 3 * half + j], LANES + l] = -1.0 if lane[l] < half else 1.0
    return invf, jnp.asarray(lanes), jnp.asarray(expand, BF16)


def _group_sum_matrix():
    idx = np.arange(MXU_DIM) // HEAD_DIM
    return jnp.asarray(idx[:, None] == idx[None, :], BF16)


def _plane_perm(n):
    out = np.arange(n)
    src = (out % (n // PLANES)) * PLANES + out // (n // PLANES)
    return np.asarray(src[:, None] == np.arange(n)[None, :], np.float32)


def kernel(x, mem, positions, mix_norm_g, mem_norm_g, w_in, w_mem_kv, q_norm_g, k_norm_g, mq_norm_g, mk_norm_g, conv_dw_w, conv_dw_b, conv_ln_g, conv_ln_b, w_out, ffn_norm_g, w_up, ffn_dw_w, ffn_dw_b, w_down):
    B, S, _ = x.shape
    depth = w_in.shape[0]
    pos_row = positions.reshape(B, 1, S)
    invf, lane_tab, expand = _rope_tables()
    gsum = _group_sum_matrix()
    perm = jnp.asarray(_plane_perm(TM), BF16)
    unperm = jnp.asarray(_plane_perm(TL).T, BF16)
    band = _band_tables()
    row = lambda a: a.reshape(1, -1)
    h = x
    for l in range(depth):
        kmt, vm = _mem_kv(mem, row(mem_norm_g[l]), w_mem_kv[l],
                          row(jnp.tile(mk_norm_g[l], MEM_HEADS)), gsum)
        q16, k16, v16, kn, vn, cb, mo = _in_proj(
            h, pos_row, row(mix_norm_g[l]), w_in[l],
            row(jnp.tile(q_norm_g[l], ATTN_HEADS)), row(jnp.tile(k_norm_g[l], ATTN_HEADS)),
            row(jnp.tile(mq_norm_g[l], MEM_HEADS)), gsum, invf, lane_tab, expand, perm,
            conv_dw_w[l], row(conv_dw_b[l]), row(conv_ln_g[l]), row(conv_ln_b[l]), kmt, vm)
        attn = _dil_attn(q16, k16, v16, kn, vn, band, unperm)
        h = _out_ffn(attn, cb, mo, h, w_out[l], row(ffn_norm_g[l]), w_up[l], ffn_dw_w[l],
                     row(ffn_dw_b[l]), w_down[l])
    return h
```

```python
import functools

import numpy as np
import jax
import jax.numpy as jnp
from jax import lax
from jax.experimental import pallas as pl
from jax.experimental.pallas import tpu as pltpu

F32 = jnp.float32
BF16 = jnp.bfloat16

D_MODEL = 1024
HEAD_DIM = 64
ATTN_HEADS = 8
ATTN_W = ATTN_HEADS * HEAD_DIM
CONV_W = 256
MEM_HEADS = 4
MEM_W = MEM_HEADS * HEAD_DIM
N_MEM = 256
PLANES = 16
MID_DIL = 4
BAND_HALF = 64
ROPE_THETA = 500000.0
ROT_DIM = HEAD_DIM // 4
CONV_K = 31
FFN_CONV_K = 3
D_FF = 2816
NORM_EPS = 1e-6
NEG_INF = -1e30
SM_SCALE = HEAD_DIM ** -0.5

LANES = 128
SUBLANES = 8
MXU_DIM = 256
BF16_ROWS = 16
VMEM_LIMIT = 56 * 1024 * 1024

TM = 512
CONV_ROWS = 128
TF = 512
TL = 128
TK = TL + 2 * BAND_HALF
SUPER = PLANES * TL
FF_CH = MXU_DIM
HALO = BF16_ROWS


def _cparams(n_axes):
    return pltpu.CompilerParams(dimension_semantics=("parallel",) * n_axes,
                                vmem_limit_bytes=VMEM_LIMIT)


def _const_spec(shape):
    return pl.BlockSpec(shape, lambda *_: (0,) * len(shape), pipeline_mode=pl.Buffered(1))


def _rms_rows(x, g):
    r = lax.rsqrt(jnp.mean(x * x, axis=-1, keepdims=True) + NORM_EPS)
    return x * r * g


def _head_sumsq(t, gsum):
    return [jnp.dot((tc * tc).astype(BF16), gsum, preferred_element_type=F32)
            for tc in (t[:, c * MXU_DIM:(c + 1) * MXU_DIM] for c in range(t.shape[1] // MXU_DIM))]


def _head_scale(t, sumsq, gain):
    outs = []
    for c, ssum in enumerate(sumsq):
        sl = slice(c * MXU_DIM, (c + 1) * MXU_DIM)
        outs.append(t[:, sl] * lax.rsqrt(ssum * (1.0 / HEAD_DIM) + NORM_EPS) * gain[:, sl])
    return outs


def _head_norm(t, gain, gsum):
    return _head_scale(t, _head_sumsq(t, gsum), gain)


def _sigmoid(x):
    return 1.0 / (1.0 + jnp.exp(-x))


IN_COLS = 3 * ATTN_W + 2 * CONV_W + MEM_W
IN_CAST = 3 * MXU_DIM
N_CAST_IN = IN_COLS // IN_CAST


def _in_proj_kernel(x_ref, xp_ref, xx_ref, pos_ref, g_ref, w32_ref, gq_ref, gk_ref, gm_ref,
                    gsum_ref, invf_ref, lane_ref, expand_ref, perm_ref,
                    cw_ref, cbias_ref, cg_ref, cbeta_ref, kmt_ref, vm_ref,
                    q16_ref, k16_ref, v16_ref, kn_ref, vn_ref, cb_ref, mo_ref,
                    w_ref, *, n_tiles):
    step = pl.program_id(0)

    for c in range(N_CAST_IN):
        @pl.when(step == c)
        def _(c=c):
            w_ref[:, c * IN_CAST:(c + 1) * IN_CAST] = w32_ref[...].astype(BF16)

    @pl.when(step >= N_CAST_IN)
    def _():
        _in_proj_tile(x_ref, xp_ref, xx_ref, pos_ref, g_ref, w_ref, gq_ref, gk_ref, gm_ref,
                      gsum_ref, invf_ref, lane_ref, expand_ref, perm_ref,
                      cw_ref, cbias_ref, cg_ref, cbeta_ref, kmt_ref, vm_ref,
                      q16_ref, k16_ref, v16_ref, kn_ref, vn_ref, cb_ref, mo_ref,
                      lax.rem(step - N_CAST_IN, n_tiles), n_tiles)


def _in_proj_tile(x_ref, xp_ref, xx_ref, pos_ref, g_ref, w_ref, gq_ref, gk_ref, gm_ref,
                  gsum_ref, invf_ref, lane_ref, expand_ref, perm_ref,
                  cw_ref, cbias_ref, cg_ref, cbeta_ref, kmt_ref, vm_ref,
                  q16_ref, k16_ref, v16_ref, kn_ref, vn_ref, cb_ref, mo_ref,
                  t, nt):
    has_prev, has_next = t > 0, t < nt - 1
    hn = _rms_rows(x_ref[0], g_ref[...]).astype(BF16)

    hn_halo = _rms_rows(jnp.concatenate([xp_ref[0], xx_ref[0]], axis=0), g_ref[...]).astype(BF16)
    hn_ext = jnp.concatenate([hn_halo[:HALO], hn, hn_halo[HALO:]], axis=0)
    cols = lambda lo, n: w_ref[:, lo:lo + n]
    q = jnp.dot(hn, cols(0, ATTN_W), preferred_element_type=F32)
    k = jnp.dot(hn, cols(ATTN_W, ATTN_W), preferred_element_type=F32)
    v = jnp.dot(hn, cols(2 * ATTN_W, ATTN_W), preferred_element_type=F32)
    glu = jnp.dot(hn_ext, cols(3 * ATTN_W, 2 * CONV_W), preferred_element_type=F32)
    qm = jnp.dot(hn, cols(3 * ATTN_W + 2 * CONV_W, MEM_W), preferred_element_type=F32)

    ang = invf_ref[:, 0:1] * pos_ref[0].astype(F32)

    def hi_lo(v):
        hi = v.astype(BF16).astype(F32)
        return [hi, (v - hi).astype(BF16).astype(F32)]

    tab = jnp.concatenate(hi_lo(jnp.cos(ang)) + hi_lo(jnp.sin(ang))
                          + [jnp.zeros((LANES - 4 * SUBLANES, TM), F32)], axis=0)
    cs = jnp.dot(tab.T.astype(BF16), expand_ref[...], preferred_element_type=F32)
    cosv = cs[:, :LANES] + lane_ref[0:1, :]
    sinv = cs[:, LANES:]
    first_half = lane_ref[1:2, :] > 0.5

    def rotary(chunks, scale):
        out = []
        for t in chunks:
            for s in range(MXU_DIM // LANES):
                xc = t[:, s * LANES:(s + 1) * LANES]
                partner = jnp.where(first_half, pltpu.roll(xc, LANES - ROT_DIM // 2, 1),
                                    pltpu.roll(xc, ROT_DIM // 2, 1))
                out.append((xc * cosv + partner * sinv) * scale)
        return out

    gsum = gsum_ref[...]
    q_ss, k_ss, qm_ss = _head_sumsq(q, gsum), _head_sumsq(k, gsum), _head_sumsq(qm, gsum)
    q_chunks = rotary(_head_scale(q, q_ss, gq_ref[...]), SM_SCALE)
    k_chunks = rotary(_head_scale(k, k_ss, gk_ref[...]), 1.0)
    qb = jnp.concatenate(q_chunks, axis=1).astype(BF16)
    kb = jnp.concatenate(k_chunks, axis=1).astype(BF16)
    vb = v.astype(BF16)
    kn_ref[0] = kb
    vn_ref[0] = vb
    qkv = jnp.dot(perm_ref[...], jnp.concatenate([qb, kb, vb], axis=1),
                  preferred_element_type=F32)
    rows = TM // PLANES
    for r in range(PLANES):
        blk = qkv[r * rows:(r + 1) * rows]
        q16_ref[0, r] = blk[:, 0:ATTN_W]
        k16_ref[0, r] = blk[:, ATTN_W:2 * ATTN_W].astype(BF16)
        v16_ref[0, r] = blk[:, 2 * ATTN_W:].astype(BF16)

    (qmn,) = _head_scale(qm, qm_ss, gm_ref[...])
    mo_ref[0] = _mem_attn((qmn * SM_SCALE).astype(BF16), kmt_ref, vm_ref)

    rid = lax.broadcasted_iota(jnp.int32, (TM + 2 * HALO, 1), 0)
    in_seq = ((rid >= HALO) | has_prev) & ((rid < TM + HALO) | has_next)
    cbuf = jnp.where(in_seq, glu[:, :CONV_W] * _sigmoid(glu[:, CONV_W:]), 0.0)
    rolled = _conv_rolls(cbuf)
    for lo in range(0, TM, CONV_ROWS):
        cb_ref[0, lo:lo + CONV_ROWS] = _conv_rows(rolled, lo, CONV_ROWS, cw_ref, cbias_ref,
                                                  cg_ref, cbeta_ref).astype(BF16)


def _in_proj(x, pos_row, g, w_in, gq, gk, gm, gsum, invf, lane_tab, expand, perm,
             conv_w, conv_b, conv_g, conv_beta, kmt, vm):
    B, S, _ = x.shape
    nt = S // TM
    per = TM // HALO

    def tile_of(step):
        i = jnp.maximum(step - N_CAST_IN, 0)
        return i // nt, i % nt

    def at_tile(fn):
        return lambda step: fn(*tile_of(step))

    tok = lambda w: pl.BlockSpec((1, TM, w), at_tile(lambda b, t: (b, t, 0)))
    xprev = pl.BlockSpec((1, HALO, D_MODEL),
                         at_tile(lambda b, t: (b, jnp.maximum(t * per - 1, 0), 0)))
    xnext = pl.BlockSpec((1, HALO, D_MODEL),
                         at_tile(lambda b, t: (b, jnp.minimum((t + 1) * per, nt * per - 1), 0)))
    plane = pl.BlockSpec((1, PLANES, TM // PLANES, ATTN_W), at_tile(lambda b, t: (b, 0, t, 0)))
    plane_shape = (B, PLANES, S // PLANES, ATTN_W)
    per_batch = lambda r, w: pl.BlockSpec((1, r, w), at_tile(lambda b, t: (b, 0, 0)))
    return pl.pallas_call(
        functools.partial(_in_proj_kernel, n_tiles=nt),
        grid=(N_CAST_IN + B * nt,),
        in_specs=[tok(D_MODEL), xprev, xnext,
                  pl.BlockSpec((1, 1, TM), at_tile(lambda b, t: (b, 0, t))),
                  _const_spec((1, D_MODEL)),
                  pl.BlockSpec((D_MODEL, IN_CAST),
                               lambda step: (0, jnp.minimum(step, N_CAST_IN - 1))),
                  _const_spec((1, ATTN_W)), _const_spec((1, ATTN_W)), _const_spec((1, MEM_W)),
                  _const_spec((MXU_DIM, MXU_DIM)), _const_spec((SUBLANES, LANES)),
                  _const_spec((SUBLANES, LANES)), _const_spec((LANES, 2 * LANES)),
                  _const_spec((TM, TM)),
                  _const_spec((CONV_K, CONV_W)), _const_spec((1, CONV_W)),
                  _const_spec((1, CONV_W)), _const_spec((1, CONV_W)),
                  per_batch(MEM_W, N_MEM), per_batch(N_MEM, MEM_W)],
        out_specs=[plane, plane, plane, tok(ATTN_W), tok(ATTN_W), tok(CONV_W), tok(MEM_W)],
        out_shape=[jax.ShapeDtypeStruct(plane_shape, F32),
                   jax.ShapeDtypeStruct(plane_shape, BF16),
                   jax.ShapeDtypeStruct(plane_shape, BF16),
                   jax.ShapeDtypeStruct((B, S, ATTN_W), BF16),
                   jax.ShapeDtypeStruct((B, S, ATTN_W), BF16),
                   jax.ShapeDtypeStruct((B, S, CONV_W), BF16),
                   jax.ShapeDtypeStruct((B, S, MEM_W), BF16)],
        scratch_shapes=[pltpu.VMEM((D_MODEL, IN_COLS), BF16)],
        compiler_params=pltpu.CompilerParams(dimension_semantics=("arbitrary",),
                                             vmem_limit_bytes=VMEM_LIMIT),
        name="in_proj",
    )(x, x, x, pos_row, g, w_in, gq, gk, gm, gsum, invf, lane_tab, expand, perm,
      conv_w, conv_b, conv_g, conv_beta, kmt, vm)


def _mem_kv_kernel(mem_ref, g_ref, w_ref, gk_ref, gsum_ref, kmt_ref, vm_ref):
    mn = _rms_rows(mem_ref[0], g_ref[...]).astype(BF16)
    kv = jnp.dot(mn, w_ref[...].astype(BF16), preferred_element_type=F32)
    (km,) = _head_norm(kv[:, :MEM_W], gk_ref[...], gsum_ref[...])
    kmt_ref[0] = km.T.astype(BF16)
    vm_ref[0] = kv[:, MEM_W:].astype(BF16)


def _mem_kv(mem, g, w, gk, gsum):
    B = mem.shape[0]
    return pl.pallas_call(
        _mem_kv_kernel,
        grid=(B,),
        in_specs=[pl.BlockSpec((1, N_MEM, D_MODEL), lambda b: (b, 0, 0)),
                  _const_spec((1, D_MODEL)), _const_spec((D_MODEL, 2 * MEM_W)),
                  _const_spec((1, MEM_W)), _const_spec((MXU_DIM, MXU_DIM))],
        out_specs=[pl.BlockSpec((1, MEM_W, N_MEM), lambda b: (b, 0, 0)),
                   pl.BlockSpec((1, N_MEM, MEM_W), lambda b: (b, 0, 0))],
        out_shape=[jax.ShapeDtypeStruct((B, MEM_W, N_MEM), BF16),
                   jax.ShapeDtypeStruct((B, N_MEM, MEM_W), BF16)],
        compiler_params=_cparams(1),
        name="mem_kv",
    )(mem, g, w, gk, gsum)


def _head_masks():
    lane = lax.broadcasted_iota(jnp.int32, (1, LANES), 1)
    low = lane < HEAD_DIM
    return low, (jnp.where(low, 1.0, 0.0).astype(BF16), jnp.where(low, 0.0, 1.0).astype(BF16))


def _mem_attn(qm, kmt_ref, vm_ref):
    low, hmask = _head_masks()
    ones = jnp.ones((N_MEM, LANES), BF16)
    out = []
    for c in range(MEM_W // LANES):
        sl = slice(c * LANES, (c + 1) * LANES)
        qc = qm[:, sl]
        v_aug = jnp.concatenate([vm_ref[0, :, sl], ones], axis=1)
        halves = []
        for e in range(2):
            s = jnp.dot(qc * hmask[e], kmt_ref[0, sl, :], preferred_element_type=F32)
            m = jnp.max(s, axis=-1, keepdims=True)
            p = jnp.exp((s - m).astype(BF16))
            r = jnp.dot(p, v_aug, preferred_element_type=F32)
            halves.append(r[:, :LANES] / r[:, LANES:])
        out.append(jnp.where(low, halves[0], halves[1]).astype(BF16))
    return jnp.concatenate(out, axis=1)


HW = ATTN_W // 2
MID_ROWS = TL // MID_DIL
MID_KEYS = TK // MID_DIL
MID_LEAD = (MID_KEYS - MID_ROWS) // 2
ONE_ROWS = TL // PLANES
TILE_GROUP = PLANES


def _band_tables():
    rho = np.arange(TL)[:, None]
    kap = np.arange(TK)[None, :]
    d16 = kap - BAND_HALF - rho
    j, lq = rho // MID_ROWS, rho % MID_ROWS
    jk, lk = kap // MID_KEYS, kap % MID_KEYS
    d4 = MID_DIL * (lk - MID_LEAD - lq) + (jk - j)
    r, l1 = rho // ONE_ROWS, rho % ONE_ROWS
    d1 = kap - BAND_HALF - PLANES * l1 - r
    tabs = [np.where(np.abs(d) <= BAND_HALF, 0.0, NEG_INF) for d in (d16, d4, d1)]
    return jnp.asarray(np.stack(tabs), F32)


def _scores(q, kw, bias2, hmask):
    out = []
    for c in range(HW // LANES):
        sl = slice(c * LANES, (c + 1) * LANES)
        qc = q[:, sl]
        qs = jnp.concatenate([qc * hmask[0], qc * hmask[1]], axis=0)
        out.append(lax.dot_general(qs, kw[:, sl], (((1,), (1,)), ((), ())),
                                   preferred_element_type=F32) + bias2)
    return out


def _softmax_pv(scores, vw, low, m_old=None):
    ones = jnp.ones((TK, LANES), BF16)
    res = []
    for c, s in enumerate(scores):
        m = jnp.max(s, axis=-1, keepdims=True)
        if m_old is None:
            shift = m
        else:
            m = jnp.maximum(m, m_old[c])
            shift = jnp.concatenate([m] * (TK // LANES), axis=1)
        p = jnp.exp((s - shift).astype(BF16))
        v_aug = jnp.concatenate([vw[:, c * LANES:(c + 1) * LANES], ones], axis=1)
        r = jnp.dot(p, v_aug, preferred_element_type=F32)
        res.append((m, jnp.where(low, r[:TL, LANES:], r[TL:, LANES:]),
                    jnp.where(low, r[:TL, :LANES], r[TL:, :LANES])))
    return res


def _merge(new, m_old, l_old, acc_old, low):
    m, l_n, acc_n = new
    b = jnp.exp(m_old - m)
    b = jnp.where(low, b[:TL], b[TL:])
    return m, l_n + b * l_old, acc_n + b * acc_old


def _dil_attn_kernel(q_ref, kc_ref, kp_ref, kx_ref, vc_ref, vp_ref, vx_ref,
                     knc_ref, knp_ref, knx_ref, vnc_ref, vnp_ref, vnx_ref, band_ref, unperm_ref,
                     o_ref, acc_ref, m_ref, l_ref, kne_ref, vne_ref):
    st = pl.program_id(1)
    first, last = st == 0, st == pl.num_programs(1) - 1
    low, hmask = _head_masks()
    col = lax.broadcasted_iota(jnp.int32, (1, TK), 1)
    chunk = lambda c: slice(c * LANES, (c + 1) * LANES)

    def stacked_bias(band, col_idx=None, lo=0, hi=TK):
        if col_idx is not None:
            band = band + jnp.where((col_idx < lo) | (col_idx >= hi), NEG_INF, 0.0)
        return jnp.concatenate([band, band], axis=0)

    bias16 = stacked_bias(band_ref[0], col, jnp.where(first, BAND_HALF, 0),
                          jnp.where(last, TK - BAND_HALF, TK))

    def pipelined(tiles, score_fn, finish_fn):
        s_next = score_fn(tiles[0])
        for i, t in enumerate(tiles):
            s = s_next
            if i + 1 < len(tiles):
                s_next = score_fn(tiles[i + 1])
            finish_fn(t, s)

    def body16(g, carry):
        def score(r):
            kw = jnp.concatenate([kp_ref[0, r], kc_ref[0, r], kx_ref[0, r]], axis=0)
            return _scores(q_ref[0, r].astype(BF16), kw, bias16, hmask)

        def finish(r, s):
            vw = jnp.concatenate([vp_ref[0, r], vc_ref[0, r], vx_ref[0, r]], axis=0)
            for c, (m, l, acc) in enumerate(_softmax_pv(s, vw, low)):
                acc_ref[r, :, chunk(c)] = acc
                l_ref[c, r] = l
                for e in range(2):
                    m_ref[c, e, r] = jnp.broadcast_to(m[e * TL:(e + 1) * TL], (TL, LANES))

        pipelined([g * TILE_GROUP + i for i in range(TILE_GROUP)], score, finish)
        return carry

    lax.fori_loop(0, PLANES // TILE_GROUP, body16, 0)

    def mid_window(cur, prev, nxt, plane, lb):
        lo = lb * MID_ROWS - MID_LEAD
        if lo < 0:
            return jnp.concatenate([prev[0, plane, TL // 2 + lo:TL // 2],
                                    cur[0, plane, 0:lo + MID_KEYS]], axis=0)
        if lo + MID_KEYS > TL:
            return jnp.concatenate([cur[0, plane, lo:TL],
                                    nxt[0, plane, 0:lo + MID_KEYS - TL]], axis=0)
        return cur[0, plane, lo:lo + MID_KEYS]

    def body4(g, carry):
        block_rows = lambda lb: slice(lb * MID_ROWS, (lb + 1) * MID_ROWS)
        plane_set = lambda r4: [r4 + MID_DIL * j for j in range(MID_DIL)]

        def score(tile):
            r4, lb = tile
            planes = plane_set(r4)
            band = band_ref[1]
            if lb == 0:
                bias = stacked_bias(band, col % MID_KEYS, jnp.where(first, MID_LEAD, 0), MID_KEYS)
            elif lb == TL // MID_ROWS - 1:
                bias = stacked_bias(band, col % MID_KEYS, 0,
                                    jnp.where(last, MID_KEYS - MID_LEAD, MID_KEYS))
            else:
                bias = stacked_bias(band)
            q = jnp.concatenate([q_ref[0, p, block_rows(lb)] for p in planes],
                                axis=0).astype(BF16)
            kw = jnp.concatenate([mid_window(kc_ref, kp_ref, kx_ref, p, lb) for p in planes],
                                 axis=0)
            return _scores(q, kw, bias, hmask)

        def finish(tile, s):
            r4, lb = tile
            planes = plane_set(r4)
            rows = block_rows(lb)
            stacked = lambda ref, c: jnp.concatenate(
                [ref[c, e, p, rows] for e in range(2) for p in planes], axis=0)
            vw = jnp.concatenate([mid_window(vc_ref, vp_ref, vx_ref, p, lb) for p in planes],
                                 axis=0)
            m_old = [stacked(m_ref, c) for c in range(HW // LANES)]
            for c, new in enumerate(_softmax_pv(s, vw, low, m_old)):
                acc_old = jnp.concatenate([acc_ref[p, rows, chunk(c)] for p in planes], axis=0)
                l_old = jnp.concatenate([l_ref[c, p, rows] for p in planes], axis=0)
                m_m, l_m, acc_m = _merge(new, m_old[c], l_old, acc_old, low)
                for j, p in enumerate(planes):
                    piece = slice(j * MID_ROWS, (j + 1) * MID_ROWS)
                    acc_ref[p, rows, chunk(c)] = acc_m[piece]
                    l_ref[c, p, rows] = l_m[piece]
                    for e in range(2):
                        head_piece = slice(e * TL + j * MID_ROWS, e * TL + (j + 1) * MID_ROWS)
                        m_ref[c, e, p, rows] = m_m[head_piece]

        per_body = TILE_GROUP // (TL // MID_ROWS)
        pipelined([(g * per_body + i, lb) for i in range(per_body)
                   for lb in range(TL // MID_ROWS)], score, finish)
        return carry

    lax.fori_loop(0, MID_DIL * (TL // MID_ROWS) // TILE_GROUP, body4, 0)

    n_tiles = SUPER // TL
    for ext_ref, prev, cur, nxt in ((kne_ref, knp_ref, knc_ref, knx_ref),
                                    (vne_ref, vnp_ref, vnc_ref, vnx_ref)):
        ext_ref[0:BAND_HALF] = prev[0]
        ext_ref[BAND_HALF:BAND_HALF + SUPER] = cur[0]
        ext_ref[BAND_HALF + SUPER:] = nxt[0]
    band1 = stacked_bias(band_ref[2])

    def body1(g, carry):
        tiles = [g * TILE_GROUP + i for i in range(TILE_GROUP)]
        tile_rows = lambda t: pl.ds(pl.multiple_of(t * ONE_ROWS, ONE_ROWS), ONE_ROWS)
        tile_keys = lambda t: pl.ds(pl.multiple_of(t * TL, TL), TK)

        def tile_scores(t):
            lo = jnp.where(first & (t == 0), BAND_HALF, 0)
            hi = jnp.where(last & (t == n_tiles - 1), TK - BAND_HALF, TK)
            bias = band1 + jnp.where((col < lo) | (col >= hi), NEG_INF, 0.0)
            q = q_ref[0, :, tile_rows(t), :].reshape(TL, HW).astype(BF16)
            return _scores(q, kne_ref[tile_keys(t), :], bias, hmask)

        def emit(t, merged):
            tok = jnp.dot(unperm_ref[...], merged, preferred_element_type=F32)
            o_ref[0, pl.ds(pl.multiple_of(t * TL, TL), TL), :] = tok.astype(BF16)

        pending = []

        def finish(t, s):
            rows = tile_rows(t)
            stacked = lambda ref, c: jnp.concatenate(
                [ref[c, e, :, rows, :].reshape(TL, LANES) for e in range(2)], axis=0)
            m_old = [stacked(m_ref, c) for c in range(HW // LANES)]
            outs = []
            for c, new in enumerate(_softmax_pv(s, vne_ref[tile_keys(t), :], low, m_old)):
                acc_old = acc_ref[:, rows, chunk(c)].reshape(TL, LANES)
                l_old = l_ref[c, :, rows, :].reshape(TL, LANES)
                _, l_m, acc_m = _merge(new, m_old[c], l_old, acc_old, low)
                outs.append((acc_m / l_m).astype(BF16))
            if pending:
                emit(*pending.pop())
            pending.append((t, jnp.concatenate(outs, axis=1)))

        pipelined(tiles, tile_scores, finish)
        emit(*pending.pop())
        return carry

    lax.fori_loop(0, n_tiles // TILE_GROUP, body1, 0)


def _dil_attn(q16, k16, v16, kn, vn, band, unperm):
    B, S, _ = kn.shape
    n_half = S // PLANES // BAND_HALF
    n_tok_half = S // BAND_HALF
    per = TL // BAND_HALF
    per_tok = SUPER // BAND_HALF
    pcur = pl.BlockSpec((1, PLANES, TL, HW), lambda b, s, hh: (b, 0, s, hh))
    pprev = pl.BlockSpec((1, PLANES, BAND_HALF, HW),
                         lambda b, s, hh: (b, 0, jnp.maximum(s * per - 1, 0), hh))
    pnext = pl.BlockSpec((1, PLANES, BAND_HALF, HW),
                         lambda b, s, hh: (b, 0, jnp.minimum((s + 1) * per, n_half - 1), hh))
    tcur = pl.BlockSpec((1, SUPER, HW), lambda b, s, hh: (b, s, hh))
    tprev = pl.BlockSpec((1, BAND_HALF, HW),
                         lambda b, s, hh: (b, jnp.maximum(s * per_tok - 1, 0), hh))
    tnext = pl.BlockSpec((1, BAND_HALF, HW),
                         lambda b, s, hh: (b, jnp.minimum((s + 1) * per_tok, n_tok_half - 1), hh))
    return pl.pallas_call(
        _dil_attn_kernel,
        grid=(B, S // SUPER, ATTN_W // HW),
        in_specs=[pcur, pcur, pprev, pnext, pcur, pprev, pnext,
                  tcur, tprev, tnext, tcur, tprev, tnext,
                  _const_spec((3, TL, TK)), _const_spec((TL, TL))],
        out_specs=tcur,
        out_shape=jax.ShapeDtypeStruct((B, S, ATTN_W), BF16),
        scratch_shapes=[pltpu.VMEM((PLANES, TL, HW), F32),
                        pltpu.VMEM((HW // LANES, 2, PLANES, TL, LANES), F32),
                        pltpu.VMEM((HW // LANES, PLANES, TL, LANES), F32),
                        pltpu.VMEM((SUPER + 2 * BAND_HALF, HW), BF16),
                        pltpu.VMEM((SUPER + 2 * BAND_HALF, HW), BF16)],
        compiler_params=_cparams(3),
        name="dil_attn",
    )(q16, k16, k16, k16, v16, v16, v16, kn, kn, kn, vn, vn, vn, band, unperm)


def _conv_rolls(buf):
    rows = buf.shape[0]
    return [buf if shift == 0 else pltpu.roll(buf, rows - shift, 0) for shift in range(SUBLANES)]


def _conv_rows(rolled, lo_row, n_rows, w_ref, b_ref, g_ref, beta_ref):
    acc = jnp.zeros((n_rows, CONV_W), F32) + b_ref[...]
    base = HALO - CONV_K // 2
    for tap in range(CONV_K):
        shift = (base + tap) % SUBLANES
        lo = lo_row + base + tap - shift
        acc = acc + w_ref[tap:tap + 1, :] * rolled[shift][lo:lo + n_rows]
    mu = jnp.mean(acc, axis=-1, keepdims=True)
    d = acc - mu
    var = jnp.mean(d * d, axis=-1, keepdims=True)
    z = d * lax.rsqrt(var + NORM_EPS) * g_ref[...] + beta_ref[...]
    return z * _sigmoid(z)


N_CAST = D_FF // FF_CH
UP_CAST = 2 * D_FF // N_CAST
DOWN_CAST = D_FF // N_CAST
OUT_CAST = LANES
assert D_MODEL // OUT_CAST <= N_CAST and N_CAST * UP_CAST == 2 * D_FF and UP_CAST % FF_CH == 0


def _out_ffn_kernel(a_ref, ap_ref, ax_ref, c_ref, cp_ref, cx_ref, m_ref, mp_ref, mx_ref,
                    x_ref, xp_ref, xx_ref, wo32_ref, g_ref, wu32_ref, dw_ref, db_ref, wd32_ref,
                    o_ref, gate_ref, wo_ref, wu_ref, wd_ref, *, n_tiles):
    step = pl.program_id(0)

    for c in range(N_CAST):
        @pl.when(step == c)
        def _(c=c):
            for half in range(UP_CAST // FF_CH):
                src = c * UP_CAST + half * FF_CH
                j, is_up = (src // FF_CH, 0) if src < D_FF else ((src - D_FF) // FF_CH, 1)
                dst = (2 * j + is_up) * FF_CH
                wu_ref[:, dst:dst + FF_CH] = wu32_ref[:, half * FF_CH:(half + 1) * FF_CH].astype(BF16)
            wd_ref[c * DOWN_CAST:(c + 1) * DOWN_CAST, :] = wd32_ref[...].astype(BF16)
            if (c + 1) * OUT_CAST <= D_MODEL:
                wo_ref[c * OUT_CAST:(c + 1) * OUT_CAST, :] = wo32_ref[...].astype(BF16)

    @pl.when(step >= N_CAST)
    def _():
        _out_ffn_tile(a_ref, ap_ref, ax_ref, c_ref, cp_ref, cx_ref, m_ref, mp_ref, mx_ref,
                      x_ref, xp_ref, xx_ref, wo_ref, g_ref, wu_ref, dw_ref, db_ref, wd_ref,
                      o_ref, gate_ref, lax.rem(step - N_CAST, n_tiles), n_tiles)


def _out_ffn_tile(a_ref, ap_ref, ax_ref, c_ref, cp_ref, cx_ref, m_ref, mp_ref, mx_ref,
                  x_ref, xp_ref, xx_ref, wo_ref, g_ref, wu_ref, dw_ref, db_ref, wd_ref,
                  o_ref, gate_ref, t, nt):
    has_next, has_prev = t < nt - 1, t > 0
    rows = TF + HALO

    def ext_rows(cur, nxt, prev):
        halo = jnp.concatenate([nxt[0, :SUBLANES].astype(F32), prev[0, SUBLANES:].astype(F32)],
                               axis=0)
        return jnp.concatenate([cur[0], halo.astype(cur.dtype)], axis=0)

    rid = lax.broadcasted_iota(jnp.int32, (rows, 1), 0)
    in_seq = ((rid < TF) | ((rid < TF + SUBLANES) & has_next)
              | ((rid >= TF + SUBLANES) & has_prev))
    mixed = jnp.concatenate([ext_rows(a_ref, ax_ref, ap_ref), ext_rows(c_ref, cx_ref, cp_ref),
                             ext_rows(m_ref, mx_ref, mp_ref)], axis=-1)
    mixed = jnp.where(in_seq, mixed, jnp.zeros_like(mixed))
    h = (jnp.where(in_seq, ext_rows(x_ref, xx_ref, xp_ref), 0.0)
         + jnp.dot(mixed, wo_ref[...], preferred_element_type=F32))
    ext = _rms_rows(h, g_ref[...]).astype(BF16)

    def conv3(f, lo):
        w = dw_ref[:, lo:lo + FF_CH]
        y = (w[0:1] * pltpu.roll(f, 1, 0) + w[1:2] * f + w[2:3] * pltpu.roll(f, rows - 1, 0))
        return y[:TF] + db_ref[:, lo:lo + FF_CH]

    n_chunks = D_FF // FF_CH
    out = h[:TF]
    for j in range(n_chunks):
        lo_g, lo_u = j * FF_CH, D_FF + j * FF_CH
        f = jnp.dot(ext, wu_ref[:, 2 * lo_g:2 * lo_g + 2 * FF_CH], preferred_element_type=F32)
        if j == n_chunks - 1:
            out = out + jnp.dot(gate_ref[:, :lo_g], wd_ref[:lo_g, :], preferred_element_type=F32)
        fg, fu = conv3(f[:, :FF_CH], lo_g), conv3(f[:, FF_CH:], lo_u)
        gate_ref[:, lo_g:lo_g + FF_CH] = (fg * _sigmoid(fg) * fu).astype(BF16)
    o_ref[0] = out + jnp.dot(gate_ref[:, D_FF - FF_CH:], wd_ref[D_FF - FF_CH:, :],
                             preferred_element_type=F32)


def _out_ffn(attn, cb, mo, x, w_out, g, w_up, dw_w, dw_b, w_down):
    B, S, _ = x.shape
    nt = S // TF
    per = TF // HALO

    def tile_of(step):
        i = jnp.maximum(step - N_CAST, 0)
        return i // nt, i % nt

    def with_halos(width):
        def cur(step):
            b, t = tile_of(step)
            return b, t, 0

        def prev(step):
            b, t = tile_of(step)
            return b, jnp.maximum(t * per - 1, 0), 0

        def nxt(step):
            b, t = tile_of(step)
            return b, jnp.minimum((t + 1) * per, nt * per - 1), 0

        return [pl.BlockSpec((1, TF, width), cur), pl.BlockSpec((1, HALO, width), prev),
                pl.BlockSpec((1, HALO, width), nxt)]

    cast_chunk = lambda limit: (lambda step: jnp.minimum(step, limit - 1))
    out_i, up_i, down_i = cast_chunk(D_MODEL // OUT_CAST), cast_chunk(N_CAST), cast_chunk(N_CAST)
    return pl.pallas_call(
        functools.partial(_out_ffn_kernel, n_tiles=nt),
        grid=(N_CAST + B * nt,),
        in_specs=with_halos(ATTN_W) + with_halos(CONV_W) + with_halos(MEM_W)
        + with_halos(D_MODEL)
        + [pl.BlockSpec((OUT_CAST, D_MODEL), lambda s: (out_i(s), 0)), _const_spec((1, D_MODEL)),
           pl.BlockSpec((D_MODEL, UP_CAST), lambda s: (0, up_i(s))),
           _const_spec((FFN_CONV_K, 2 * D_FF)), _const_spec((1, 2 * D_FF)),
           pl.BlockSpec((DOWN_CAST, D_MODEL), lambda s: (down_i(s), 0))],
        out_specs=pl.BlockSpec((1, TF, D_MODEL), lambda s: (*tile_of(s), 0)),
        out_shape=jax.ShapeDtypeStruct((B, S, D_MODEL), F32),
        scratch_shapes=[pltpu.VMEM((TF, D_FF), BF16), pltpu.VMEM((D_MODEL, D_MODEL), BF16),
                        pltpu.VMEM((D_MODEL, 2 * D_FF), BF16), pltpu.VMEM((D_FF, D_MODEL), BF16)],
        compiler_params=pltpu.CompilerParams(dimension_semantics=("arbitrary",),
                                             vmem_limit_bytes=VMEM_LIMIT),
        name="out_ffn",
    )(attn, attn, attn, cb, cb, cb, mo, mo, mo, x, x, x, w_out, g, w_up, dw_w, dw_b, w_down)


def _rope_tables():
    inv_freq = ROPE_THETA ** (-jnp.arange(0, ROT_DIM, 2, dtype=F32) / ROT_DIM)
    invf = jnp.broadcast_to(inv_freq[:, None], (SUBLANES, LANES))
    half = ROT_DIM // 2
    lane = np.arange(LANES) % HEAD_DIM
    lanes = np.zeros((SUBLANES, LANES), np.float32)
    lanes[0] = lane >= ROT_DIM
    lanes[1] = lane < half
    expand = np.zeros((LANES, 2 * LANES), np.float32)
    for l in range(LANES):
        if lane[l] < ROT_DIM:
            j = lane[l] % half
            expand[[j, half + j], l] = 1.0
            expand[[2 * half + j, 3 * half + j], LANES + l] = -1.0 if lane[l] < half else 1.0
    return invf, jnp.asarray(lanes), jnp.asarray(expand, BF16)


def _group_sum_matrix():
    idx = np.arange(MXU_DIM) // HEAD_DIM
    return jnp.asarray(idx[:, None] == idx[None, :], BF16)


def _plane_perm(n):
    out = np.arange(n)
    src = (out % (n // PLANES)) * PLANES + out // (n // PLANES)
    return np.asarray(src[:, None] == np.arange(n)[None, :], np.float32)


def kernel(x, mem, positions, mix_norm_g, mem_norm_g, w_in, w_mem_kv, q_norm_g, k_norm_g, mq_norm_g, mk_norm_g, conv_dw_w, conv_dw_b, conv_ln_g, conv_ln_b, w_out, ffn_norm_g, w_up, ffn_dw_w, ffn_dw_b, w_down):
    B, S, _ = x.shape
    depth = w_in.shape[0]
    pos_row = positions.reshape(B, 1, S)
    invf, lane_tab, expand = _rope_tables()
    gsum = _group_sum_matrix()
    perm = jnp.asarray(_plane_perm(TM), BF16)
    unperm = jnp.asarray(_plane_perm(TL).T, BF16)
    band = _band_tables()
    row = lambda a: a.reshape(1, -1)
    h = x
    for l in range(depth):
        kmt, vm = _mem_kv(mem, row(mem_norm_g[l]), w_mem_kv[l],
                          row(jnp.tile(mk_norm_g[l], MEM_HEADS)), gsum)
        q16, k16, v16, kn, vn, cb, mo = _in_proj(
            h, pos_row, row(mix_norm_g[l]), w_in[l],
            row(jnp.tile(q_norm_g[l], ATTN_HEADS)), row(jnp.tile(k_norm_g[l], ATTN_HEADS)),
            row(jnp.tile(mq_norm_g[l], MEM_HEADS)), gsum, invf, lane_tab, expand, perm,
            conv_dw_w[l], row(conv_dw_b[l]), row(conv_ln_g[l]), row(conv_ln_b[l]), kmt, vm)
        attn = _dil_attn(q16, k16, v16, kn, vn, band, unperm)
        h = _out_ffn(attn, cb, mo, h, w_out[l], row(ffn_norm_g[l]), w_up[l], ffn_dw_w[l],
                     row(ffn_dw_b[l]), w_down[l])
    return h
```

```python
import functools

import numpy as np
import jax
import jax.numpy as jnp
from jax import lax
from jax.experimental import pallas as pl
from jax.experimental.pallas import tpu as pltpu

F32 = jnp.float32
BF16 = jnp.bfloat16

D_MODEL = 1024
HEAD_DIM = 64
ATTN_HEADS = 8
ATTN_W = ATTN_HEADS * HEAD_DIM
CONV_W = 256
MEM_HEADS = 4
MEM_W = MEM_HEADS * HEAD_DIM
N_MEM = 256
PLANES = 16
MID_DIL = 4
BAND_HALF = 64
ROPE_THETA = 500000.0
ROT_DIM = HEAD_DIM // 4
CONV_K = 31
FFN_CONV_K = 3
D_FF = 2816
NORM_EPS = 1e-6
NEG_INF = -1e30
SM_SCALE = HEAD_DIM ** -0.5

LANES = 128
SUBLANES = 8
MXU_DIM = 256
BF16_ROWS = 16
VMEM_LIMIT = 56 * 1024 * 1024

TM = 512
CONV_ROWS = 128
TF = 512
TL = 128
TK = TL + 2 * BAND_HALF
SUPER = PLANES * TL
FF_CH = MXU_DIM
FFN_ROWS = 128
HALO = BF16_ROWS


def _cparams(n_axes):
    return pltpu.CompilerParams(dimension_semantics=("parallel",) * n_axes,
                                vmem_limit_bytes=VMEM_LIMIT)


def _const_spec(shape):
    return pl.BlockSpec(shape, lambda *_: (0,) * len(shape), pipeline_mode=pl.Buffered(1))


def _rms_rows(x, g):
    r = lax.rsqrt(jnp.mean(x * x, axis=-1, keepdims=True) + NORM_EPS)
    return x * r * g


def _head_sumsq(t, gsum):
    return [jnp.dot((tc * tc).astype(BF16), gsum, preferred_element_type=F32)
            for tc in (t[:, c * MXU_DIM:(c + 1) * MXU_DIM] for c in range(t.shape[1] // MXU_DIM))]


def _head_scale(t, sumsq, gain):
    outs = []
    for c, ssum in enumerate(sumsq):
        sl = slice(c * MXU_DIM, (c + 1) * MXU_DIM)
        outs.append(t[:, sl] * lax.rsqrt(ssum * (1.0 / HEAD_DIM) + NORM_EPS) * gain[:, sl])
    return outs


def _head_norm(t, gain, gsum):
    return _head_scale(t, _head_sumsq(t, gsum), gain)


def _sigmoid(x):
    return 1.0 / (1.0 + jnp.exp(-x))


IN_COLS = 3 * ATTN_W + 2 * CONV_W + MEM_W
IN_CAST = 3 * MXU_DIM
N_CAST_IN = IN_COLS // IN_CAST


def _in_proj_kernel(x_ref, xp_ref, xx_ref, pos_ref, g_ref, w32_ref, gq_ref, gk_ref, gm_ref,
                    gsum_ref, invf_ref, lane_ref, expand_ref, perm_ref,
                    cw_ref, cbias_ref, cg_ref, cbeta_ref, kmt_ref, vm_ref,
                    q16_ref, k16_ref, v16_ref, kn_ref, vn_ref, cb_ref, mo_ref,
                    w_ref, *, n_tiles):
    step = pl.program_id(0)

    for c in range(N_CAST_IN):
        @pl.when(step == c)
        def _(c=c):
            w_ref[:, c * IN_CAST:(c + 1) * IN_CAST] = w32_ref[...].astype(BF16)

    @pl.when(step >= N_CAST_IN)
    def _():
        _in_proj_tile(x_ref, xp_ref, xx_ref, pos_ref, g_ref, w_ref, gq_ref, gk_ref, gm_ref,
                      gsum_ref, invf_ref, lane_ref, expand_ref, perm_ref,
                      cw_ref, cbias_ref, cg_ref, cbeta_ref, kmt_ref, vm_ref,
                      q16_ref, k16_ref, v16_ref, kn_ref, vn_ref, cb_ref, mo_ref,
                      lax.rem(step - N_CAST_IN, n_tiles), n_tiles)


def _in_proj_tile(x_ref, xp_ref, xx_ref, pos_ref, g_ref, w_ref, gq_ref, gk_ref, gm_ref,
                  gsum_ref, invf_ref, lane_ref, expand_ref, perm_ref,
                  cw_ref, cbias_ref, cg_ref, cbeta_ref, kmt_ref, vm_ref,
                  q16_ref, k16_ref, v16_ref, kn_ref, vn_ref, cb_ref, mo_ref,
                  t, nt):
    has_prev, has_next = t > 0, t < nt - 1
    hn = _rms_rows(x_ref[0], g_ref[...]).astype(BF16)

    hn_halo = _rms_rows(jnp.concatenate([xp_ref[0], xx_ref[0]], axis=0), g_ref[...]).astype(BF16)
    hn_ext = jnp.concatenate([hn_halo[:HALO], hn, hn_halo[HALO:]], axis=0)
    cols = lambda lo, n: w_ref[:, lo:lo + n]
    q = jnp.dot(hn, cols(0, ATTN_W), preferred_element_type=F32)
    k = jnp.dot(hn, cols(ATTN_W, ATTN_W), preferred_element_type=F32)
    v = jnp.dot(hn, cols(2 * ATTN_W, ATTN_W), preferred_element_type=F32)
    glu = jnp.dot(hn_ext, cols(3 * ATTN_W, 2 * CONV_W), preferred_element_type=F32)
    qm = jnp.dot(hn, cols(3 * ATTN_W + 2 * CONV_W, MEM_W), preferred_element_type=F32)

    ang = invf_ref[:, 0:1] * pos_ref[0].astype(F32)

    def hi_lo(v):
        hi = v.astype(BF16).astype(F32)
        return [hi, (v - hi).astype(BF16).astype(F32)]

    tab = jnp.concatenate(hi_lo(jnp.cos(ang)) + hi_lo(jnp.sin(ang))
                          + [jnp.zeros((LANES - 4 * SUBLANES, TM), F32)], axis=0)
    cs = jnp.dot(tab.T.astype(BF16), expand_ref[...], preferred_element_type=F32)
    cosv = cs[:, :LANES] + lane_ref[0:1, :]
    sinv = cs[:, LANES:]
    first_half = lane_ref[1:2, :] > 0.5

    def rotary(chunks, scale):
        out = []
        for t in chunks:
            for s in range(MXU_DIM // LANES):
                xc = t[:, s * LANES:(s + 1) * LANES]
                partner = jnp.where(first_half, pltpu.roll(xc, LANES - ROT_DIM // 2, 1),
                                    pltpu.roll(xc, ROT_DIM // 2, 1))
                out.append((xc * cosv + partner * sinv) * scale)
        return out

    gsum = gsum_ref[...]
    q_ss, k_ss, qm_ss = _head_sumsq(q, gsum), _head_sumsq(k, gsum), _head_sumsq(qm, gsum)
    q_chunks = rotary(_head_scale(q, q_ss, gq_ref[...]), SM_SCALE)
    k_chunks = rotary(_head_scale(k, k_ss, gk_ref[...]), 1.0)
    qb = jnp.concatenate(q_chunks, axis=1).astype(BF16)
    kb = jnp.concatenate(k_chunks, axis=1).astype(BF16)
    vb = v.astype(BF16)
    kn_ref[0] = kb
    vn_ref[0] = vb
    qkv = jnp.dot(perm_ref[...], jnp.concatenate([qb, kb, vb], axis=1),
                  preferred_element_type=F32)
    rows = TM // PLANES
    for r in range(PLANES):
        blk = qkv[r * rows:(r + 1) * rows]
        q16_ref[0, r] = blk[:, 0:ATTN_W]
        k16_ref[0, r] = blk[:, ATTN_W:2 * ATTN_W].astype(BF16)
        v16_ref[0, r] = blk[:, 2 * ATTN_W:].astype(BF16)

    (qmn,) = _head_scale(qm, qm_ss, gm_ref[...])
    mo_ref[0] = _mem_attn((qmn * SM_SCALE).astype(BF16), kmt_ref, vm_ref)

    rid = lax.broadcasted_iota(jnp.int32, (TM + 2 * HALO, 1), 0)
    in_seq = ((rid >= HALO) | has_prev) & ((rid < TM + HALO) | has_next)
    cbuf = jnp.where(in_seq, glu[:, :CONV_W] * _sigmoid(glu[:, CONV_W:]), 0.0)
    rolled = _conv_rolls(cbuf)
    for lo in range(0, TM, CONV_ROWS):
        cb_ref[0, lo:lo + CONV_ROWS] = _conv_rows(rolled, lo, CONV_ROWS, cw_ref, cbias_ref,
                                                  cg_ref, cbeta_ref).astype(BF16)


def _in_proj(x, pos_row, g, w_in, gq, gk, gm, gsum, invf, lane_tab, expand, perm,
             conv_w, conv_b, conv_g, conv_beta, kmt, vm):
    B, S, _ = x.shape
    nt = S // TM
    per = TM // HALO

    def tile_of(step):
        i = jnp.maximum(step - N_CAST_IN, 0)
        return i // nt, i % nt

    def at_tile(fn):
        return lambda step: fn(*tile_of(step))

    tok = lambda w: pl.BlockSpec((1, TM, w), at_tile(lambda b, t: (b, t, 0)))
    xprev = pl.BlockSpec((1, HALO, D_MODEL),
                         at_tile(lambda b, t: (b, jnp.maximum(t * per - 1, 0), 0)))
    xnext = pl.BlockSpec((1, HALO, D_MODEL),
                         at_tile(lambda b, t: (b, jnp.minimum((t + 1) * per, nt * per - 1), 0)))
    plane = pl.BlockSpec((1, PLANES, TM // PLANES, ATTN_W), at_tile(lambda b, t: (b, 0, t, 0)))
    plane_shape = (B, PLANES, S // PLANES, ATTN_W)
    per_batch = lambda r, w: pl.BlockSpec((1, r, w), at_tile(lambda b, t: (b, 0, 0)))
    return pl.pallas_call(
        functools.partial(_in_proj_kernel, n_tiles=nt),
        grid=(N_CAST_IN + B * nt,),
        in_specs=[tok(D_MODEL), xprev, xnext,
                  pl.BlockSpec((1, 1, TM), at_tile(lambda b, t: (b, 0, t))),
                  _const_spec((1, D_MODEL)),
                  pl.BlockSpec((D_MODEL, IN_CAST),
                               lambda step: (0, jnp.minimum(step, N_CAST_IN - 1))),
                  _const_spec((1, ATTN_W)), _const_spec((1, ATTN_W)), _const_spec((1, MEM_W)),
                  _const_spec((MXU_DIM, MXU_DIM)), _const_spec((SUBLANES, LANES)),
                  _const_spec((SUBLANES, LANES)), _const_spec((LANES, 2 * LANES)),
                  _const_spec((TM, TM)),
                  _const_spec((CONV_K, CONV_W)), _const_spec((1, CONV_W)),
                  _const_spec((1, CONV_W)), _const_spec((1, CONV_W)),
                  per_batch(MEM_W, N_MEM), per_batch(N_MEM, MEM_W)],
        out_specs=[plane, plane, plane, tok(ATTN_W), tok(ATTN_W), tok(CONV_W), tok(MEM_W)],
        out_shape=[jax.ShapeDtypeStruct(plane_shape, F32),
                   jax.ShapeDtypeStruct(plane_shape, BF16),
                   jax.ShapeDtypeStruct(plane_shape, BF16),
                   jax.ShapeDtypeStruct((B, S, ATTN_W), BF16),
                   jax.ShapeDtypeStruct((B, S, ATTN_W), BF16),
                   jax.ShapeDtypeStruct((B, S, CONV_W), BF16),
                   jax.ShapeDtypeStruct((B, S, MEM_W), BF16)],
        scratch_shapes=[pltpu.VMEM((D_MODEL, IN_COLS), BF16)],
        compiler_params=pltpu.CompilerParams(dimension_semantics=("arbitrary",),
                                             vmem_limit_bytes=VMEM_LIMIT),
        name="in_proj",
    )(x, x, x, pos_row, g, w_in, gq, gk, gm, gsum, invf, lane_tab, expand, perm,
      conv_w, conv_b, conv_g, conv_beta, kmt, vm)


def _mem_kv_kernel(mem_ref, g_ref, w_ref, gk_ref, gsum_ref, kmt_ref, vm_ref):
    mn = _rms_rows(mem_ref[0], g_ref[...]).astype(BF16)
    kv = jnp.dot(mn, w_ref[...].astype(BF16), preferred_element_type=F32)
    (km,) = _head_norm(kv[:, :MEM_W], gk_ref[...], gsum_ref[...])
    kmt_ref[0] = km.T.astype(BF16)
    vm_ref[0] = kv[:, MEM_W:].astype(BF16)


def _mem_kv(mem, g, w, gk, gsum):
    B = mem.shape[0]
    return pl.pallas_call(
        _mem_kv_kernel,
        grid=(B,),
        in_specs=[pl.BlockSpec((1, N_MEM, D_MODEL), lambda b: (b, 0, 0)),
                  _const_spec((1, D_MODEL)), _const_spec((D_MODEL, 2 * MEM_W)),
                  _const_spec((1, MEM_W)), _const_spec((MXU_DIM, MXU_DIM))],
        out_specs=[pl.BlockSpec((1, MEM_W, N_MEM), lambda b: (b, 0, 0)),
                   pl.BlockSpec((1, N_MEM, MEM_W), lambda b: (b, 0, 0))],
        out_shape=[jax.ShapeDtypeStruct((B, MEM_W, N_MEM), BF16),
                   jax.ShapeDtypeStruct((B, N_MEM, MEM_W), BF16)],
        compiler_params=_cparams(1),
        name="mem_kv",
    )(mem, g, w, gk, gsum)


def _head_masks():
    lane = lax.broadcasted_iota(jnp.int32, (1, LANES), 1)
    low = lane < HEAD_DIM
    return low, (jnp.where(low, 1.0, 0.0).astype(BF16), jnp.where(low, 0.0, 1.0).astype(BF16))


def _mem_attn(qm, kmt_ref, vm_ref):
    low, hmask = _head_masks()
    ones = jnp.ones((N_MEM, LANES), BF16)
    out = []
    for c in range(MEM_W // LANES):
        sl = slice(c * LANES, (c + 1) * LANES)
        qc = qm[:, sl]
        v_aug = jnp.concatenate([vm_ref[0, :, sl], ones], axis=1)
        halves = []
        for e in range(2):
            s = jnp.dot(qc * hmask[e], kmt_ref[0, sl, :], preferred_element_type=F32)
            m = jnp.max(s, axis=-1, keepdims=True)
            p = jnp.exp((s - m).astype(BF16))
            r = jnp.dot(p, v_aug, preferred_element_type=F32)
            halves.append(r[:, :LANES] / r[:, LANES:])
        out.append(jnp.where(low, halves[0], halves[1]).astype(BF16))
    return jnp.concatenate(out, axis=1)


HW = ATTN_W // 2
MID_ROWS = TL // MID_DIL
MID_KEYS = TK // MID_DIL
MID_LEAD = (MID_KEYS - MID_ROWS) // 2
ONE_ROWS = TL // PLANES
TILE_GROUP = PLANES


def _band_tables():
    rho = np.arange(TL)[:, None]
    kap = np.arange(TK)[None, :]
    d16 = kap - BAND_HALF - rho
    j, lq = rho // MID_ROWS, rho % MID_ROWS
    jk, lk = kap // MID_KEYS, kap % MID_KEYS
    d4 = MID_DIL * (lk - MID_LEAD - lq) + (jk - j)
    r, l1 = rho // ONE_ROWS, rho % ONE_ROWS
    d1 = kap - BAND_HALF - PLANES * l1 - r
    tabs = [np.where(np.abs(d) <= BAND_HALF, 0.0, NEG_INF) for d in (d16, d4, d1)]
    return jnp.asarray(np.stack(tabs), F32)


def _scores(q, kw, bias2, hmask):
    out = []
    for c in range(HW // LANES):
        sl = slice(c * LANES, (c + 1) * LANES)
        qc = q[:, sl]
        qs = jnp.concatenate([qc * hmask[0], qc * hmask[1]], axis=0)
        out.append(lax.dot_general(qs, kw[:, sl], (((1,), (1,)), ((), ())),
                                   preferred_element_type=F32) + bias2)
    return out


def _softmax_pv(scores, vw, low, m_old=None):
    ones = jnp.ones((TK, LANES), BF16)
    res = []
    for c, s in enumerate(scores):
        m = jnp.max(s, axis=-1, keepdims=True)
        if m_old is None:
            shift = m
        else:
            m = jnp.maximum(m, m_old[c])
            shift = jnp.concatenate([m] * (TK // LANES), axis=1)
        p = jnp.exp((s - shift).astype(BF16))
        v_aug = jnp.concatenate([vw[:, c * LANES:(c + 1) * LANES], ones], axis=1)
        r = jnp.dot(p, v_aug, preferred_element_type=F32)
        res.append((m, jnp.where(low, r[:TL, LANES:], r[TL:, LANES:]),
                    jnp.where(low, r[:TL, :LANES], r[TL:, :LANES])))
    return res


def _merge(new, m_old, l_old, acc_old, low):
    m, l_n, acc_n = new
    b = jnp.exp(m_old - m)
    b = jnp.where(low, b[:TL], b[TL:])
    return m, l_n + b * l_old, acc_n + b * acc_old


def _dil_attn_kernel(q_ref, kc_ref, kp_ref, kx_ref, vc_ref, vp_ref, vx_ref,
                     knc_ref, knp_ref, knx_ref, vnc_ref, vnp_ref, vnx_ref, band_ref, unperm_ref,
                     o_ref, acc_ref, m_ref, l_ref, kne_ref, vne_ref):
    st = pl.program_id(1)
    first, last = st == 0, st == pl.num_programs(1) - 1
    low, hmask = _head_masks()
    col = lax.broadcasted_iota(jnp.int32, (1, TK), 1)
    chunk = lambda c: slice(c * LANES, (c + 1) * LANES)

    def stacked_bias(band, col_idx=None, lo=0, hi=TK):
        if col_idx is not None:
            band = band + jnp.where((col_idx < lo) | (col_idx >= hi), NEG_INF, 0.0)
        return jnp.concatenate([band, band], axis=0)

    bias16 = stacked_bias(band_ref[0], col, jnp.where(first, BAND_HALF, 0),
                          jnp.where(last, TK - BAND_HALF, TK))

    def pipelined(tiles, score_fn, finish_fn):
        s_next = score_fn(tiles[0])
        for i, t in enumerate(tiles):
            s = s_next
            if i + 1 < len(tiles):
                s_next = score_fn(tiles[i + 1])
            finish_fn(t, s)

    def body16(g, carry):
        def score(r):
            kw = jnp.concatenate([kp_ref[0, r], kc_ref[0, r], kx_ref[0, r]], axis=0)
            return _scores(q_ref[0, r].astype(BF16), kw, bias16, hmask)

        def finish(r, s):
            vw = jnp.concatenate([vp_ref[0, r], vc_ref[0, r], vx_ref[0, r]], axis=0)
            for c, (m, l, acc) in enumerate(_softmax_pv(s, vw, low)):
                acc_ref[r, :, chunk(c)] = acc
                l_ref[c, r] = l
                for e in range(2):
                    m_ref[c, e, r] = jnp.broadcast_to(m[e * TL:(e + 1) * TL], (TL, LANES))

        pipelined([g * TILE_GROUP + i for i in range(TILE_GROUP)], score, finish)
        return carry

    lax.fori_loop(0, PLANES // TILE_GROUP, body16, 0)

    def mid_window(cur, prev, nxt, plane, lb):
        lo = lb * MID_ROWS - MID_LEAD
        if lo < 0:
            return jnp.concatenate([prev[0, plane, TL // 2 + lo:TL // 2],
                                    cur[0, plane, 0:lo + MID_KEYS]], axis=0)
        if lo + MID_KEYS > TL:
            return jnp.concatenate([cur[0, plane, lo:TL],
                                    nxt[0, plane, 0:lo + MID_KEYS - TL]], axis=0)
        return cur[0, plane, lo:lo + MID_KEYS]

    def body4(g, carry):
        block_rows = lambda lb: slice(lb * MID_ROWS, (lb + 1) * MID_ROWS)
        plane_set = lambda r4: [r4 + MID_DIL * j for j in range(MID_DIL)]

        def score(tile):
            r4, lb = tile
            planes = plane_set(r4)
            band = band_ref[1]
            if lb == 0:
                bias = stacked_bias(band, col % MID_KEYS, jnp.where(first, MID_LEAD, 0), MID_KEYS)
            elif lb == TL // MID_ROWS - 1:
                bias = stacked_bias(band, col % MID_KEYS, 0,
                                    jnp.where(last, MID_KEYS - MID_LEAD, MID_KEYS))
            else:
                bias = stacked_bias(band)
            q = jnp.concatenate([q_ref[0, p, block_rows(lb)] for p in planes],
                                axis=0).astype(BF16)
            kw = jnp.concatenate([mid_window(kc_ref, kp_ref, kx_ref, p, lb) for p in planes],
                                 axis=0)
            return _scores(q, kw, bias, hmask)

        def finish(tile, s):
            r4, lb = tile
            planes = plane_set(r4)
            rows = block_rows(lb)
            stacked = lambda ref, c: jnp.concatenate(
                [ref[c, e, p, rows] for e in range(2) for p in planes], axis=0)
            vw = jnp.concatenate([mid_window(vc_ref, vp_ref, vx_ref, p, lb) for p in planes],
                                 axis=0)
            m_old = [stacked(m_ref, c) for c in range(HW // LANES)]
            for c, new in enumerate(_softmax_pv(s, vw, low, m_old)):
                acc_old = jnp.concatenate([acc_ref[p, rows, chunk(c)] for p in planes], axis=0)
                l_old = jnp.concatenate([l_ref[c, p, rows] for p in planes], axis=0)
                m_m, l_m, acc_m = _merge(new, m_old[c], l_old, acc_old, low)
                for j, p in enumerate(planes):
                    piece = slice(j * MID_ROWS, (j + 1) * MID_ROWS)
                    acc_ref[p, rows, chunk(c)] = acc_m[piece]
                    l_ref[c, p, rows] = l_m[piece]
                    for e in range(2):
                        head_piece = slice(e * TL + j * MID_ROWS, e * TL + (j + 1) * MID_ROWS)
                        m_ref[c, e, p, rows] = m_m[head_piece]

        per_body = TILE_GROUP // (TL // MID_ROWS)
        pipelined([(g * per_body + i, lb) for i in range(per_body)
                   for lb in range(TL // MID_ROWS)], score, finish)
        return carry

    lax.fori_loop(0, MID_DIL * (TL // MID_ROWS) // TILE_GROUP, body4, 0)

    n_tiles = SUPER // TL
    for ext_ref, prev, cur, nxt in ((kne_ref, knp_ref, knc_ref, knx_ref),
                                    (vne_ref, vnp_ref, vnc_ref, vnx_ref)):
        ext_ref[0:BAND_HALF] = prev[0]
        ext_ref[BAND_HALF:BAND_HALF + SUPER] = cur[0]
        ext_ref[BAND_HALF + SUPER:] = nxt[0]
    band1 = stacked_bias(band_ref[2])

    def body1(g, carry):
        tiles = [g * TILE_GROUP + i for i in range(TILE_GROUP)]
        tile_rows = lambda t: pl.ds(pl.multiple_of(t * ONE_ROWS, ONE_ROWS), ONE_ROWS)
        tile_keys = lambda t: pl.ds(pl.multiple_of(t * TL, TL), TK)

        def tile_scores(t):
            lo = jnp.where(first & (t == 0), BAND_HALF, 0)
            hi = jnp.where(last & (t == n_tiles - 1), TK - BAND_HALF, TK)
            bias = band1 + jnp.where((col < lo) | (col >= hi), NEG_INF, 0.0)
            q = q_ref[0, :, tile_rows(t), :].reshape(TL, HW).astype(BF16)
            return _scores(q, kne_ref[tile_keys(t), :], bias, hmask)

        def emit(t, merged):
            tok = jnp.dot(unperm_ref[...], merged, preferred_element_type=F32)
            o_ref[0, pl.ds(pl.multiple_of(t * TL, TL), TL), :] = tok.astype(BF16)

        pending = []

        def finish(t, s):
            rows = tile_rows(t)
            stacked = lambda ref, c: jnp.concatenate(
                [ref[c, e, :, rows, :].reshape(TL, LANES) for e in range(2)], axis=0)
            m_old = [stacked(m_ref, c) for c in range(HW // LANES)]
            outs = []
            for c, new in enumerate(_softmax_pv(s, vne_ref[tile_keys(t), :], low, m_old)):
                acc_old = acc_ref[:, rows, chunk(c)].reshape(TL, LANES)
                l_old = l_ref[c, :, rows, :].reshape(TL, LANES)
                _, l_m, acc_m = _merge(new, m_old[c], l_old, acc_old, low)
                outs.append((acc_m / l_m).astype(BF16))
            if pending:
                emit(*pending.pop())
            pending.append((t, jnp.concatenate(outs, axis=1)))

        pipelined(tiles, tile_scores, finish)
        emit(*pending.pop())
        return carry

    lax.fori_loop(0, n_tiles // TILE_GROUP, body1, 0)


def _dil_attn(q16, k16, v16, kn, vn, band, unperm):
    B, S, _ = kn.shape
    n_half = S // PLANES // BAND_HALF
    n_tok_half = S // BAND_HALF
    per = TL // BAND_HALF
    per_tok = SUPER // BAND_HALF
    pcur = pl.BlockSpec((1, PLANES, TL, HW), lambda b, s, hh: (b, 0, s, hh))
    pprev = pl.BlockSpec((1, PLANES, BAND_HALF, HW),
                         lambda b, s, hh: (b, 0, jnp.maximum(s * per - 1, 0), hh))
    pnext = pl.BlockSpec((1, PLANES, BAND_HALF, HW),
                         lambda b, s, hh: (b, 0, jnp.minimum((s + 1) * per, n_half - 1), hh))
    tcur = pl.BlockSpec((1, SUPER, HW), lambda b, s, hh: (b, s, hh))
    tprev = pl.BlockSpec((1, BAND_HALF, HW),
                         lambda b, s, hh: (b, jnp.maximum(s * per_tok - 1, 0), hh))
    tnext = pl.BlockSpec((1, BAND_HALF, HW),
                         lambda b, s, hh: (b, jnp.minimum((s + 1) * per_tok, n_tok_half - 1), hh))
    return pl.pallas_call(
        _dil_attn_kernel,
        grid=(B, S // SUPER, ATTN_W // HW),
        in_specs=[pcur, pcur, pprev, pnext, pcur, pprev, pnext,
                  tcur, tprev, tnext, tcur, tprev, tnext,
                  _const_spec((3, TL, TK)), _const_spec((TL, TL))],
        out_specs=tcur,
        out_shape=jax.ShapeDtypeStruct((B, S, ATTN_W), BF16),
        scratch_shapes=[pltpu.VMEM((PLANES, TL, HW), F32),
                        pltpu.VMEM((HW // LANES, 2, PLANES, TL, LANES), F32),
                        pltpu.VMEM((HW // LANES, PLANES, TL, LANES), F32),
                        pltpu.VMEM((SUPER + 2 * BAND_HALF, HW), BF16),
                        pltpu.VMEM((SUPER + 2 * BAND_HALF, HW), BF16)],
        compiler_params=_cparams(3),
        name="dil_attn",
    )(q16, k16, k16, k16, v16, v16, v16, kn, kn, kn, vn, vn, vn, band, unperm)


def _conv_rolls(buf):
    rows = buf.shape[0]
    return [buf if shift == 0 else pltpu.roll(buf, rows - shift, 0) for shift in range(SUBLANES)]


def _conv_rows(rolled, lo_row, n_rows, w_ref, b_ref, g_ref, beta_ref):
    acc = jnp.zeros((n_rows, CONV_W), F32) + b_ref[...]
    base = HALO - CONV_K // 2
    for tap in range(CONV_K):
        shift = (base + tap) % SUBLANES
        lo = lo_row + base + tap - shift
        acc = acc + w_ref[tap:tap + 1, :] * rolled[shift][lo:lo + n_rows]
    mu = jnp.mean(acc, axis=-1, keepdims=True)
    d = acc - mu
    var = jnp.mean(d * d, axis=-1, keepdims=True)
    z = d * lax.rsqrt(var + NORM_EPS) * g_ref[...] + beta_ref[...]
    return z * _sigmoid(z)


N_CAST = D_FF // FF_CH
UP_CAST = 2 * D_FF // N_CAST
DOWN_CAST = D_FF // N_CAST
OUT_CAST = LANES
assert D_MODEL // OUT_CAST <= N_CAST and N_CAST * UP_CAST == 2 * D_FF and UP_CAST % FF_CH == 0


def _out_ffn_kernel(a_ref, ap_ref, ax_ref, c_ref, cp_ref, cx_ref, m_ref, mp_ref, mx_ref,
                    x_ref, xp_ref, xx_ref, wo32_ref, g_ref, wu32_ref, dw_ref, db_ref, wd32_ref,
                    o_ref, gate_ref, wo_ref, wu_ref, wd_ref, *, n_tiles):
    step = pl.program_id(0)

    for c in range(N_CAST):
        @pl.when(step == c)
        def _(c=c):
            for half in range(UP_CAST // FF_CH):
                src = c * UP_CAST + half * FF_CH
                j, is_up = (src // FF_CH, 0) if src < D_FF else ((src - D_FF) // FF_CH, 1)
                dst = (2 * j + is_up) * FF_CH
                wu_ref[:, dst:dst + FF_CH] = wu32_ref[:, half * FF_CH:(half + 1) * FF_CH].astype(BF16)
            wd_ref[c * DOWN_CAST:(c + 1) * DOWN_CAST, :] = wd32_ref[...].astype(BF16)
            if (c + 1) * OUT_CAST <= D_MODEL:
                wo_ref[c * OUT_CAST:(c + 1) * OUT_CAST, :] = wo32_ref[...].astype(BF16)

    @pl.when(step >= N_CAST)
    def _():
        _out_ffn_tile(a_ref, ap_ref, ax_ref, c_ref, cp_ref, cx_ref, m_ref, mp_ref, mx_ref,
                      x_ref, xp_ref, xx_ref, wo_ref, g_ref, wu_ref, dw_ref, db_ref, wd_ref,
                      o_ref, gate_ref, lax.rem(step - N_CAST, n_tiles), n_tiles)


def _out_ffn_tile(a_ref, ap_ref, ax_ref, c_ref, cp_ref, cx_ref, m_ref, mp_ref, mx_ref,
                  x_ref, xp_ref, xx_ref, wo_ref, g_ref, wu_ref, dw_ref, db_ref, wd_ref,
                  o_ref, gate_ref, t, nt):
    has_next, has_prev = t < nt - 1, t > 0
    rows = TF + HALO

    def ext_rows(cur, nxt, prev):
        halo = jnp.concatenate([nxt[0, :SUBLANES].astype(F32), prev[0, SUBLANES:].astype(F32)],
                               axis=0)
        return jnp.concatenate([cur[0], halo.astype(cur.dtype)], axis=0)

    rid = lax.broadcasted_iota(jnp.int32, (rows, 1), 0)
    in_seq = ((rid < TF) | ((rid < TF + SUBLANES) & has_next)
              | ((rid >= TF + SUBLANES) & has_prev))
    mixed = jnp.concatenate([ext_rows(a_ref, ax_ref, ap_ref), ext_rows(c_ref, cx_ref, cp_ref),
                             ext_rows(m_ref, mx_ref, mp_ref)], axis=-1)
    mixed = jnp.where(in_seq, mixed, jnp.zeros_like(mixed))
    h = (jnp.where(in_seq, ext_rows(x_ref, xx_ref, xp_ref), 0.0)
         + jnp.dot(mixed, wo_ref[...], preferred_element_type=F32))
    ext = _rms_rows(h, g_ref[...]).astype(BF16)

    def gate_rows(f, dn, up, piece, lo_g, lo_u):
        def conv3(col, lo):
            w = dw_ref[:, lo:lo + FF_CH]
            return (w[0:1] * dn[piece, col] + w[1:2] * f[piece, col] + w[2:3] * up[piece, col]
                    + db_ref[:, lo:lo + FF_CH])

        fg, fu = conv3(slice(0, FF_CH), lo_g), conv3(slice(FF_CH, 2 * FF_CH), lo_u)
        gate_ref[piece, lo_g:lo_g + FF_CH] = (fg * _sigmoid(fg) * fu).astype(BF16)

    n_chunks = D_FF // FF_CH
    out = h[:TF]
    for j in range(n_chunks):
        lo_g, lo_u = j * FF_CH, D_FF + j * FF_CH
        f = jnp.dot(ext, wu_ref[:, 2 * lo_g:2 * lo_g + 2 * FF_CH], preferred_element_type=F32)
        if j == n_chunks - 1:
            out = out + jnp.dot(gate_ref[:, :lo_g], wd_ref[:lo_g, :], preferred_element_type=F32)
        dn, up = pltpu.roll(f, 1, 0), pltpu.roll(f, rows - 1, 0)
        for lo in range(0, TF, FFN_ROWS):
            gate_rows(f, dn, up, slice(lo, lo + FFN_ROWS), lo_g, lo_u)
    o_ref[0] = out + jnp.dot(gate_ref[:, D_FF - FF_CH:], wd_ref[D_FF - FF_CH:, :],
                             preferred_element_type=F32)


def _out_ffn(attn, cb, mo, x, w_out, g, w_up, dw_w, dw_b, w_down):
    B, S, _ = x.shape
    nt = S // TF
    per = TF // HALO

    def tile_of(step):
        i = jnp.maximum(step - N_CAST, 0)
        return i // nt, i % nt

    def with_halos(width):
        def cur(step):
            b, t = tile_of(step)
            return b, t, 0

        def prev(step):
            b, t = tile_of(step)
            return b, jnp.maximum(t * per - 1, 0), 0

        def nxt(step):
            b, t = tile_of(step)
            return b, jnp.minimum((t + 1) * per, nt * per - 1), 0

        return [pl.BlockSpec((1, TF, width), cur), pl.BlockSpec((1, HALO, width), prev),
                pl.BlockSpec((1, HALO, width), nxt)]

    cast_chunk = lambda limit: (lambda step: jnp.minimum(step, limit - 1))
    out_i, up_i, down_i = cast_chunk(D_MODEL // OUT_CAST), cast_chunk(N_CAST), cast_chunk(N_CAST)
    return pl.pallas_call(
        functools.partial(_out_ffn_kernel, n_tiles=nt),
        grid=(N_CAST + B * nt,),
        in_specs=with_halos(ATTN_W) + with_halos(CONV_W) + with_halos(MEM_W)
        + with_halos(D_MODEL)
        + [pl.BlockSpec((OUT_CAST, D_MODEL), lambda s: (out_i(s), 0)), _const_spec((1, D_MODEL)),
           pl.BlockSpec((D_MODEL, UP_CAST), lambda s: (0, up_i(s))),
           _const_spec((FFN_CONV_K, 2 * D_FF)), _const_spec((1, 2 * D_FF)),
           pl.BlockSpec((DOWN_CAST, D_MODEL), lambda s: (down_i(s), 0))],
        out_specs=pl.BlockSpec((1, TF, D_MODEL), lambda s: (*tile_of(s), 0)),
        out_shape=jax.ShapeDtypeStruct((B, S, D_MODEL), F32),
        scratch_shapes=[pltpu.VMEM((TF, D_FF), BF16), pltpu.VMEM((D_MODEL, D_MODEL), BF16),
                        pltpu.VMEM((D_MODEL, 2 * D_FF), BF16), pltpu.VMEM((D_FF, D_MODEL), BF16)],
        compiler_params=pltpu.CompilerParams(dimension_semantics=("arbitrary",),
                                             vmem_limit_bytes=VMEM_LIMIT),
        name="out_ffn",
    )(attn, attn, attn, cb, cb, cb, mo, mo, mo, x, x, x, w_out, g, w_up, dw_w, dw_b, w_down)


def _rope_tables():
    inv_freq = ROPE_THETA ** (-jnp.arange(0, ROT_DIM, 2, dtype=F32) / ROT_DIM)
    invf = jnp.broadcast_to(inv_freq[:, None], (SUBLANES, LANES))
    half = ROT_DIM // 2
    lane = np.arange(LANES) % HEAD_DIM
    lanes = np.zeros((SUBLANES, LANES), np.float32)
    lanes[0] = lane >= ROT_DIM
    lanes[1] = lane < half
    expand = np.zeros((LANES, 2 * LANES), np.float32)
    for l in range(LANES):
        if lane[l] < ROT_DIM:
            j = lane[l] % half
            expand[[j, half + j], l] = 1.0
            expand[[2 * half + j, 3 * half + j], LANES + l] = -1.0 if lane[l] < half else 1.0
    return invf, jnp.asarray(lanes), jnp.asarray(expand, BF16)


def _group_sum_matrix():
    idx = np.arange(MXU_DIM) // HEAD_DIM
    return jnp.asarray(idx[:, None] == idx[None, :], BF16)


def _plane_perm(n):
    out = np.arange(n)
    src = (out % (n // PLANES)) * PLANES + out // (n // PLANES)
    return np.asarray(src[:, None] == np.arange(n)[None, :], np.float32)


def kernel(x, mem, positions, mix_norm_g, mem_norm_g, w_in, w_mem_kv, q_norm_g, k_norm_g, mq_norm_g, mk_norm_g, conv_dw_w, conv_dw_b, conv_ln_g, conv_ln_b, w_out, ffn_norm_g, w_up, ffn_dw_w, ffn_dw_b, w_down):
    B, S, _ = x.shape
    depth = w_in.shape[0]
    pos_row = positions.reshape(B, 1, S)
    invf, lane_tab, expand = _rope_tables()
    gsum = _group_sum_matrix()
    perm = jnp.asarray(_plane_perm(TM), BF16)
    unperm = jnp.asarray(_plane_perm(TL).T, BF16)
    band = _band_tables()
    row = lambda a: a.reshape(1, -1)
    h = x
    for l in range(depth):
        kmt, vm = _mem_kv(mem, row(mem_norm_g[l]), w_mem_kv[l],
                          row(jnp.tile(mk_norm_g[l], MEM_HEADS)), gsum)
        q16, k16, v16, kn, vn, cb, mo = _in_proj(
            h, pos_row, row(mix_norm_g[l]), w_in[l],
            row(jnp.tile(q_norm_g[l], ATTN_HEADS)), row(jnp.tile(k_norm_g[l], ATTN_HEADS)),
            row(jnp.tile(mq_norm_g[l], MEM_HEADS)), gsum, invf, lane_tab, expand, perm,
            conv_dw_w[l], row(conv_dw_b[l]), row(conv_ln_g[l]), row(conv_ln_b[l]), kmt, vm)
        attn = _dil_attn(q16, k16, v16, kn, vn, band, unperm)
        h = _out_ffn(attn, cb, mo, h, w_out[l], row(ffn_norm_g[l]), w_up[l], ffn_dw_w[l],
                     row(ffn_dw_b[l]), w_down[l])
    return h
```
